```python
import jax, jax.numpy as jnp
from jax import lax
import numpy as np

D_MODEL = 2048
BATCH = 8
SEQ = 2048
DEPTH = 2
DEC_BATCH = 32
DEC_SEQ = 32
PAST_LEN = 1024

CHUNK = 64
EPS = 1e-6
N_EVEN = (DEPTH + 1) // 2
N_ODD = DEPTH // 2

D_CONV = D_MODEL // 2
CONV_WIDTH = 31
RET_HEADS = 8
RET_DK = 128
RET_DV = 128
D_RET = RET_HEADS * RET_DV
D_RET_QK = RET_HEADS * RET_DK
D_IN_EVEN = 2 * D_CONV + 2 * D_RET_QK + 2 * D_RET
D_MIX_EVEN = D_CONV + D_RET
EVEN_SPLITS = [D_CONV, 2 * D_CONV, 2 * D_CONV + D_RET_QK, 2 * D_CONV + 2 * D_RET_QK, 2 * D_CONV + 2 * D_RET_QK + D_RET]

ATT_HEADS = 16
ATT_KV_HEADS = 4
HEAD_DIM = D_MODEL // ATT_HEADS
IDX_HEADS = 16
IDX_DIM = 64
TOPK_MAX = 256
D_Q = ATT_HEADS * HEAD_DIM
D_KV = ATT_KV_HEADS * HEAD_DIM
D_QIDX = IDX_HEADS * IDX_DIM
D_IN_ODD = D_Q + 2 * D_KV + D_QIDX + IDX_DIM + IDX_HEADS
D_MIX_ODD = D_Q
ODD_SPLITS = [D_Q, D_Q + D_KV, D_Q + 2 * D_KV, D_Q + 2 * D_KV + D_QIDX, D_Q + 2 * D_KV + D_QIDX + IDX_DIM]

N_GROUPS = 4
EXPERTS_PER_GROUP = 8
N_EXPERTS = N_GROUPS * EXPERTS_PER_GROUP
TOP_K_INNER = 2
D_EXPERT = 512

kernel_name = "conv_retention_dsa_hmoe_stream_step"


def rms_norm(x, g):
    xf = x.astype(jnp.float32)
    y = xf * lax.rsqrt(jnp.mean(xf * xf, axis=-1, keepdims=True) + EPS)
    return (y * g.astype(jnp.float32)).astype(x.dtype)


def layer_norm(x, g, b):
    xf = x.astype(jnp.float32)
    mu = jnp.mean(xf, axis=-1, keepdims=True)
    d = xf - mu
    y = d * lax.rsqrt(jnp.mean(d * d, axis=-1, keepdims=True) + EPS)
    return (y * g.astype(jnp.float32) + b.astype(jnp.float32)).astype(x.dtype)


def modulate(x, g, shift, scale):
    return rms_norm(x, g) * (1.0 + scale[:, None, :]) + shift[:, None, :]


def alibi_slopes():
    return 2.0 ** (-8.0 * jnp.arange(1, ATT_HEADS + 1, dtype=jnp.float32) / ATT_HEADS)


def causal_depthwise_conv(u, buf, w, b):
    up = jnp.concatenate([buf, u], axis=1)
    y = lax.conv_general_dilated(up, w[:, None, :], window_strides=(1,), padding='VALID',
                                 dimension_numbers=('NWC', 'WIO', 'NWC'), feature_group_count=u.shape[-1])
    return y + b, up[:, -(CONV_WIDTH - 1):]


def retention(q, k, v, state0, block):
    f32 = jnp.float32
    B, L, H, dk = q.shape
    dv = v.shape[-1]
    n = L // block
    lg = jnp.log1p(-(2.0 ** (-5.0 - jnp.arange(H, dtype=f32))))
    pos = jnp.arange(block, dtype=f32)
    diff = pos[:, None] - pos[None, :]
    d_in = jnp.where(diff >= 0, jnp.exp(lg[:, None, None] * jnp.maximum(diff, 0.0)), 0.0)
    d_q = jnp.exp(lg[None, :] * (pos[:, None] + 1.0))
    d_k = jnp.exp(lg[:, None] * (block - 1.0 - pos[None, :]))
    d_blk = jnp.exp(lg * block)

    def to_blocks(t):
        return t.astype(f32).reshape(B, n, block, H, t.shape[-1]).swapaxes(0, 1)

    def step(S, inp):
        qb, kb, vb = inp
        sc = jnp.einsum('bihd,bjhd->bhij', qb, kb) * d_in[None]
        o = (jnp.einsum('bhij,bjhe->bihe', sc, vb)
             + jnp.einsum('bihd,bhde->bihe', qb, S) * d_q[None, :, :, None])
        S = S * d_blk[None, :, None, None] + jnp.einsum('bjhd,bjhe,hj->bhde', kb, vb, d_k)
        return S, o

    S, o = lax.scan(step, state0.astype(f32), (to_blocks(q), to_blocks(k), to_blocks(v)))
    o = o.swapaxes(0, 1).reshape(B, L, H, dv)
    return o.astype(v.dtype), S.astype(state0.dtype)


def conv_retention_mixer(h, conv_buf, ret_state, w_in, conv_w, conv_b, cn_g, cn_b, ret_g, w_out):
    B, L, _ = h.shape
    u = h @ w_in
    a_val, a_gate, q, k, v, g = jnp.split(u, EVEN_SPLITS, axis=-1)
    glu = a_val * jax.nn.sigmoid(a_gate)
    dc, new_buf = causal_depthwise_conv(glu, conv_buf, conv_w, conv_b)
    a_out = jax.nn.silu(layer_norm(dc, cn_g, cn_b))
    q = q.reshape(B, L, RET_HEADS, RET_DK)
    k = k.reshape(B, L, RET_HEADS, RET_DK) * (RET_DK ** -0.5)
    v = v.reshape(B, L, RET_HEADS, RET_DV)
    o, new_state = retention(q, k, v, ret_state, min(CHUNK, L))
    o = rms_norm(o, ret_g.reshape(RET_HEADS, RET_DV))
    b_out = o.reshape(B, L, D_RET) * jax.nn.silu(g)
    return jnp.concatenate([a_out, b_out], axis=-1) @ w_out, new_buf, new_state


def sparse_attend(q, q_idx, w_idx, q_pos, k_all, v_all, kidx_all, n_sel):
    f32 = jnp.float32
    B, T = q.shape[:2]
    S = k_all.shape[1]
    G = ATT_HEADS // ATT_KV_HEADS
    rel = jax.nn.relu(jnp.einsum('bthd,bsd->bths', q_idx, kidx_all).astype(f32))
    score = jnp.einsum('bths,bth->bts', rel, w_idx.astype(f32))
    k_pos = jnp.arange(S, dtype=jnp.int32)
    admissible = (k_pos[None, :] // CHUNK) <= (q_pos[:, None] // CHUNK)
    score = jnp.where(admissible[None], score, -jnp.inf)
    top_s, sel = lax.top_k(score, n_sel)
    valid = top_s > -jnp.inf
    gather = jax.vmap(lambda rows, ids: rows[ids])
    k_sel = gather(k_all, sel)
    v_sel = gather(v_all, sel)
    qg = q.reshape(B, T, ATT_KV_HEADS, G, HEAD_DIM)
    logits = jnp.einsum('btgrd,btkgd->btgrk', qg, k_sel).astype(f32) * (HEAD_DIM ** -0.5)
    dist = jnp.abs(q_pos[None, :, None] - sel).astype(f32)
    slopes = alibi_slopes().reshape(ATT_KV_HEADS, G)
    logits = logits - slopes[None, None, :, :, None] * dist[:, :, None, None, :]
    logits = jnp.where(valid[:, :, None, None, :], logits, -jnp.inf)
    p = jax.nn.softmax(logits, axis=-1).astype(v_all.dtype)
    o = jnp.einsum('btgrk,btkgd->btgrd', p, v_sel)
    return o.reshape(B, T, D_MIX_ODD)


def dsa_mixer(h, past_k, past_v, past_kidx, w_in, q_g, k_g, kidx_g, w_out):
    B, L, _ = h.shape
    past_len = past_k.shape[1]
    u = h @ w_in
    q, k, v, q_idx, k_idx, w_idx = jnp.split(u, ODD_SPLITS, axis=-1)
    q = rms_norm(q.reshape(B, L, ATT_HEADS, HEAD_DIM), q_g)
    k = rms_norm(k.reshape(B, L, ATT_KV_HEADS, HEAD_DIM), k_g)
    v = v.reshape(B, L, ATT_KV_HEADS, HEAD_DIM)
    q_idx = q_idx.reshape(B, L, IDX_HEADS, IDX_DIM)
    k_idx = rms_norm(k_idx, kidx_g)
    w_idx = w_idx * (IDX_HEADS ** -0.5 * IDX_DIM ** -0.5)
    k_all = jnp.concatenate([past_k, k], axis=1)
    v_all = jnp.concatenate([past_v, v], axis=1)
    kidx_all = jnp.concatenate([past_kidx, k_idx], axis=1)
    n_keys = past_len + L
    n_sel = min(TOPK_MAX, n_keys // 4)
    q_pos = past_len + jnp.arange(L, dtype=jnp.int32)
    blk = min(CHUNK, L)
    nb = L // blk

    def attend_block(args):
        qb, qib, wb, pb = args
        return sparse_attend(qb, qib, wb, pb, k_all, v_all, kidx_all, n_sel)

    xs = (q.reshape(B, nb, blk, ATT_HEADS, HEAD_DIM).swapaxes(0, 1),
          q_idx.reshape(B, nb, blk, IDX_HEADS, IDX_DIM).swapaxes(0, 1),
          w_idx.reshape(B, nb, blk, IDX_HEADS).swapaxes(0, 1),
          q_pos.reshape(nb, blk))
    o = lax.map(attend_block, xs)
    o = o.swapaxes(0, 1).reshape(B, L, D_MIX_ODD)
    return o @ w_out, k, v, k_idx


def hier_moe(h, w_group, b_group, w_erouter, b_erouter, w_gate, w_up, w_down):
    f32 = jnp.float32
    g_logits = (h @ w_group).astype(f32) + b_group.astype(f32)
    g_sel = jnp.argmax(g_logits, axis=-1)
    g_onehot = jax.nn.one_hot(g_sel, N_GROUPS, dtype=f32)
    g_w = jnp.sum(jax.nn.softmax(g_logits, axis=-1) * g_onehot, axis=-1, keepdims=True)
    e_logits = jnp.einsum('bld,gde->blge', h, w_erouter).astype(f32) + b_erouter.astype(f32)
    e_logits = jnp.einsum('blge,blg->ble', e_logits, g_onehot)
    top_v, top_i = lax.top_k(e_logits, TOP_K_INNER)
    top_w = jax.nn.softmax(top_v, axis=-1) * g_w
    expert_id = g_sel[..., None] * EXPERTS_PER_GROUP + top_i
    combine = jnp.einsum('blk,blke->ble', top_w, jax.nn.one_hot(expert_id, N_EXPERTS, dtype=f32))

    def add_expert(acc, ex):
        wg, wu, wd, cw = ex
        y = (jax.nn.silu(h @ wg) * (h @ wu)) @ wd
        return acc + cw[..., None].astype(h.dtype) * y, None

    out, _ = lax.scan(add_expert, jnp.zeros_like(h), (w_gate, w_up, w_down, jnp.moveaxis(combine, -1, 0)))
    return out


def trunk(x, c, conv_buf, ret_state, past_k, past_v, past_kidx,
          norm_mix_g, norm_ffn_g, w_ada, b_ada,
          cr_w_in, conv_w, conv_b, conv_norm_g, conv_norm_b, ret_norm_g, cr_w_out,
          dsa_w_in, q_norm_g, k_norm_g, kidx_norm_g, dsa_w_out,
          moe_w_group, moe_b_group, moe_w_erouter, moe_b_erouter, moe_w_gate, moe_w_up, moe_w_down):
    new_conv, new_ret, new_k, new_v, new_kidx = [], [], [], [], []
    c_act = jax.nn.silu(c)
    for i in range(DEPTH):
        ada = c_act @ w_ada[i] + b_ada[i]
        sh1, sc1, g1, sh2, sc2, g2 = jnp.split(ada, 6, axis=-1)
        h = modulate(x, norm_mix_g[i], sh1, sc1)
        j = i // 2
        if i % 2 == 0:
            y, nbuf, nst = conv_retention_mixer(h, conv_buf[j], ret_state[j], cr_w_in[j], conv_w[j], conv_b[j],
                                                conv_norm_g[j], conv_norm_b[j], ret_norm_g[j], cr_w_out[j])
            new_conv.append(nbuf)
            new_ret.append(nst)
        else:
            y, nk, nv, nki = dsa_mixer(h, past_k[j], past_v[j], past_kidx[j], dsa_w_in[j],
                                       q_norm_g[j], k_norm_g[j], kidx_norm_g[j], dsa_w_out[j])
            new_k.append(nk)
            new_v.append(nv)
            new_kidx.append(nki)
        x = x + g1[:, None, :] * y
        h = modulate(x, norm_ffn_g[i], sh2, sc2)
        x = x + g2[:, None, :] * hier_moe(h, moe_w_group[i], moe_b_group[i], moe_w_erouter[i], moe_b_erouter[i],
                                          moe_w_gate[i], moe_w_up[i], moe_w_down[i])
    return x, jnp.stack(new_conv), jnp.stack(new_ret), jnp.stack(new_k), jnp.stack(new_v), jnp.stack(new_kidx)


def setup_inputs(seed: int = 0) -> dict:
    key = jax.random.key(seed)
    ks = jax.random.split(key, 40)
    f32 = jnp.float32

    def nrm(k, shape, scale):
        return jax.random.normal(k, shape, f32) * scale

    def gain(k, shape):
        return 1.0 + 0.05 * jax.random.normal(k, shape, f32)

    D = D_MODEL
    return {
        'x_prompt': nrm(ks[0], (BATCH, SEQ, D), 1.0),
        'x_sample': nrm(ks[1], (DEC_BATCH, DEC_SEQ, D), 1.0),
        'c_prompt': nrm(ks[2], (BATCH, D), 1.0),
        'c_sample': nrm(ks[3], (DEC_BATCH, D), 1.0),
        'cache_conv': nrm(ks[4], (N_EVEN, DEC_BATCH, CONV_WIDTH - 1, D_CONV), 0.5),
        'state_ret': nrm(ks[5], (N_EVEN, DEC_BATCH, RET_HEADS, RET_DK, RET_DV), 0.5),
        'cache_k': nrm(ks[6], (N_ODD, DEC_BATCH, PAST_LEN, ATT_KV_HEADS, HEAD_DIM), 1.0),
        'cache_v': nrm(ks[7], (N_ODD, DEC_BATCH, PAST_LEN, ATT_KV_HEADS, HEAD_DIM), 1.0),
        'cache_kidx': nrm(ks[8], (N_ODD, DEC_BATCH, PAST_LEN, IDX_DIM), 1.0),
        'norm_mix_g': gain(ks[9], (DEPTH, D)),
        'norm_ffn_g': gain(ks[10], (DEPTH, D)),
        'w_ada': nrm(ks[11], (DEPTH, D, 6 * D), 0.5 * D ** -0.5),
        'b_ada': nrm(ks[12], (DEPTH, 6 * D), 0.02),
        'cr_w_in': nrm(ks[13], (N_EVEN, D, D_IN_EVEN), D ** -0.5),
        'conv_w': nrm(ks[14], (N_EVEN, CONV_WIDTH, D_CONV), CONV_WIDTH ** -0.5),
        'conv_b': nrm(ks[15], (N_EVEN, D_CONV), 0.02),
        'conv_norm_g': gain(ks[16], (N_EVEN, D_CONV)),
        'conv_norm_b': nrm(ks[17], (N_EVEN, D_CONV), 0.02),
        'ret_norm_g': gain(ks[18], (N_EVEN, D_RET)),
        'cr_w_out': nrm(ks[19], (N_EVEN, D_MIX_EVEN, D), D_MIX_EVEN ** -0.5),
        'dsa_w_in': nrm(ks[20], (N_ODD, D, D_IN_ODD), D ** -0.5),
        'q_norm_g': gain(ks[21], (N_ODD, HEAD_DIM)),
        'k_norm_g': gain(ks[22], (N_ODD, HEAD_DIM)),
        'kidx_norm_g': gain(ks[23], (N_ODD, IDX_DIM)),
        'dsa_w_out': nrm(ks[24], (N_ODD, D_MIX_ODD, D), D_MIX_ODD ** -0.5),
        'moe_w_group': nrm(ks[25], (DEPTH, D, N_GROUPS), D ** -0.5),
        'moe_b_group': nrm(ks[26], (DEPTH, N_GROUPS), 0.01),
        'moe_w_erouter': nrm(ks[27], (DEPTH, N_GROUPS, D, EXPERTS_PER_GROUP), D ** -0.5),
        'moe_b_erouter': nrm(ks[28], (DEPTH, N_GROUPS, EXPERTS_PER_GROUP), 0.01),
        'moe_w_gate': nrm(ks[29], (DEPTH, N_EXPERTS, D, D_EXPERT), D ** -0.5),
        'moe_w_up': nrm(ks[30], (DEPTH, N_EXPERTS, D, D_EXPERT), D ** -0.5),
        'moe_w_down': nrm(ks[31], (DEPTH, N_EXPERTS, D_EXPERT, D), D_EXPERT ** -0.5),
    }


def reference(x_prompt, x_sample, c_prompt, c_sample, cache_conv, state_ret, cache_k, cache_v, cache_kidx,
              norm_mix_g, norm_ffn_g, w_ada, b_ada,
              cr_w_in, conv_w, conv_b, conv_norm_g, conv_norm_b, ret_norm_g, cr_w_out,
              dsa_w_in, q_norm_g, k_norm_g, kidx_norm_g, dsa_w_out,
              moe_w_group, moe_b_group, moe_w_erouter, moe_b_erouter, moe_w_gate, moe_w_up, moe_w_down):
    params = (norm_mix_g, norm_ffn_g, w_ada, b_ada,
              cr_w_in, conv_w, conv_b, conv_norm_g, conv_norm_b, ret_norm_g, cr_w_out,
              dsa_w_in, q_norm_g, k_norm_g, kidx_norm_g, dsa_w_out,
              moe_w_group, moe_b_group, moe_w_erouter, moe_b_erouter, moe_w_gate, moe_w_up, moe_w_down)
    B = x_prompt.shape[0]
    zero_conv = jnp.zeros((N_EVEN, B, CONV_WIDTH - 1, D_CONV), x_prompt.dtype)
    zero_ret = jnp.zeros((N_EVEN, B, RET_HEADS, RET_DK, RET_DV), state_ret.dtype)
    empty_kv = jnp.zeros((N_ODD, B, 0, ATT_KV_HEADS, HEAD_DIM), x_prompt.dtype)
    empty_kidx = jnp.zeros((N_ODD, B, 0, IDX_DIM), x_prompt.dtype)
    y_prompt, conv_p, ret_p, k_p, v_p, kidx_p = trunk(x_prompt, c_prompt, zero_conv, zero_ret,
                                                      empty_kv, empty_kv, empty_kidx, *params)
    y_sample, conv_s, ret_s, k_s, v_s, kidx_s = trunk(x_sample, c_sample, cache_conv, state_ret,
                                                      cache_k, cache_v, cache_kidx, *params)
    return (y_prompt, y_sample, conv_p, conv_s, ret_p, ret_s, k_p, k_s, v_p, v_s, kidx_p, kidx_s)
```

```python
import functools
import math

import jax
import jax.numpy as jnp
from jax import lax
from jax.experimental import pallas as pl
from jax.experimental.pallas import tpu as pltpu

F32 = jnp.float32
BF16 = jnp.bfloat16

EPS = 1e-6
CHUNK = 64
ADA_BLOCK = 32
CONV_WIDTH = 31
RET_HEADS = 8
RET_DK = 128
ATT_HEADS = 16
ATT_KV_HEADS = 4
IDX_HEADS = 16
IDX_DIM = 64
TOPK_MAX = 256
N_GROUPS = 4
EXPERTS_PER_GROUP = 8
N_EXPERTS = N_GROUPS * EXPERTS_PER_GROUP
LANES = 128
VMEM_LIMIT = 56 * 1024 * 1024
BISECT_ITERS = 30
MASKED_DIST = 1e30
EXPERT_TILE = 256
GATHER_CHUNK = 256
ROUTER_TILE = 512
COMBINE_TILE = 256


def _params(sem, vmem=VMEM_LIMIT):
    return pltpu.CompilerParams(dimension_semantics=sem, vmem_limit_bytes=vmem)


def _pick(n, cands):
    for c in cands:
        if n % c == 0:
            return c
    raise ValueError(f"no tile in {cands} divides {n}")


def _dot(a, b):
    return jnp.dot(a, b, preferred_element_type=F32)


def _dot_nt(a, b):
    return lax.dot_general(a, b, (((1,), (1,)), ((), ())), preferred_element_type=F32)


def _dot_tn(a, b):
    return lax.dot_general(a, b, (((0,), (0,)), ((), ())), preferred_element_type=F32)


def _silu(x):
    return x * jax.nn.sigmoid(x)


def _ada_kernel(c_ref, w_ref, b_ref, o_ref):
    c = _silu(c_ref[...]).astype(BF16)
    o_ref[...] = _dot(c, w_ref[...].astype(BF16)) + b_ref[...]


def _ada(c_all, w_ada, b_ada):
    depth, d, n = w_ada.shape
    rows = c_all.shape[0]
    tn = _pick(n, (1024, 512, 256, 128))
    return pl.pallas_call(
        _ada_kernel,
        grid=(depth, n // tn),
        in_specs=[pl.BlockSpec((rows, d), lambda l, j: (0, 0)),
                  pl.BlockSpec((None, d, tn), lambda l, j: (l, 0, j)),
                  pl.BlockSpec((None, 1, tn), lambda l, j: (l, 0, j))],
        out_specs=pl.BlockSpec((None, rows, tn), lambda l, j: (l, 0, j)),
        out_shape=jax.ShapeDtypeStruct((depth, rows, n), F32),
        compiler_params=_params(("parallel", "parallel")),
        name="ada",
    )(c_all, w_ada, b_ada.reshape(depth, 1, n))


def _modulate_rows(x_ref, shift_ref, scale_ref, g_ref, store):
    nblk = x_ref.shape[0] // ADA_BLOCK

    def body(r, carry):
        rows = pl.ds(pl.multiple_of(r * ADA_BLOCK, ADA_BLOCK), ADA_BLOCK)
        x = x_ref[rows, :]
        y = x * lax.rsqrt(jnp.mean(x * x, axis=-1, keepdims=True) + EPS) * g_ref[...]
        y = y * (1.0 + scale_ref[pl.ds(r, 1), :]) + shift_ref[pl.ds(r, 1), :]
        store(rows, y)
        return carry

    lax.fori_loop(0, nblk, body, 0)


def _modmm_kernel(x_ref, shift_ref, scale_ref, g_ref, w_ref, o_ref, h_ref):
    @pl.when(pl.program_id(1) == 0)
    def _():
        def store(rows, y):
            h_ref[rows, :] = y.astype(BF16)
        _modulate_rows(x_ref, shift_ref, scale_ref, g_ref, store)

    o_ref[...] = _dot(h_ref[...], w_ref[...])


def _modulated_matmul(x, shift, scale, g, w, tm, name):
    t, d = x.shape
    n = w.shape[1]
    tn = _pick(n, (1024, 512, 256, 128))
    nb = tm // ADA_BLOCK
    return pl.pallas_call(
        _modmm_kernel,
        grid=(t // tm, n // tn),
        in_specs=[pl.BlockSpec((tm, d), lambda i, j: (i, 0)),
                  pl.BlockSpec((nb, d), lambda i, j: (i, 0)),
                  pl.BlockSpec((nb, d), lambda i, j: (i, 0)),
                  pl.BlockSpec((1, d), lambda i, j: (0, 0)),
                  pl.BlockSpec((d, tn), lambda i, j: (0, j))],
        out_specs=pl.BlockSpec((tm, tn), lambda i, j: (i, j)),
        out_shape=jax.ShapeDtypeStruct((t, n), F32),
        scratch_shapes=[pltpu.VMEM((tm, d), BF16)],
        compiler_params=_params(("parallel", "arbitrary")),
        name=name,
    )(x, shift, scale, g.reshape(1, d), w)


def _outproj_kernel(n_in, *refs):
    a_refs = refs[:n_in]
    w_refs = refs[n_in:2 * n_in]
    x_ref, gate_ref, o_ref, y_ref = refs[2 * n_in:]
    y = _dot(a_refs[0][...], w_refs[0][...])
    for a_ref, w_ref in zip(a_refs[1:], w_refs[1:]):
        y = y + _dot(a_ref[...], w_ref[...])
    y_ref[...] = y
    nblk = x_ref.shape[0] // ADA_BLOCK

    def body(r, carry):
        rows = pl.ds(pl.multiple_of(r * ADA_BLOCK, ADA_BLOCK), ADA_BLOCK)
        o_ref[rows, :] = x_ref[rows, :] + gate_ref[pl.ds(r, 1), :] * y_ref[rows, :]
        return carry

    lax.fori_loop(0, nblk, body, 0)


def _outproj_residual(acts, ws, x, gate, tm, name):
    t, d = x.shape
    tn = _pick(d, (1024, 512, 256, 128))
    nb = tm // ADA_BLOCK
    n_in = len(acts)
    in_specs = ([pl.BlockSpec((tm, a.shape[1]), lambda i, j: (i, 0)) for a in acts]
                + [pl.BlockSpec((w.shape[0], tn), lambda i, j: (0, j)) for w in ws]
                + [pl.BlockSpec((tm, tn), lambda i, j: (i, j)),
                   pl.BlockSpec((nb, tn), lambda i, j: (i, j))])
    return pl.pallas_call(
        functools.partial(_outproj_kernel, n_in),
        grid=(t // tm, d // tn),
        in_specs=in_specs,
        out_specs=pl.BlockSpec((tm, tn), lambda i, j: (i, j)),
        out_shape=jax.ShapeDtypeStruct((t, d), F32),
        scratch_shapes=[pltpu.VMEM((tm, tn), F32)],
        compiler_params=_params(("parallel", "parallel")),
        name=name,
    )(*acts, *ws, x, gate)


CONV_ROWS = 32
CONV_HIST = 32


def _conv_kernel(tl, val_ref, gate_ref, pval_ref, pgate_ref, buf_ref, w_ref, b_ref, ng_ref, nb_ref,
                 o_ref, nbuf_ref, up_ref):
    li = pl.program_id(1)
    hist = CONV_WIDTH - 1
    pad = CONV_HIST - hist
    glu = val_ref[...] * jax.nn.sigmoid(gate_ref[...])
    up_ref[CONV_HIST:CONV_HIST + tl, :] = glu

    @pl.when(li == 0)
    def _():
        up_ref[pad:CONV_HIST, :] = buf_ref[...]

    @pl.when(li > 0)
    def _():
        prev = pval_ref[...] * jax.nn.sigmoid(pgate_ref[...])
        up_ref[pad:CONV_HIST, :] = prev[pad:, :]

    for c in range(tl // CONV_ROWS):
        r0 = c * CONV_ROWS
        acc = jnp.zeros((CONV_ROWS, val_ref.shape[1]), F32)
        for j in range(CONV_WIDTH):
            acc = acc + up_ref[r0 + pad + j:r0 + pad + j + CONV_ROWS, :] * w_ref[j:j + 1, :]
        acc = acc + b_ref[...]
        mu = jnp.mean(acc, axis=-1, keepdims=True)
        dlt = acc - mu
        y = dlt * lax.rsqrt(jnp.mean(dlt * dlt, axis=-1, keepdims=True) + EPS)
        y = y * ng_ref[...] + nb_ref[...]
        o_ref[r0:r0 + CONV_ROWS, :] = _silu(y).astype(o_ref.dtype)

    @pl.when(li == pl.num_programs(1) - 1)
    def _():
        nbuf_ref[...] = up_ref[CONV_HIST + tl - hist:CONV_HIST + tl, :]


def _conv_branch(u, row0, batch, seq, conv_buf, conv_w, conv_b, cn_g, cn_b, d_conv):
    tl = _pick(seq, (128, 64, 32))
    nl = seq // tl
    hist = CONV_WIDTH - 1
    rb = row0 // tl
    pb = tl // CONV_HIST
    cur = lambda col: pl.BlockSpec((tl, d_conv), lambda b, l: (rb + b * nl + l, col))
    prev = lambda col: pl.BlockSpec(
        (CONV_HIST, d_conv), lambda b, l: (jnp.maximum((rb + b * nl + l) * pb - 1, 0), col))
    vec = pl.BlockSpec((1, d_conv), lambda b, l: (0, 0))
    return pl.pallas_call(
        functools.partial(_conv_kernel, tl),
        grid=(batch, nl),
        in_specs=[cur(0), cur(1), prev(0), prev(1),
                  pl.BlockSpec((None, hist, d_conv), lambda b, l: (b, 0, 0)),
                  pl.BlockSpec((CONV_WIDTH, d_conv), lambda b, l: (0, 0)),
                  vec, vec, vec],
        out_specs=[pl.BlockSpec((tl, d_conv), lambda b, l: (b * nl + l, 0)),
                   pl.BlockSpec((None, hist, d_conv), lambda b, l: (b, 0, 0))],
        out_shape=[jax.ShapeDtypeStruct((batch * seq, d_conv), BF16),
                   jax.ShapeDtypeStruct((batch, hist, d_conv), F32)],
        scratch_shapes=[pltpu.VMEM((CONV_HIST + tl, d_conv), F32)],
        compiler_params=_params(("parallel", "arbitrary")),
        name="conv_branch",
    )(u, u, u, u, conv_buf, conv_w, conv_b.reshape(1, -1), cn_g.reshape(1, -1), cn_b.reshape(1, -1))


def _retention_kernel(q_ref, k_ref, v_ref, g_ref, s0_ref, din_ref, dq_ref, dk_ref, dblk_ref, rg_ref,
                      o_ref, s_out_ref, s_ref):
    ci = pl.program_id(1)

    @pl.when(ci == 0)
    def _():
        s_ref[...] = s0_ref[...]

    dv = s_ref.shape[2]
    for h in range(RET_HEADS):
        cols = slice(h * dv, (h + 1) * dv)
        q = q_ref[:, cols].astype(BF16)
        k = k_ref[:, cols] * (RET_DK ** -0.5)
        v = v_ref[:, cols].astype(BF16)
        s_prev = s_ref[h]
        sc = _dot_nt(q, k.astype(BF16)) * din_ref[h]
        o = _dot(sc.astype(BF16), v) + _dot(q, s_prev.astype(BF16)) * dq_ref[h]
        kd = (k * dk_ref[h]).astype(BF16)
        s_ref[h] = s_prev * dblk_ref[h] + _dot_tn(kd, v)
        o = o * lax.rsqrt(jnp.mean(o * o, axis=-1, keepdims=True) + EPS) * rg_ref[:, cols]
        o_ref[:, cols] = (o * _silu(g_ref[:, cols])).astype(o_ref.dtype)

    @pl.when(ci == pl.num_programs(1) - 1)
    def _():
        s_out_ref[...] = s_ref[...]


def _retention_branch(u, row0, batch, seq, state0, ret_g, col_q):
    heads, dk, dv = state0.shape[1:]
    d_ret = heads * dv
    c = _pick(seq, (256, 128, 64, 32))
    nc = seq // c
    rb = row0 // c
    lg = jnp.log1p(-(2.0 ** (-5.0 - jnp.arange(heads, dtype=F32))))
    pos = jnp.arange(c, dtype=F32)
    diff = pos[:, None] - pos[None, :]
    d_in = jnp.where(diff >= 0, jnp.exp(lg[:, None, None] * jnp.maximum(diff, 0.0)), 0.0)
    d_q = jnp.broadcast_to(jnp.exp(lg[:, None] * (pos[None, :] + 1.0))[:, :, None], (heads, c, dv))
    d_k = jnp.broadcast_to(jnp.exp(lg[:, None] * (c - 1.0 - pos[None, :]))[:, :, None], (heads, c, dk))
    d_blk = jnp.broadcast_to(jnp.exp(lg * c)[:, None, None], (heads, dk, dv))
    blk = lambda col: pl.BlockSpec((c, d_ret), lambda b, i: (rb + b * nc + i, col))
    const3 = lambda shape: pl.BlockSpec(shape, lambda b, i: (0, 0, 0))
    return pl.pallas_call(
        _retention_kernel,
        grid=(batch, nc),
        in_specs=[blk(col_q), blk(col_q + 1), blk(col_q + 2), blk(col_q + 3),
                  pl.BlockSpec((None, heads, dk, dv), lambda b, i: (b, 0, 0, 0)),
                  const3((heads, c, c)), const3((heads, c, dv)), const3((heads, c, dk)),
                  const3((heads, dk, dv)),
                  pl.BlockSpec((1, d_ret), lambda b, i: (0, 0))],
        out_specs=[pl.BlockSpec((c, d_ret), lambda b, i: (b * nc + i, 0)),
                   pl.BlockSpec((None, heads, dk, dv), lambda b, i: (b, 0, 0, 0))],
        out_shape=[jax.ShapeDtypeStruct((batch * seq, d_ret), BF16),
                   jax.ShapeDtypeStruct(state0.shape, F32)],
        scratch_shapes=[pltpu.VMEM((heads, dk, dv), F32)],
        compiler_params=_params(("parallel", "arbitrary")),
        name="retention_branch",
    )(u, u, u, u, state0, d_in, d_q, d_k, d_blk, ret_g.reshape(1, d_ret))


def _head_rms(x, g, hd):
    outs = []
    for h in range(x.shape[1] // hd):
        xh = x[:, h * hd:(h + 1) * hd]
        outs.append(xh * lax.rsqrt(jnp.mean(xh * xh, axis=-1, keepdims=True) + EPS) * g)
    return outs


def _qknorm_kernel(q_ref, k_ref, ki_ref, qg_ref, kg_ref, kig_ref, qo_ref, ko_ref, kio_ref):
    hd = qg_ref.shape[1]
    for h, qh in enumerate(_head_rms(q_ref[...], qg_ref[...], hd)):
        qo_ref[:, h * hd:(h + 1) * hd] = qh.astype(qo_ref.dtype)
    for h, kh in enumerate(_head_rms(k_ref[...], kg_ref[...], hd)):
        ko_ref[:, h * hd:(h + 1) * hd] = kh
    ki = ki_ref[:, :IDX_DIM]
    kio_ref[...] = ki * lax.rsqrt(jnp.mean(ki * ki, axis=-1, keepdims=True) + EPS) * kig_ref[...]


def _qk_norms(u, ux, q_g, k_g, kidx_g, tm, d_q, d_kv):
    t = u.shape[0]
    hd = q_g.shape[0]
    return pl.pallas_call(
        _qknorm_kernel,
        grid=(t // tm,),
        in_specs=[pl.BlockSpec((tm, d_q), lambda i: (i, 0)),
                  pl.BlockSpec((tm, d_kv), lambda i: (i, d_q // d_kv)),
                  pl.BlockSpec((tm, LANES), lambda i: (i, 0)),
                  pl.BlockSpec((1, hd), lambda i: (0, 0)),
                  pl.BlockSpec((1, hd), lambda i: (0, 0)),
                  pl.BlockSpec((1, IDX_DIM), lambda i: (0, 0))],
        out_specs=[pl.BlockSpec((tm, d_q), lambda i: (i, 0)),
                   pl.BlockSpec((tm, d_kv), lambda i: (i, 0)),
                   pl.BlockSpec((tm, IDX_DIM), lambda i: (i, 0))],
        out_shape=[jax.ShapeDtypeStruct((t, d_q), BF16),
                   jax.ShapeDtypeStruct((t, d_kv), F32),
                   jax.ShapeDtypeStruct((t, IDX_DIM), F32)],
        compiler_params=_params(("parallel",)),
        name="qk_norms",
    )(u, u, ux, q_g.reshape(1, hd), k_g.reshape(1, hd), kidx_g.reshape(1, IDX_DIM))


def _dsa_kernel(tq, rep, kb, past, n_keys, n_sel, q_ref, qi_ref, wt_ref, slope_ref, tri_ref,
                k_ref, vt_ref, ki_ref, o_ref, sc_ref, dist_ref):
    t0 = pl.program_id(1) * tq
    hd = k_ref.shape[2] // ATT_KV_HEADS
    grp = ATT_HEADS // ATT_KV_HEADS
    wq = rep * tq
    n_adm_tile = jnp.minimum(((past + t0 + tq - 1) // CHUNK + 1) * CHUNK, n_keys)
    nkb = (n_adm_tile + kb - 1) // kb
    q_pos = past + t0 + lax.broadcasted_iota(jnp.int32, (1, wq), 1) % tq
    q_chunk = q_pos // CHUNK
    n_adm = jnp.minimum((q_chunk + 1) * CHUNK, n_keys).astype(F32)
    need = jnp.minimum(n_adm, float(n_sel))
    neg_inf = jnp.float32(-jnp.inf)
    w_t = wt_ref[...] * (IDX_HEADS ** -0.5 * IDX_DIM ** -0.5)

    def key_pos(i):
        return i * kb + lax.broadcasted_iota(jnp.int32, (kb, 1), 0)

    def admissible(i):
        kp = key_pos(i)
        return jnp.logical_and(kp // CHUNK <= q_chunk, kp < n_keys)

    def score_body(i, carry):
        lo, hi = carry
        ki = ki_ref[i]
        acc = jnp.zeros((kb, wq), F32)
        for h in range(IDX_HEADS):
            qi = jnp.concatenate([qi_ref[:, h * IDX_DIM:(h + 1) * IDX_DIM]] * rep, axis=0)
            rel = _dot_nt(ki, qi)
            acc = acc + jnp.maximum(rel, 0.0) * w_t[h:h + 1, :]
        adm = admissible(i)
        sc_ref[i] = jnp.where(adm, acc, neg_inf)
        lo = jnp.minimum(lo, jnp.min(jnp.where(adm, acc, jnp.inf), axis=0, keepdims=True))
        hi = jnp.maximum(hi, jnp.max(jnp.where(adm, acc, neg_inf), axis=0, keepdims=True))
        return lo, hi

    lo, hi = lax.fori_loop(0, nkb, score_body,
                           (jnp.full((1, wq), jnp.inf, F32), jnp.full((1, wq), neg_inf, F32)))

    def count(pred):
        def body(i, acc):
            return acc + jnp.sum(jnp.where(pred(sc_ref[i]), 1.0, 0.0), axis=0, keepdims=True)
        return lax.fori_loop(0, nkb, body, jnp.zeros((1, wq), F32))

    def bisect(_, carry):
        lo, hi, c_lo = carry
        mid = 0.5 * (lo + hi)
        c_mid = count(lambda s: s >= mid)
        ge = c_mid >= need
        return jnp.where(ge, mid, lo), jnp.where(ge, hi, mid), jnp.where(ge, c_mid, c_lo)

    lo, hi, c_lo = lax.fori_loop(0, BISECT_ITERS, bisect, (lo, hi, n_adm))

    def write_dist(i, sel):
        dist = jnp.abs(q_pos - key_pos(i)).astype(F32)
        dist_ref[i] = jnp.where(sel, dist, MASKED_DIST)

    resolved = jnp.max(c_lo - need) <= 0.0

    @pl.when(resolved)
    def _():
        def body(i, carry):
            write_dist(i, sc_ref[i] >= lo)
            return carry
        lax.fori_loop(0, nkb, body, 0)

    @pl.when(jnp.logical_not(resolved))
    def _():
        n_above = count(lambda s: s > hi)
        room = need - n_above

        def body(i, seen):
            s = sc_ref[i]
            above = s > hi
            band = jnp.logical_and(s >= lo, jnp.logical_not(above))
            band_f = jnp.where(band, 1.0, 0.0)
            rank = _dot(tri_ref[...], band_f.astype(BF16)) + seen
            write_dist(i, jnp.logical_or(above, jnp.logical_and(band, rank <= room)))
            return seen + jnp.sum(band_f, axis=0, keepdims=True)
        lax.fori_loop(0, nkb, body, jnp.zeros((1, wq), F32))

    scale = hd ** -0.5
    for g in range(ATT_KV_HEADS):
        qg = jnp.concatenate(
            [q_ref[:, (g * grp + r) * hd:(g * grp + r + 1) * hd] for r in range(grp)], axis=0)
        slope = slope_ref[g]

        def att_body(i, carry):
            m, l, acc = carry
            logits = _dot_nt(k_ref[i, :, g * hd:(g + 1) * hd], qg) * scale
            dist = dist_ref[i]
            z = logits - slope * jnp.concatenate([dist] * (grp // rep), axis=1)
            m_new = jnp.maximum(m, jnp.max(z, axis=0, keepdims=True))
            alpha = jnp.exp(m - m_new)
            p = jnp.exp(z - m_new)
            l = l * alpha + jnp.sum(p, axis=0, keepdims=True)
            acc = acc * alpha + _dot(vt_ref[i, g * hd:(g + 1) * hd, :], p.astype(BF16))
            return m_new, l, acc

        m, l, acc = lax.fori_loop(
            0, nkb, att_body,
            (jnp.full((1, grp * tq), neg_inf, F32), jnp.zeros((1, grp * tq), F32),
             jnp.zeros((hd, grp * tq), F32)))
        out = (acc / l).T
        for r in range(grp):
            o_ref[:, (g * grp + r) * hd:(g * grp + r + 1) * hd] = out[r * tq:(r + 1) * tq, :].astype(o_ref.dtype)


def _sparse_attention(q, qidx, widx, k_blk, vt_blk, ki_blk, row0, batch, seq, past, n_keys, kb):
    d_q = q.shape[1]
    nkb_all = k_blk.shape[1]
    d_kv = k_blk.shape[3]
    tq = _pick(seq, (128, 64, 32))
    nq = seq // tq
    rb = row0 // tq
    grp = ATT_HEADS // ATT_KV_HEADS
    rep = max(1, LANES // tq)
    wq = rep * tq
    n_sel = min(TOPK_MAX, n_keys // 4)
    widx_t = jnp.tile(jnp.swapaxes(widx.reshape(batch * nq, tq, IDX_HEADS), 1, 2), (1, 1, rep))
    slopes = 2.0 ** (-8.0 * jnp.arange(1, ATT_HEADS + 1, dtype=F32) / ATT_HEADS)
    slope_rows = jnp.repeat(slopes.reshape(ATT_KV_HEADS, grp), tq, axis=1).reshape(ATT_KV_HEADS, 1, grp * tq)
    tri = (jnp.arange(kb)[:, None] >= jnp.arange(kb)[None, :]).astype(BF16)
    return pl.pallas_call(
        functools.partial(_dsa_kernel, tq, rep, kb, past, n_keys, n_sel),
        grid=(batch, nq),
        in_specs=[pl.BlockSpec((tq, d_q), lambda b, i: (rb + b * nq + i, 0)),
                  pl.BlockSpec((tq, qidx.shape[1]), lambda b, i: (rb + b * nq + i, 0)),
                  pl.BlockSpec((None, IDX_HEADS, wq), lambda b, i: (b * nq + i, 0, 0)),
                  pl.BlockSpec((ATT_KV_HEADS, 1, grp * tq), lambda b, i: (0, 0, 0)),
                  pl.BlockSpec((kb, kb), lambda b, i: (0, 0)),
                  pl.BlockSpec((None, nkb_all, kb, d_kv), lambda b, i: (b, 0, 0, 0)),
                  pl.BlockSpec((None, nkb_all, d_kv, kb), lambda b, i: (b, 0, 0, 0)),
                  pl.BlockSpec((None, nkb_all, kb, IDX_DIM), lambda b, i: (b, 0, 0, 0))],
        out_specs=pl.BlockSpec((tq, d_q), lambda b, i: (b * nq + i, 0)),
        out_shape=jax.ShapeDtypeStruct((batch * seq, d_q), BF16),
        scratch_shapes=[pltpu.VMEM((nkb_all, kb, wq), F32), pltpu.VMEM((nkb_all, kb, wq), F32)],
        compiler_params=_params(("parallel", "arbitrary")),
        name="sparse_attention",
    )(q, qidx, widx_t, slope_rows, tri, k_blk, vt_blk, ki_blk)


META_E, META_W, META_R = 0, 2, 4


def _router_kernel(x_ref, shift_ref, scale_ref, g_ref, wr_ref, br_ref, tri_ref,
                   h_ref, meta_ref, cnt_ref, carry_ref):
    @pl.when(pl.program_id(0) == 0)
    def _():
        carry_ref[...] = jnp.zeros_like(carry_ref)

    def store(rows, y):
        h_ref[rows, :] = y
    _modulate_rows(x_ref, shift_ref, scale_ref, g_ref, store)

    logits = jnp.dot(h_ref[...], wr_ref[...], preferred_element_type=F32,
                     precision=lax.Precision.HIGHEST) + br_ref[...]
    tm = logits.shape[0]
    lane = lax.broadcasted_iota(jnp.int32, (tm, LANES), 1).astype(F32)
    neg_inf = jnp.float32(-jnp.inf)

    def first_argmax(v):
        top = jnp.max(v, axis=-1, keepdims=True)
        return top, jnp.min(jnp.where(v == top, lane, float(LANES)), axis=-1, keepdims=True)

    is_group = lane < N_GROUPS
    gl = jnp.where(is_group, logits, neg_inf)
    g_top, g_sel = first_argmax(gl)
    g_w = 1.0 / jnp.sum(jnp.where(is_group, jnp.exp(gl - g_top), 0.0), axis=-1, keepdims=True)
    first = N_GROUPS + g_sel * EXPERTS_PER_GROUP
    el = jnp.where(jnp.logical_and(lane >= first, lane < first + EXPERTS_PER_GROUP), logits, neg_inf)
    v1, i1 = first_argmax(el)
    v2, i2 = first_argmax(jnp.where(lane == i1, neg_inf, el))
    e21 = jnp.exp(v2 - v1)
    w1 = g_w / (1.0 + e21)
    w2 = g_w * e21 / (1.0 + e21)
    e1 = i1 - N_GROUPS
    e2 = i2 - N_GROUPS

    oh1 = jnp.where(lane == e1, 1.0, 0.0)
    oh2 = jnp.where(lane == e2, 1.0, 0.0)
    both = oh1 + oh2
    before = _dot(tri_ref[...], both.astype(BF16)) + carry_ref[...]
    r1 = jnp.sum(before * oh1, axis=-1, keepdims=True)
    r2 = jnp.sum(before * oh2, axis=-1, keepdims=True)
    carry_ref[...] += jnp.sum(both, axis=0, keepdims=True)

    meta = jnp.zeros((tm, LANES), F32)
    for ln, val in ((META_E, e1), (META_E + 1, e2), (META_W, w1), (META_W + 1, w2),
                    (META_R, r1), (META_R + 1, r2)):
        meta = jnp.where(lane == ln, val, meta)
    meta_ref[...] = meta
    cnt_ref[...] = carry_ref[...]


def _router(x, shift, scale, g, w_router, b_router, tm):
    t, d = x.shape
    nb = tm // ADA_BLOCK
    tri = (jnp.arange(tm)[:, None] > jnp.arange(tm)[None, :]).astype(BF16)
    return pl.pallas_call(
        _router_kernel,
        grid=(t // tm,),
        in_specs=[pl.BlockSpec((tm, d), lambda i: (i, 0)),
                  pl.BlockSpec((nb, d), lambda i: (i, 0)),
                  pl.BlockSpec((nb, d), lambda i: (i, 0)),
                  pl.BlockSpec((1, d), lambda i: (0, 0)),
                  pl.BlockSpec((d, LANES), lambda i: (0, 0)),
                  pl.BlockSpec((1, LANES), lambda i: (0, 0)),
                  pl.BlockSpec((tm, tm), lambda i: (0, 0))],
        out_specs=[pl.BlockSpec((tm, d), lambda i: (i, 0)),
                   pl.BlockSpec((tm, LANES), lambda i: (i, 0)),
                   pl.BlockSpec((1, LANES), lambda i: (0, 0))],
        out_shape=[jax.ShapeDtypeStruct((t, d), F32),
                   jax.ShapeDtypeStruct((t, LANES), F32),
                   jax.ShapeDtypeStruct((1, LANES), F32)],
        scratch_shapes=[pltpu.VMEM((1, LANES), F32)],
        compiler_params=_params(("arbitrary",)),
        name="moe_router",
    )(x, shift, scale, g.reshape(1, d), w_router, b_router, tri)


def _row_copy(table_ref, out_ref, sem, src_row, dst_row):
    return pltpu.make_async_copy(table_ref.at[pl.ds(src_row, 1)], out_ref.at[pl.ds(dst_row, 1)], sem)


def _gather_kernel(idx_ref, table_ref, out_ref, sems):
    i = pl.program_id(0)
    n = pl.num_programs(0)
    ch = idx_ref.shape[2]

    def start(r, carry):
        _row_copy(table_ref, out_ref, sems.at[i % 2], idx_ref[0, 0, r], i * ch + r).start()
        return carry
    lax.fori_loop(0, ch, start, 0)

    def drain(slot):
        def wait(r, carry):
            _row_copy(table_ref, out_ref, sems.at[slot], 0, 0).wait()
            return carry
        lax.fori_loop(0, ch, wait, 0)

    @pl.when(i > 0)
    def _():
        drain((i + 1) % 2)

    @pl.when(i == n - 1)
    def _():
        drain(i % 2)


def _gather_rows(table, idx):
    p = idx.shape[0]
    ch = GATHER_CHUNK
    return pl.pallas_call(
        _gather_kernel,
        grid=(p // ch,),
        in_specs=[pl.BlockSpec((1, 1, ch), lambda i: (i, 0, 0), memory_space=pltpu.SMEM),
                  pl.BlockSpec(memory_space=pl.ANY)],
        out_specs=pl.BlockSpec(memory_space=pl.ANY),
        out_shape=jax.ShapeDtypeStruct((p, table.shape[1]), table.dtype),
        scratch_shapes=[pltpu.SemaphoreType.DMA((2,))],
        compiler_params=_params(("arbitrary",)),
        name="gather_rows",
    )(idx.reshape(p // ch, 1, ch), table)


def _expert_kernel(te_ref, x_ref, wg_ref, wu_ref, wd_ref, o_ref):
    del te_ref
    x = x_ref[...].astype(BF16)
    a = _dot(x, wg_ref[...])
    b = _dot(x, wu_ref[...])
    o_ref[...] = _dot((_silu(a) * b).astype(BF16), wd_ref[...])


def _expert_mlp(xs, tile_expert, wg, wu, wd):
    p, d = xs.shape
    de = wg.shape[2]
    tm = EXPERT_TILE
    return pl.pallas_call(
        _expert_kernel,
        grid_spec=pltpu.PrefetchScalarGridSpec(
            num_scalar_prefetch=1,
            grid=(p // tm,),
            in_specs=[pl.BlockSpec((tm, d), lambda i, te: (i, 0)),
                      pl.BlockSpec((None, d, de), lambda i, te: (te[i], 0, 0)),
                      pl.BlockSpec((None, d, de), lambda i, te: (te[i], 0, 0)),
                      pl.BlockSpec((None, de, d), lambda i, te: (te[i], 0, 0))],
            out_specs=pl.BlockSpec((tm, d), lambda i, te: (i, 0))),
        out_shape=jax.ShapeDtypeStruct((p, d), F32),
        compiler_params=_params(("arbitrary",)),
        name="moe_experts",
    )(tile_expert, xs, wg, wu, wd)


def _combine_kernel(x_ref, gate_ref, meta_ref, y_ref, o_ref):
    nblk = x_ref.shape[0] // ADA_BLOCK

    def body(r, carry):
        rows = pl.ds(pl.multiple_of(r * ADA_BLOCK, ADA_BLOCK), ADA_BLOCK)
        meta = meta_ref[rows, :]
        y = meta[:, META_W:META_W + 1] * y_ref[0, rows, :] + meta[:, META_W + 1:META_W + 2] * y_ref[1, rows, :]
        o_ref[rows, :] = x_ref[rows, :] + gate_ref[pl.ds(r, 1), :] * y
        return carry

    lax.fori_loop(0, nblk, body, 0)


def _combine(x, gate, meta, yk, tm):
    t, d = x.shape
    nb = tm // ADA_BLOCK
    return pl.pallas_call(
        _combine_kernel,
        grid=(t // tm,),
        in_specs=[pl.BlockSpec((tm, d), lambda i: (i, 0)),
                  pl.BlockSpec((nb, d), lambda i: (i, 0)),
                  pl.BlockSpec((tm, LANES), lambda i: (i, 0)),
                  pl.BlockSpec((2, tm, d), lambda i: (0, i, 0))],
        out_specs=pl.BlockSpec((tm, d), lambda i: (i, 0)),
        out_shape=jax.ShapeDtypeStruct((t, d), F32),
        compiler_params=_params(("parallel",)),
        name="moe_combine",
    )(x, gate, meta, yk)


def _hier_moe(x, shift, scale, gate, g, w_group, b_group, w_er, b_er, wg, wu, wd, tm):
    t, d = x.shape
    w_router = jnp.concatenate([w_group, jnp.moveaxis(w_er, 0, 1).reshape(d, N_EXPERTS)], axis=1)
    b_router = jnp.concatenate([b_group, b_er.reshape(N_EXPERTS)])
    n_route = N_GROUPS + N_EXPERTS
    w_router = jnp.pad(w_router, ((0, 0), (0, LANES - n_route)))
    b_router = jnp.pad(b_router, (0, LANES - n_route)).reshape(1, LANES)
    h, meta, counts = _router(x, shift, scale, g, w_router, b_router, min(tm, ROUTER_TILE))

    tile = EXPERT_TILE
    n_tiles = pl.cdiv(2 * t + N_EXPERTS * (tile - 1), tile)
    n_slots = n_tiles * tile
    expert = meta[:, META_E:META_E + 2].astype(jnp.int32)
    rank = meta[:, META_R:META_R + 2].astype(jnp.int32)
    cnt = counts[0, :N_EXPERTS].astype(jnp.int32)
    padded = (cnt + tile - 1) // tile * tile
    ends = jnp.cumsum(padded)
    pos = (ends - padded)[expert] + rank
    src = jnp.zeros((n_slots,), jnp.int32).at[pos.reshape(-1)].set(jnp.repeat(jnp.arange(t, dtype=jnp.int32), 2))
    tile_expert = jnp.minimum(
        jnp.searchsorted(ends, jnp.arange(n_tiles, dtype=jnp.int32) * tile, side="right"), N_EXPERTS - 1
    ).astype(jnp.int32)

    xs = _gather_rows(h, src)
    ys = _expert_mlp(xs, tile_expert, wg, wu, wd)
    yk = _gather_rows(ys, pos.T.reshape(-1)).reshape(2, t, d)
    return _combine(x, gate, meta, yk, min(tm, COMBINE_TILE))


def _key_blocks(k, vt_src, ki, n_pad, kb):
    b, s, _ = k.shape
    padk = lambda a: jnp.pad(a.astype(BF16), ((0, 0), (0, n_pad - s), (0, 0)))
    nkb = n_pad // kb
    k_blk = padk(k).reshape(b, nkb, kb, k.shape[2])
    vt_blk = jnp.swapaxes(padk(vt_src).reshape(b, nkb, kb, vt_src.shape[2]), 2, 3)
    ki_blk = padk(ki).reshape(b, nkb, kb, ki.shape[2])
    return k_blk, vt_blk, ki_blk


def kernel(x_prompt, x_sample, c_prompt, c_sample, cache_conv, state_ret, cache_k, cache_v, cache_kidx, norm_mix_g, norm_ffn_g, w_ada, b_ada, cr_w_in, conv_w, conv_b, conv_norm_g, conv_norm_b, ret_norm_g, cr_w_out, dsa_w_in, q_norm_g, k_norm_g, kidx_norm_g, dsa_w_out, moe_w_group, moe_b_group, moe_w_erouter, moe_b_erouter, moe_w_gate, moe_w_up, moe_w_down):
    bp, lp, d = x_prompt.shape
    bs, ls, _ = x_sample.shape
    tp, ts = bp * lp, bs * ls
    t = tp + ts
    depth = w_ada.shape[0]
    past = cache_k.shape[2]
    d_conv = conv_w.shape[2]
    d_ret = ret_norm_g.shape[1]
    d_q = dsa_w_out.shape[1]
    d_kv = cache_k.shape[3] * cache_k.shape[4]
    tm = _pick(math.gcd(tp, ts), (1024, 512, 256, 128))
    groups = ((0, bp, lp), (tp, bs, ls))

    x = jnp.concatenate([x_prompt.reshape(tp, d), x_sample.reshape(ts, d)], axis=0)

    c_all = jnp.concatenate([c_prompt, c_sample], axis=0)
    n_c = c_all.shape[0]
    c_all = jnp.pad(c_all, ((0, -n_c % 8), (0, 0)))
    ada = _ada(c_all, w_ada, b_ada)
    per_block = lambda a, n: jnp.broadcast_to(a[:, :, None, :], a.shape[:2] + (n, a.shape[2])).reshape(depth, -1, a.shape[2])
    ada_blk = jnp.concatenate([per_block(ada[:, :bp], lp // ADA_BLOCK),
                               per_block(ada[:, bp:bp + bs], ls // ADA_BLOCK)], axis=1)

    new_conv, new_ret, new_k, new_v, new_kidx = [], [], [], [], []
    for i in range(depth):
        sh1, sc1, g1, sh2, sc2, g2 = [ada_blk[i, :, m * d:(m + 1) * d] for m in range(6)]
        j = i // 2
        if i % 2 == 0:
            u = _modulated_matmul(x, sh1, sc1, norm_mix_g[i], cr_w_in[j].astype(BF16), tm, "cr_in_proj")
            a_out, b_out, bufs, states = [], [], [], []
            for gi, (row0, batch, seq) in enumerate(groups):
                buf0 = jnp.zeros((batch, CONV_WIDTH - 1, d_conv), F32) if gi == 0 else cache_conv[j]
                st0 = jnp.zeros((batch,) + state_ret.shape[2:], F32) if gi == 0 else state_ret[j]
                a, nbuf = _conv_branch(u, row0, batch, seq, buf0, conv_w[j], conv_b[j],
                                       conv_norm_g[j], conv_norm_b[j], d_conv)
                bo, nst = _retention_branch(u, row0, batch, seq, st0, ret_norm_g[j], 2 * d_conv // d_ret)
                a_out.append(a); b_out.append(bo); bufs.append(nbuf); states.append(nst)
            new_conv.append(bufs)
            new_ret.append(states)
            w_out = cr_w_out[j].astype(BF16)
            x = _outproj_residual([jnp.concatenate(a_out, axis=0), jnp.concatenate(b_out, axis=0)],
                                  [w_out[:d_conv], w_out[d_conv:]], x, g1, tm, "cr_out_proj")
        else:
            w_in = dsa_w_in[j].astype(BF16)
            n_main = d_q + 2 * d_kv + IDX_HEADS * IDX_DIM
            u = _modulated_matmul(x, sh1, sc1, norm_mix_g[i], w_in[:, :n_main], tm, "dsa_in_proj")
            w_x = jnp.pad(w_in[:, n_main:], ((0, 0), (0, LANES - (w_in.shape[1] - n_main))))
            ux = _modulated_matmul(x, sh1, sc1, norm_mix_g[i], w_x, tm, "dsa_in_proj_idx")
            q, k, kidx = _qk_norms(u, ux, q_norm_g[j], k_norm_g[j], kidx_norm_g[j], tm, d_q, d_kv)
            v = u[:, d_q + d_kv:d_q + 2 * d_kv]
            qidx = u[:, d_q + 2 * d_kv:n_main].astype(BF16)
            widx = ux[:, IDX_DIM:IDX_DIM + IDX_HEADS]
            outs, ks, vs, kis = [], [], [], []
            for gi, (row0, batch, seq) in enumerate(groups):
                rows = slice(row0, row0 + batch * seq)
                kg = k[rows].reshape(batch, seq, d_kv)
                vg = v[rows].reshape(batch, seq, d_kv)
                kig = kidx[rows].reshape(batch, seq, IDX_DIM)
                ks.append(kg); vs.append(vg); kis.append(kig)
                g_past = 0 if gi == 0 else past
                if g_past:
                    k_all = jnp.concatenate([cache_k[j].reshape(batch, past, d_kv), kg], axis=1)
                    v_all = jnp.concatenate([cache_v[j].reshape(batch, past, d_kv), vg], axis=1)
                    ki_all = jnp.concatenate([cache_kidx[j], kig], axis=1)
                else:
                    k_all, v_all, ki_all = kg, vg, kig
                n_keys = g_past + seq
                kb = 256 if n_keys >= 256 else 128
                n_pad = -(-n_keys // kb) * kb
                k_blk, vt_blk, ki_blk = _key_blocks(k_all, v_all, ki_all, n_pad, kb)
                outs.append(_sparse_attention(q, qidx, widx[rows], k_blk, vt_blk, ki_blk,
                                              row0, batch, seq, g_past, n_keys, kb))
            new_k.append(ks); new_v.append(vs); new_kidx.append(kis)
            x = _outproj_residual([jnp.concatenate(outs, axis=0)], [dsa_w_out[j].astype(BF16)],
                                  x, g1, tm, "dsa_out_proj")
        x = _hier_moe(x, sh2, sc2, g2, norm_ffn_g[i], moe_w_group[i], moe_b_group[i], moe_w_erouter[i],
                      moe_b_erouter[i], moe_w_gate[i].astype(BF16), moe_w_up[i].astype(BF16),
                      moe_w_down[i].astype(BF16), tm)

    kv_heads, hd = cache_k.shape[3], cache_k.shape[4]
    stack = lambda per_layer, gi, shape: jnp.stack([lay[gi].reshape(shape) for lay in per_layer])
    return (x[:tp].reshape(bp, lp, d), x[tp:].reshape(bs, ls, d),
            stack(new_conv, 0, (bp, CONV_WIDTH - 1, d_conv)), stack(new_conv, 1, (bs, CONV_WIDTH - 1, d_conv)),
            stack(new_ret, 0, (bp,) + state_ret.shape[2:]), stack(new_ret, 1, (bs,) + state_ret.shape[2:]),
            stack(new_k, 0, (bp, lp, kv_heads, hd)), stack(new_k, 1, (bs, ls, kv_heads, hd)),
            stack(new_v, 0, (bp, lp, kv_heads, hd)), stack(new_v, 1, (bs, ls, kv_heads, hd)),
            stack(new_kidx, 0, (bp, lp, IDX_DIM)), stack(new_kidx, 1, (bs, ls, IDX_DIM)))
```

```python
import functools
import math

import jax
import jax.numpy as jnp
from jax import lax
from jax.experimental import pallas as pl
from jax.experimental.pallas import tpu as pltpu

F32 = jnp.float32
BF16 = jnp.bfloat16

EPS = 1e-6
CHUNK = 64
ADA_BLOCK = 32
CONV_WIDTH = 31
RET_HEADS = 8
RET_DK = 128
ATT_HEADS = 16
ATT_KV_HEADS = 4
IDX_HEADS = 16
IDX_DIM = 64
TOPK_MAX = 256
N_GROUPS = 4
EXPERTS_PER_GROUP = 8
N_EXPERTS = N_GROUPS * EXPERTS_PER_GROUP
LANES = 128
VMEM_LIMIT = 56 * 1024 * 1024
BISECT_ITERS = 30
MASKED_DIST = 1e30
EXPERT_TILE = 256
GATHER_CHUNK = 256
ROUTER_TILE = 512
COMBINE_TILE = 256


def _params(sem, vmem=VMEM_LIMIT):
    return pltpu.CompilerParams(dimension_semantics=sem, vmem_limit_bytes=vmem)


def _pick(n, cands):
    for c in cands:
        if n % c == 0:
            return c
    raise ValueError(f"no tile in {cands} divides {n}")


def _dot(a, b):
    return jnp.dot(a, b, preferred_element_type=F32)


def _dot_nt(a, b):
    return lax.dot_general(a, b, (((1,), (1,)), ((), ())), preferred_element_type=F32)


def _dot_tn(a, b):
    return lax.dot_general(a, b, (((0,), (0,)), ((), ())), preferred_element_type=F32)


def _silu(x):
    return x * jax.nn.sigmoid(x)


def _ada_kernel(c_ref, w_ref, b_ref, o_ref):
    c = _silu(c_ref[...]).astype(BF16)
    o_ref[...] = _dot(c, w_ref[...].astype(BF16)) + b_ref[...]


def _ada(c_all, w_ada, b_ada):
    depth, d, n = w_ada.shape
    rows = c_all.shape[0]
    tn = _pick(n, (1024, 512, 256, 128))
    return pl.pallas_call(
        _ada_kernel,
        grid=(depth, n // tn),
        in_specs=[pl.BlockSpec((rows, d), lambda l, j: (0, 0)),
                  pl.BlockSpec((None, d, tn), lambda l, j: (l, 0, j)),
                  pl.BlockSpec((None, 1, tn), lambda l, j: (l, 0, j))],
        out_specs=pl.BlockSpec((None, rows, tn), lambda l, j: (l, 0, j)),
        out_shape=jax.ShapeDtypeStruct((depth, rows, n), F32),
        compiler_params=_params(("parallel", "parallel")),
        name="ada",
    )(c_all, w_ada, b_ada.reshape(depth, 1, n))


def _modulate_rows(x_ref, shift_ref, scale_ref, g_ref, store):
    nblk = x_ref.shape[0] // ADA_BLOCK

    def body(r, carry):
        rows = pl.ds(pl.multiple_of(r * ADA_BLOCK, ADA_BLOCK), ADA_BLOCK)
        x = x_ref[rows, :]
        y = x * lax.rsqrt(jnp.mean(x * x, axis=-1, keepdims=True) + EPS) * g_ref[...]
        y = y * (1.0 + scale_ref[pl.ds(r, 1), :]) + shift_ref[pl.ds(r, 1), :]
        store(rows, y)
        return carry

    lax.fori_loop(0, nblk, body, 0)


def _modmm_kernel(x_ref, shift_ref, scale_ref, g_ref, w_ref, o_ref, h_ref):
    @pl.when(pl.program_id(1) == 0)
    def _():
        def store(rows, y):
            h_ref[rows, :] = y.astype(BF16)
        _modulate_rows(x_ref, shift_ref, scale_ref, g_ref, store)

    o_ref[...] = _dot(h_ref[...], w_ref[...])


def _modulated_matmul(x, shift, scale, g, w, tm, name):
    t, d = x.shape
    n = w.shape[1]
    tn = _pick(n, (1024, 512, 256, 128))
    nb = tm // ADA_BLOCK
    return pl.pallas_call(
        _modmm_kernel,
        grid=(t // tm, n // tn),
        in_specs=[pl.BlockSpec((tm, d), lambda i, j: (i, 0)),
                  pl.BlockSpec((nb, d), lambda i, j: (i, 0)),
                  pl.BlockSpec((nb, d), lambda i, j: (i, 0)),
                  pl.BlockSpec((1, d), lambda i, j: (0, 0)),
                  pl.BlockSpec((d, tn), lambda i, j: (0, j))],
        out_specs=pl.BlockSpec((tm, tn), lambda i, j: (i, j)),
        out_shape=jax.ShapeDtypeStruct((t, n), F32),
        scratch_shapes=[pltpu.VMEM((tm, d), BF16)],
        compiler_params=_params(("parallel", "arbitrary")),
        name=name,
    )(x, shift, scale, g.reshape(1, d), w)


def _outproj_kernel(n_in, *refs):
    a_refs = refs[:n_in]
    w_refs = refs[n_in:2 * n_in]
    x_ref, gate_ref, o_ref, y_ref = refs[2 * n_in:]
    y = _dot(a_refs[0][...], w_refs[0][...])
    for a_ref, w_ref in zip(a_refs[1:], w_refs[1:]):
        y = y + _dot(a_ref[...], w_ref[...])
    y_ref[...] = y
    nblk = x_ref.shape[0] // ADA_BLOCK

    def body(r, carry):
        rows = pl.ds(pl.multiple_of(r * ADA_BLOCK, ADA_BLOCK), ADA_BLOCK)
        o_ref[rows, :] = x_ref[rows, :] + gate_ref[pl.ds(r, 1), :] * y_ref[rows, :]
        return carry

    lax.fori_loop(0, nblk, body, 0)


def _outproj_residual(acts, ws, x, gate, tm, name):
    t, d = x.shape
    tn = _pick(d, (1024, 512, 256, 128))
    nb = tm // ADA_BLOCK
    n_in = len(acts)
    in_specs = ([pl.BlockSpec((tm, a.shape[1]), lambda i, j: (i, 0)) for a in acts]
                + [pl.BlockSpec((w.shape[0], tn), lambda i, j: (0, j)) for w in ws]
                + [pl.BlockSpec((tm, tn), lambda i, j: (i, j)),
                   pl.BlockSpec((nb, tn), lambda i, j: (i, j))])
    return pl.pallas_call(
        functools.partial(_outproj_kernel, n_in),
        grid=(t // tm, d // tn),
        in_specs=in_specs,
        out_specs=pl.BlockSpec((tm, tn), lambda i, j: (i, j)),
        out_shape=jax.ShapeDtypeStruct((t, d), F32),
        scratch_shapes=[pltpu.VMEM((tm, tn), F32)],
        compiler_params=_params(("parallel", "parallel")),
        name=name,
    )(*acts, *ws, x, gate)


CONV_ROWS = 32
CONV_HIST = 32


def _conv_kernel(tl, val_ref, gate_ref, pval_ref, pgate_ref, buf_ref, w_ref, b_ref, ng_ref, nb_ref,
                 o_ref, nbuf_ref, up_ref):
    li = pl.program_id(1)
    hist = CONV_WIDTH - 1
    pad = CONV_HIST - hist
    glu = val_ref[...] * jax.nn.sigmoid(gate_ref[...])
    up_ref[CONV_HIST:CONV_HIST + tl, :] = glu

    @pl.when(li == 0)
    def _():
        up_ref[pad:CONV_HIST, :] = buf_ref[...]

    @pl.when(li > 0)
    def _():
        prev = pval_ref[...] * jax.nn.sigmoid(pgate_ref[...])
        up_ref[pad:CONV_HIST, :] = prev[pad:, :]

    for c in range(tl // CONV_ROWS):
        r0 = c * CONV_ROWS
        acc = jnp.zeros((CONV_ROWS, val_ref.shape[1]), F32)
        for j in range(CONV_WIDTH):
            acc = acc + up_ref[r0 + pad + j:r0 + pad + j + CONV_ROWS, :] * w_ref[j:j + 1, :]
        acc = acc + b_ref[...]
        mu = jnp.mean(acc, axis=-1, keepdims=True)
        dlt = acc - mu
        y = dlt * lax.rsqrt(jnp.mean(dlt * dlt, axis=-1, keepdims=True) + EPS)
        y = y * ng_ref[...] + nb_ref[...]
        o_ref[r0:r0 + CONV_ROWS, :] = _silu(y).astype(o_ref.dtype)

    @pl.when(li == pl.num_programs(1) - 1)
    def _():
        nbuf_ref[...] = up_ref[CONV_HIST + tl - hist:CONV_HIST + tl, :]


def _conv_branch(u, row0, batch, seq, conv_buf, conv_w, conv_b, cn_g, cn_b, d_conv):
    tl = _pick(seq, (128, 64, 32))
    nl = seq // tl
    hist = CONV_WIDTH - 1
    rb = row0 // tl
    pb = tl // CONV_HIST
    cur = lambda col: pl.BlockSpec((tl, d_conv), lambda b, l: (rb + b * nl + l, col))
    prev = lambda col: pl.BlockSpec(
        (CONV_HIST, d_conv), lambda b, l: (jnp.maximum((rb + b * nl + l) * pb - 1, 0), col))
    vec = pl.BlockSpec((1, d_conv), lambda b, l: (0, 0))
    return pl.pallas_call(
        functools.partial(_conv_kernel, tl),
        grid=(batch, nl),
        in_specs=[cur(0), cur(1), prev(0), prev(1),
                  pl.BlockSpec((None, hist, d_conv), lambda b, l: (b, 0, 0)),
                  pl.BlockSpec((CONV_WIDTH, d_conv), lambda b, l: (0, 0)),
                  vec, vec, vec],
        out_specs=[pl.BlockSpec((tl, d_conv), lambda b, l: (b * nl + l, 0)),
                   pl.BlockSpec((None, hist, d_conv), lambda b, l: (b, 0, 0))],
        out_shape=[jax.ShapeDtypeStruct((batch * seq, d_conv), BF16),
                   jax.ShapeDtypeStruct((batch, hist, d_conv), F32)],
        scratch_shapes=[pltpu.VMEM((CONV_HIST + tl, d_conv), F32)],
        compiler_params=_params(("parallel", "arbitrary")),
        name="conv_branch",
    )(u, u, u, u, conv_buf, conv_w, conv_b.reshape(1, -1), cn_g.reshape(1, -1), cn_b.reshape(1, -1))


def _retention_kernel(q_ref, k_ref, v_ref, g_ref, s0_ref, din_ref, dq_ref, dk_ref, dblk_ref, rg_ref,
                      o_ref, s_out_ref, s_ref):
    ci = pl.program_id(1)

    @pl.when(ci == 0)
    def _():
        s_ref[...] = s0_ref[...]

    dv = s_ref.shape[2]
    for h in range(RET_HEADS):
        cols = slice(h * dv, (h + 1) * dv)
        q = q_ref[:, cols].astype(BF16)
        k = k_ref[:, cols] * (RET_DK ** -0.5)
        v = v_ref[:, cols].astype(BF16)
        s_prev = s_ref[h]
        sc = _dot_nt(q, k.astype(BF16)) * din_ref[h]
        o = _dot(sc.astype(BF16), v) + _dot(q, s_prev.astype(BF16)) * dq_ref[h]
        kd = (k * dk_ref[h]).astype(BF16)
        s_ref[h] = s_prev * dblk_ref[h] + _dot_tn(kd, v)
        o = o * lax.rsqrt(jnp.mean(o * o, axis=-1, keepdims=True) + EPS) * rg_ref[:, cols]
        o_ref[:, cols] = (o * _silu(g_ref[:, cols])).astype(o_ref.dtype)

    @pl.when(ci == pl.num_programs(1) - 1)
    def _():
        s_out_ref[...] = s_ref[...]


def _retention_branch(u, row0, batch, seq, state0, ret_g, col_q):
    heads, dk, dv = state0.shape[1:]
    d_ret = heads * dv
    c = _pick(seq, (256, 128, 64, 32))
    nc = seq // c
    rb = row0 // c
    lg = jnp.log1p(-(2.0 ** (-5.0 - jnp.arange(heads, dtype=F32))))
    pos = jnp.arange(c, dtype=F32)
    diff = pos[:, None] - pos[None, :]
    d_in = jnp.where(diff >= 0, jnp.exp(lg[:, None, None] * jnp.maximum(diff, 0.0)), 0.0)
    d_q = jnp.broadcast_to(jnp.exp(lg[:, None] * (pos[None, :] + 1.0))[:, :, None], (heads, c, dv))
    d_k = jnp.broadcast_to(jnp.exp(lg[:, None] * (c - 1.0 - pos[None, :]))[:, :, None], (heads, c, dk))
    d_blk = jnp.broadcast_to(jnp.exp(lg * c)[:, None, None], (heads, dk, dv))
    blk = lambda col: pl.BlockSpec((c, d_ret), lambda b, i: (rb + b * nc + i, col))
    const3 = lambda shape: pl.BlockSpec(shape, lambda b, i: (0, 0, 0))
    return pl.pallas_call(
        _retention_kernel,
        grid=(batch, nc),
        in_specs=[blk(col_q), blk(col_q + 1), blk(col_q + 2), blk(col_q + 3),
                  pl.BlockSpec((None, heads, dk, dv), lambda b, i: (b, 0, 0, 0)),
                  const3((heads, c, c)), const3((heads, c, dv)), const3((heads, c, dk)),
                  const3((heads, dk, dv)),
                  pl.BlockSpec((1, d_ret), lambda b, i: (0, 0))],
        out_specs=[pl.BlockSpec((c, d_ret), lambda b, i: (b * nc + i, 0)),
                   pl.BlockSpec((None, heads, dk, dv), lambda b, i: (b, 0, 0, 0))],
        out_shape=[jax.ShapeDtypeStruct((batch * seq, d_ret), BF16),
                   jax.ShapeDtypeStruct(state0.shape, F32)],
        scratch_shapes=[pltpu.VMEM((heads, dk, dv), F32)],
        compiler_params=_params(("parallel", "arbitrary")),
        name="retention_branch",
    )(u, u, u, u, state0, d_in, d_q, d_k, d_blk, ret_g.reshape(1, d_ret))


def _head_rms(x, g, hd):
    outs = []
    for h in range(x.shape[1] // hd):
        xh = x[:, h * hd:(h + 1) * hd]
        outs.append(xh * lax.rsqrt(jnp.mean(xh * xh, axis=-1, keepdims=True) + EPS) * g)
    return outs


def _qknorm_kernel(q_ref, k_ref, qi_ref, ki_ref, qg_ref, kg_ref, kig_ref, qo_ref, ko_ref, qio_ref, kio_ref):
    hd = qg_ref.shape[1]
    qio_ref[...] = qi_ref[...].astype(qio_ref.dtype)
    for h, qh in enumerate(_head_rms(q_ref[...], qg_ref[...], hd)):
        qo_ref[:, h * hd:(h + 1) * hd] = qh.astype(qo_ref.dtype)
    for h, kh in enumerate(_head_rms(k_ref[...], kg_ref[...], hd)):
        ko_ref[:, h * hd:(h + 1) * hd] = kh
    ki = ki_ref[:, :IDX_DIM]
    kio_ref[...] = ki * lax.rsqrt(jnp.mean(ki * ki, axis=-1, keepdims=True) + EPS) * kig_ref[...]


def _qk_norms(u, ux, q_g, k_g, kidx_g, tm, d_q, d_kv):
    t = u.shape[0]
    hd = q_g.shape[0]
    d_qi = IDX_HEADS * IDX_DIM
    return pl.pallas_call(
        _qknorm_kernel,
        grid=(t // tm,),
        in_specs=[pl.BlockSpec((tm, d_q), lambda i: (i, 0)),
                  pl.BlockSpec((tm, d_kv), lambda i: (i, d_q // d_kv)),
                  pl.BlockSpec((tm, d_qi), lambda i: (i, (d_q + 2 * d_kv) // d_qi)),
                  pl.BlockSpec((tm, LANES), lambda i: (i, 0)),
                  pl.BlockSpec((1, hd), lambda i: (0, 0)),
                  pl.BlockSpec((1, hd), lambda i: (0, 0)),
                  pl.BlockSpec((1, IDX_DIM), lambda i: (0, 0))],
        out_specs=[pl.BlockSpec((tm, d_q), lambda i: (i, 0)),
                   pl.BlockSpec((tm, d_kv), lambda i: (i, 0)),
                   pl.BlockSpec((tm, d_qi), lambda i: (i, 0)),
                   pl.BlockSpec((tm, IDX_DIM), lambda i: (i, 0))],
        out_shape=[jax.ShapeDtypeStruct((t, d_q), BF16),
                   jax.ShapeDtypeStruct((t, d_kv), F32),
                   jax.ShapeDtypeStruct((t, d_qi), BF16),
                   jax.ShapeDtypeStruct((t, IDX_DIM), F32)],
        compiler_params=_params(("parallel",)),
        name="qk_norms",
    )(u, u, u, ux, q_g.reshape(1, hd), k_g.reshape(1, hd), kidx_g.reshape(1, IDX_DIM))


def _dsa_kernel(tq, rep, kb, past, n_keys, n_sel, q_ref, qi_ref, wt_ref, slope_ref, tri_ref,
                k_ref, vt_ref, ki_ref, o_ref, sc_ref, dist_ref, qg_ref, acc_ref):
    t0 = pl.program_id(1) * tq
    hd = k_ref.shape[2] // ATT_KV_HEADS
    grp = ATT_HEADS // ATT_KV_HEADS
    wq = rep * tq
    n_adm_tile = jnp.minimum(((past + t0 + tq - 1) // CHUNK + 1) * CHUNK, n_keys)
    nkb = (n_adm_tile + kb - 1) // kb
    q_pos = past + t0 + lax.broadcasted_iota(jnp.int32, (1, wq), 1) % tq
    q_chunk = q_pos // CHUNK
    n_adm = jnp.minimum((q_chunk + 1) * CHUNK, n_keys).astype(F32)
    need = jnp.minimum(n_adm, float(n_sel))
    neg_inf = jnp.float32(-jnp.inf)
    w_t = wt_ref[...] * (IDX_HEADS ** -0.5 * IDX_DIM ** -0.5)

    def key_pos(i):
        return i * kb + lax.broadcasted_iota(jnp.int32, (kb, 1), 0)

    def admissible(i):
        kp = key_pos(i)
        return jnp.logical_and(kp // CHUNK <= q_chunk, kp < n_keys)

    def score_body(i, carry):
        lo, hi = carry
        ki = ki_ref[i]
        acc = jnp.zeros((kb, wq), F32)
        for h in range(IDX_HEADS):
            qi = jnp.concatenate([qi_ref[:, h * IDX_DIM:(h + 1) * IDX_DIM]] * rep, axis=0)
            rel = _dot_nt(ki, qi)
            acc = acc + jnp.maximum(rel, 0.0) * w_t[h:h + 1, :]
        adm = admissible(i)
        sc_ref[i] = jnp.where(adm, acc, neg_inf)
        lo = jnp.minimum(lo, jnp.min(jnp.where(adm, acc, jnp.inf), axis=0, keepdims=True))
        hi = jnp.maximum(hi, jnp.max(jnp.where(adm, acc, neg_inf), axis=0, keepdims=True))
        return lo, hi

    lo, hi = lax.fori_loop(0, nkb, score_body,
                           (jnp.full((1, wq), jnp.inf, F32), jnp.full((1, wq), neg_inf, F32)))

    def count(pred):
        def body(i, acc):
            ones = jnp.where(pred(sc_ref[i]), 1.0, 0.0)
            parts = [ones[r:r + 8, :] for r in range(0, kb, 8)]
            while len(parts) > 1:
                parts = [parts[a] + parts[a + 1] for a in range(0, len(parts), 2)]
            return acc + parts[0]
        return jnp.sum(lax.fori_loop(0, nkb, body, jnp.zeros((8, wq), F32)), axis=0, keepdims=True)

    def bisect(_, carry):
        lo, hi, c_lo = carry
        mid = 0.5 * (lo + hi)
        c_mid = count(lambda s: s >= mid)
        ge = c_mid >= need
        return jnp.where(ge, mid, lo), jnp.where(ge, hi, mid), jnp.where(ge, c_mid, c_lo)

    lo, hi, c_lo = lax.fori_loop(0, BISECT_ITERS, bisect, (lo, hi, n_adm))

    def write_dist(i, sel):
        dist = jnp.abs(q_pos - key_pos(i)).astype(F32)
        dist_ref[i] = jnp.where(sel, dist, MASKED_DIST)

    resolved = jnp.max(c_lo - need) <= 0.0

    @pl.when(resolved)
    def _():
        def body(i, carry):
            write_dist(i, sc_ref[i] >= lo)
            return carry
        lax.fori_loop(0, nkb, body, 0)

    @pl.when(jnp.logical_not(resolved))
    def _():
        n_above = count(lambda s: s > hi)
        room = need - n_above

        def body(i, seen):
            s = sc_ref[i]
            above = s > hi
            band = jnp.logical_and(s >= lo, jnp.logical_not(above))
            band_f = jnp.where(band, 1.0, 0.0)
            rank = _dot(tri_ref[...], band_f.astype(BF16)) + seen
            write_dist(i, jnp.logical_or(above, jnp.logical_and(band, rank <= room)))
            return seen + jnp.sum(band_f, axis=0, keepdims=True)
        lax.fori_loop(0, nkb, body, jnp.zeros((1, wq), F32))

    scale = hd ** -0.5
    for g in range(ATT_KV_HEADS):
        qg_ref[g] = jnp.concatenate(
            [q_ref[:, (g * grp + r) * hd:(g * grp + r + 1) * hd] for r in range(grp)], axis=0)
    acc_ref[...] = jnp.zeros_like(acc_ref)

    def att_body(i, carry):
        ms, ls = carry
        dist = jnp.concatenate([dist_ref[i]] * (grp // rep), axis=1)
        new_ms, new_ls = [], []
        for g in range(ATT_KV_HEADS):
            logits = _dot_nt(k_ref[i, :, g * hd:(g + 1) * hd], qg_ref[g]) * scale
            z = logits - slope_ref[g] * dist
            m_new = jnp.maximum(ms[g], jnp.max(z, axis=0, keepdims=True))
            alpha = jnp.exp(ms[g] - m_new)
            p = jnp.exp(z - m_new)
            new_ls.append(ls[g] * alpha + jnp.sum(p, axis=0, keepdims=True))
            new_ms.append(m_new)
            acc_ref[g] = acc_ref[g] * alpha + _dot(vt_ref[i, g * hd:(g + 1) * hd, :], p.astype(BF16))
        return tuple(new_ms), tuple(new_ls)

    _, ls = lax.fori_loop(
        0, nkb, att_body,
        (tuple(jnp.full((1, grp * tq), neg_inf, F32) for _ in range(ATT_KV_HEADS)),
         tuple(jnp.zeros((1, grp * tq), F32) for _ in range(ATT_KV_HEADS))))
    for g in range(ATT_KV_HEADS):
        out = (acc_ref[g] / ls[g]).T
        for r in range(grp):
            o_ref[:, (g * grp + r) * hd:(g * grp + r + 1) * hd] = out[r * tq:(r + 1) * tq, :].astype(o_ref.dtype)


def _sparse_attention(q, qidx, widx, k_blk, vt_blk, ki_blk, row0, batch, seq, past, n_keys, kb):
    d_q = q.shape[1]
    nkb_all = k_blk.shape[1]
    d_kv = k_blk.shape[3]
    tq = _pick(seq, (128, 64, 32))
    nq = seq // tq
    rb = row0 // tq
    grp = ATT_HEADS // ATT_KV_HEADS
    rep = max(1, LANES // tq)
    wq = rep * tq
    n_sel = min(TOPK_MAX, n_keys // 4)
    widx_t = jnp.tile(jnp.swapaxes(widx.reshape(batch * nq, tq, IDX_HEADS), 1, 2), (1, 1, rep))
    slopes = 2.0 ** (-8.0 * jnp.arange(1, ATT_HEADS + 1, dtype=F32) / ATT_HEADS)
    slope_rows = jnp.repeat(slopes.reshape(ATT_KV_HEADS, grp), tq, axis=1).reshape(ATT_KV_HEADS, 1, grp * tq)
    tri = (jnp.arange(kb)[:, None] >= jnp.arange(kb)[None, :]).astype(BF16)
    return pl.pallas_call(
        functools.partial(_dsa_kernel, tq, rep, kb, past, n_keys, n_sel),
        grid=(batch, nq),
        in_specs=[pl.BlockSpec((tq, d_q), lambda b, i: (rb + b * nq + i, 0)),
                  pl.BlockSpec((tq, qidx.shape[1]), lambda b, i: (rb + b * nq + i, 0)),
                  pl.BlockSpec((None, IDX_HEADS, wq), lambda b, i: (b * nq + i, 0, 0)),
                  pl.BlockSpec((ATT_KV_HEADS, 1, grp * tq), lambda b, i: (0, 0, 0)),
                  pl.BlockSpec((kb, kb), lambda b, i: (0, 0)),
                  pl.BlockSpec((None, nkb_all, kb, d_kv), lambda b, i: (b, 0, 0, 0)),
                  pl.BlockSpec((None, nkb_all, d_kv, kb), lambda b, i: (b, 0, 0, 0)),
                  pl.BlockSpec((None, nkb_all, kb, IDX_DIM), lambda b, i: (b, 0, 0, 0))],
        out_specs=pl.BlockSpec((tq, d_q), lambda b, i: (b * nq + i, 0)),
        out_shape=jax.ShapeDtypeStruct((batch * seq, d_q), BF16),
        scratch_shapes=[pltpu.VMEM((nkb_all, kb, wq), F32), pltpu.VMEM((nkb_all, kb, wq), F32),
                        pltpu.VMEM((ATT_KV_HEADS, grp * tq, d_kv // ATT_KV_HEADS), BF16),
                        pltpu.VMEM((ATT_KV_HEADS, d_kv // ATT_KV_HEADS, grp * tq), F32)],
        compiler_params=_params(("parallel", "arbitrary")),
        name="sparse_attention",
    )(q, qidx, widx_t, slope_rows, tri, k_blk, vt_blk, ki_blk)


META_E, META_W, META_R = 0, 2, 4


def _router_kernel(x_ref, shift_ref, scale_ref, g_ref, wr_ref, br_ref, tri_ref,
                   h_ref, meta_ref, cnt_ref, carry_ref):
    @pl.when(pl.program_id(0) == 0)
    def _():
        carry_ref[...] = jnp.zeros_like(carry_ref)

    def store(rows, y):
        h_ref[rows, :] = y
    _modulate_rows(x_ref, shift_ref, scale_ref, g_ref, store)

    logits = jnp.dot(h_ref[...], wr_ref[...], preferred_element_type=F32,
                     precision=lax.Precision.HIGHEST) + br_ref[...]
    tm = logits.shape[0]
    lane = lax.broadcasted_iota(jnp.int32, (tm, LANES), 1).astype(F32)
    neg_inf = jnp.float32(-jnp.inf)

    def first_argmax(v):
        top = jnp.max(v, axis=-1, keepdims=True)
        return top, jnp.min(jnp.where(v == top, lane, float(LANES)), axis=-1, keepdims=True)

    is_group = lane < N_GROUPS
    gl = jnp.where(is_group, logits, neg_inf)
    g_top, g_sel = first_argmax(gl)
    g_w = 1.0 / jnp.sum(jnp.where(is_group, jnp.exp(gl - g_top), 0.0), axis=-1, keepdims=True)
    first = N_GROUPS + g_sel * EXPERTS_PER_GROUP
    el = jnp.where(jnp.logical_and(lane >= first, lane < first + EXPERTS_PER_GROUP), logits, neg_inf)
    v1, i1 = first_argmax(el)
    v2, i2 = first_argmax(jnp.where(lane == i1, neg_inf, el))
    e21 = jnp.exp(v2 - v1)
    w1 = g_w / (1.0 + e21)
    w2 = g_w * e21 / (1.0 + e21)
    e1 = i1 - N_GROUPS
    e2 = i2 - N_GROUPS

    oh1 = jnp.where(lane == e1, 1.0, 0.0)
    oh2 = jnp.where(lane == e2, 1.0, 0.0)
    both = oh1 + oh2
    before = _dot(tri_ref[...], both.astype(BF16)) + carry_ref[...]
    r1 = jnp.sum(before * oh1, axis=-1, keepdims=True)
    r2 = jnp.sum(before * oh2, axis=-1, keepdims=True)
    carry_ref[...] += jnp.sum(both, axis=0, keepdims=True)

    meta = jnp.zeros((tm, LANES), F32)
    for ln, val in ((META_E, e1), (META_E + 1, e2), (META_W, w1), (META_W + 1, w2),
                    (META_R, r1), (META_R + 1, r2)):
        meta = jnp.where(lane == ln, val, meta)
    meta_ref[...] = meta
    cnt_ref[...] = carry_ref[...]


def _router(x, shift, scale, g, w_router, b_router, tm):
    t, d = x.shape
    nb = tm // ADA_BLOCK
    tri = (jnp.arange(tm)[:, None] > jnp.arange(tm)[None, :]).astype(BF16)
    return pl.pallas_call(
        _router_kernel,
        grid=(t // tm,),
        in_specs=[pl.BlockSpec((tm, d), lambda i: (i, 0)),
                  pl.BlockSpec((nb, d), lambda i: (i, 0)),
                  pl.BlockSpec((nb, d), lambda i: (i, 0)),
                  pl.BlockSpec((1, d), lambda i: (0, 0)),
                  pl.BlockSpec((d, LANES), lambda i: (0, 0)),
                  pl.BlockSpec((1, LANES), lambda i: (0, 0)),
                  pl.BlockSpec((tm, tm), lambda i: (0, 0))],
        out_specs=[pl.BlockSpec((tm, d), lambda i: (i, 0)),
                   pl.BlockSpec((tm, LANES), lambda i: (i, 0)),
                   pl.BlockSpec((1, LANES), lambda i: (0, 0))],
        out_shape=[jax.ShapeDtypeStruct((t, d), F32),
                   jax.ShapeDtypeStruct((t, LANES), F32),
                   jax.ShapeDtypeStruct((1, LANES), F32)],
        scratch_shapes=[pltpu.VMEM((1, LANES), F32)],
        compiler_params=_params(("arbitrary",)),
        name="moe_router",
    )(x, shift, scale, g.reshape(1, d), w_router, b_router, tri)


def _row_copy(src_ref, dst_ref, sem, src_row, dst_row):
    return pltpu.make_async_copy(src_ref.at[pl.ds(src_row, 1)], dst_ref.at[pl.ds(dst_row, 1)], sem)


def _dispatch_kernel(pos_ref, h_ref, xs_ref, sem):
    n = h_ref.shape[0]

    def start(r, carry):
        for k in range(2):
            _row_copy(h_ref, xs_ref, sem, r, pos_ref[0, 0, 2 * r + k]).start()
        return carry
    lax.fori_loop(0, n, start, 0)

    def wait(r, carry):
        _row_copy(h_ref, xs_ref, sem, 0, 0).wait()
        return carry
    lax.fori_loop(0, 2 * n, wait, 0)


def _dispatch(h, pos, ch):
    t, d = h.shape
    return pl.pallas_call(
        _dispatch_kernel,
        grid=(t // ch,),
        in_specs=[pl.BlockSpec((1, 1, 2 * ch), lambda i: (i, 0, 0), memory_space=pltpu.SMEM),
                  pl.BlockSpec((ch, d), lambda i: (i, 0))],
        out_specs=pl.BlockSpec(memory_space=pl.ANY),
        out_shape=jax.ShapeDtypeStruct((2 * t, d), F32),
        scratch_shapes=[pltpu.SemaphoreType.DMA(())],
        compiler_params=_params(("arbitrary",)),
        name="moe_dispatch",
    )(pos.reshape(t // ch, 1, 2 * ch), h)


def _cast_rows(src_ref, dst_ref, rows):
    def body(c, carry):
        r = pl.ds(pl.multiple_of(c * rows, rows), rows)
        dst_ref[r, :] = src_ref[r, :].astype(dst_ref.dtype)
        return carry
    lax.fori_loop(0, src_ref.shape[0] // rows, body, 0)


def _expert_kernel(vt_ref, ve_ref, vlo_ref, vhi_ref, x_ref, wg_ref, wu_ref, wd_ref, o_ref,
                   wgb_ref, wub_ref, wdb_ref):
    v = pl.program_id(0)
    lo, hi = vlo_ref[v], vhi_ref[v]
    prev = jnp.maximum(v - 1, 0)

    @pl.when(jnp.logical_and(hi > lo, jnp.logical_or(v == 0, ve_ref[prev] != ve_ref[v])))
    def _():
        _cast_rows(wg_ref, wgb_ref, 64)
        _cast_rows(wu_ref, wub_ref, 64)
        _cast_rows(wd_ref, wdb_ref, 16)

    @pl.when(hi > lo)
    def _():
        x = x_ref[...].astype(BF16)
        a = _dot(x, wgb_ref[...])
        b = _dot(x, wub_ref[...])
        y = _dot((_silu(a) * b).astype(BF16), wdb_ref[...])
        row = lax.broadcasted_iota(jnp.int32, (x.shape[0], 1), 0)
        mine = jnp.logical_and(row >= lo, row < hi)
        first = jnp.logical_or(v == 0, vt_ref[prev] != vt_ref[v])

        @pl.when(first)
        def _():
            o_ref[...] = jnp.where(mine, y, 0.0)

        @pl.when(jnp.logical_not(first))
        def _():
            o_ref[...] = jnp.where(mine, y, o_ref[...])


def _expert_mlp(xs, visits, layer, wg, wu, wd):
    p, d = xs.shape
    de = wg.shape[3]
    tm = EXPERT_TILE
    n_visits = visits[0].shape[0]
    return pl.pallas_call(
        _expert_kernel,
        grid_spec=pltpu.PrefetchScalarGridSpec(
            num_scalar_prefetch=4,
            grid=(n_visits,),
            in_specs=[pl.BlockSpec((tm, d), lambda v, vt, ve, lo, hi: (vt[v], 0)),
                      pl.BlockSpec((None, None, d, de), lambda v, vt, ve, lo, hi: (layer, ve[v], 0, 0)),
                      pl.BlockSpec((None, None, d, de), lambda v, vt, ve, lo, hi: (layer, ve[v], 0, 0)),
                      pl.BlockSpec((None, None, de, d), lambda v, vt, ve, lo, hi: (layer, ve[v], 0, 0))],
            out_specs=pl.BlockSpec((tm, d), lambda v, vt, ve, lo, hi: (vt[v], 0)),
            scratch_shapes=[pltpu.VMEM((d, de), BF16), pltpu.VMEM((d, de), BF16), pltpu.VMEM((de, d), BF16)]),
        out_shape=jax.ShapeDtypeStruct((p, d), F32),
        compiler_params=_params(("arbitrary",)),
        name="moe_experts",
    )(*visits, xs, wg, wu, wd)


def _combine_kernel(pos_ref, npos_ref, x_ref, gate_ref, meta_ref, ys_ref, o_ref, ybuf, sems):
    i = pl.program_id(0)
    n = pl.num_programs(0)
    tm = x_ref.shape[0]

    def fetch(p_ref, slot):
        def body(r, carry):
            for k in range(2):
                _row_copy(ys_ref, ybuf.at[slot, k], sems.at[slot], p_ref[0, 0, 2 * r + k], r).start()
            return carry
        lax.fori_loop(0, tm, body, 0)

    @pl.when(i == 0)
    def _():
        fetch(pos_ref, 0)

    @pl.when(i + 1 < n)
    def _():
        fetch(npos_ref, (i + 1) % 2)

    slot = i % 2

    def wait(r, carry):
        _row_copy(ys_ref, ybuf.at[slot, 0], sems.at[slot], 0, 0).wait()
        return carry
    lax.fori_loop(0, 2 * tm, wait, 0)

    def body(r, carry):
        rows = pl.ds(pl.multiple_of(r * ADA_BLOCK, ADA_BLOCK), ADA_BLOCK)
        meta = meta_ref[rows, :]
        y = meta[:, META_W:META_W + 1] * ybuf[slot, 0, rows, :] + meta[:, META_W + 1:META_W + 2] * ybuf[slot, 1, rows, :]
        o_ref[rows, :] = x_ref[rows, :] + gate_ref[pl.ds(r, 1), :] * y
        return carry
    lax.fori_loop(0, tm // ADA_BLOCK, body, 0)


def _combine(x, gate, meta, ys, pos, tm):
    t, d = x.shape
    nb = tm // ADA_BLOCK
    n = t // tm
    pos3 = pos.reshape(n, 1, 2 * tm)
    return pl.pallas_call(
        _combine_kernel,
        grid=(n,),
        in_specs=[pl.BlockSpec((1, 1, 2 * tm), lambda i: (i, 0, 0), memory_space=pltpu.SMEM),
                  pl.BlockSpec((1, 1, 2 * tm), lambda i: (jnp.minimum(i + 1, n - 1), 0, 0), memory_space=pltpu.SMEM),
                  pl.BlockSpec((tm, d), lambda i: (i, 0)),
                  pl.BlockSpec((nb, d), lambda i: (i, 0)),
                  pl.BlockSpec((tm, LANES), lambda i: (i, 0)),
                  pl.BlockSpec(memory_space=pl.ANY)],
        out_specs=pl.BlockSpec((tm, d), lambda i: (i, 0)),
        out_shape=jax.ShapeDtypeStruct((t, d), F32),
        scratch_shapes=[pltpu.VMEM((2, 2, tm, d), F32), pltpu.SemaphoreType.DMA((2,))],
        compiler_params=_params(("arbitrary",)),
        name="moe_combine",
    )(pos3, pos3, x, gate, meta, ys)


def _expert_visits(cnt, n_tiles):
    tile = EXPERT_TILE
    ends = jnp.cumsum(cnt)
    starts = ends - cnt
    first_tile = starts // tile
    n_vis = jnp.where(cnt > 0, (ends - 1) // tile - first_tile + 1, 0)
    vis_end = jnp.cumsum(n_vis)
    v = jnp.arange(n_tiles + N_EXPERTS - 1, dtype=jnp.int32)
    e = jnp.minimum(jnp.sum(vis_end[None, :] <= v[:, None], axis=1), N_EXPERTS - 1).astype(jnp.int32)
    real = v < vis_end[-1]
    tile_id = jnp.where(real, first_tile[e] + v - (vis_end - n_vis)[e], n_tiles - 1).astype(jnp.int32)
    lo = jnp.where(real, jnp.maximum(starts[e] - tile_id * tile, 0), 0).astype(jnp.int32)
    hi = jnp.where(real, jnp.minimum(ends[e] - tile_id * tile, tile), 0).astype(jnp.int32)
    return tile_id, e, lo, hi


def _hier_moe(x, shift, scale, gate, g, w_group, b_group, w_er, b_er, layer, wg, wu, wd, tm):
    t, d = x.shape
    w_router = jnp.concatenate([w_group, jnp.moveaxis(w_er, 0, 1).reshape(d, N_EXPERTS)], axis=1)
    b_router = jnp.concatenate([b_group, b_er.reshape(N_EXPERTS)])
    n_route = N_GROUPS + N_EXPERTS
    w_router = jnp.pad(w_router, ((0, 0), (0, LANES - n_route)))
    b_router = jnp.pad(b_router, (0, LANES - n_route)).reshape(1, LANES)
    h, meta, counts = _router(x, shift, scale, g, w_router, b_router, min(tm, ROUTER_TILE))

    expert = meta[:, META_E:META_E + 2].astype(jnp.int32)
    rank = meta[:, META_R:META_R + 2].astype(jnp.int32)
    cnt = counts[0, :N_EXPERTS].astype(jnp.int32)
    starts = jnp.cumsum(cnt) - cnt
    pos = jnp.sum(jnp.where(expert[:, :, None] == jnp.arange(N_EXPERTS), starts, 0), axis=-1) + rank
    visits = _expert_visits(cnt, 2 * t // EXPERT_TILE)

    xs = _dispatch(h, pos, min(tm, GATHER_CHUNK))
    ys = _expert_mlp(xs, visits, layer, wg, wu, wd)
    return _combine(x, gate, meta, ys, pos, min(tm, COMBINE_TILE))


def _key_blocks(k, vt_src, ki, n_pad, kb):
    b, s, _ = k.shape
    padk = lambda a: jnp.pad(a.astype(BF16), ((0, 0), (0, n_pad - s), (0, 0)))
    nkb = n_pad // kb
    k_blk = padk(k).reshape(b, nkb, kb, k.shape[2])
    vt_blk = jnp.swapaxes(padk(vt_src).reshape(b, nkb, kb, vt_src.shape[2]), 2, 3)
    ki_blk = padk(ki).reshape(b, nkb, kb, ki.shape[2])
    return k_blk, vt_blk, ki_blk


def kernel(x_prompt, x_sample, c_prompt, c_sample, cache_conv, state_ret, cache_k, cache_v, cache_kidx, norm_mix_g, norm_ffn_g, w_ada, b_ada, cr_w_in, conv_w, conv_b, conv_norm_g, conv_norm_b, ret_norm_g, cr_w_out, dsa_w_in, q_norm_g, k_norm_g, kidx_norm_g, dsa_w_out, moe_w_group, moe_b_group, moe_w_erouter, moe_b_erouter, moe_w_gate, moe_w_up, moe_w_down):
    bp, lp, d = x_prompt.shape
    bs, ls, _ = x_sample.shape
    tp, ts = bp * lp, bs * ls
    t = tp + ts
    depth = w_ada.shape[0]
    past = cache_k.shape[2]
    d_conv = conv_w.shape[2]
    d_ret = ret_norm_g.shape[1]
    d_q = dsa_w_out.shape[1]
    d_kv = cache_k.shape[3] * cache_k.shape[4]
    tm = _pick(math.gcd(tp, ts), (1024, 512, 256, 128))
    groups = ((0, bp, lp), (tp, bs, ls))

    x = jnp.concatenate([x_prompt.reshape(tp, d), x_sample.reshape(ts, d)], axis=0)

    c_all = jnp.concatenate([c_prompt, c_sample], axis=0)
    n_c = c_all.shape[0]
    c_all = jnp.pad(c_all, ((0, -n_c % 8), (0, 0)))
    ada = _ada(c_all, w_ada, b_ada)
    per_block = lambda a, n: jnp.broadcast_to(a[:, :, None, :], a.shape[:2] + (n, a.shape[2])).reshape(depth, -1, a.shape[2])
    ada_blk = jnp.concatenate([per_block(ada[:, :bp], lp // ADA_BLOCK),
                               per_block(ada[:, bp:bp + bs], ls // ADA_BLOCK)], axis=1)

    new_conv, new_ret, new_k, new_v, new_kidx = [], [], [], [], []
    for i in range(depth):
        sh1, sc1, g1, sh2, sc2, g2 = [ada_blk[i, :, m * d:(m + 1) * d] for m in range(6)]
        j = i // 2
        if i % 2 == 0:
            u = _modulated_matmul(x, sh1, sc1, norm_mix_g[i], cr_w_in[j].astype(BF16), tm, "cr_in_proj")
            a_out, b_out, bufs, states = [], [], [], []
            for gi, (row0, batch, seq) in enumerate(groups):
                buf0 = jnp.zeros((batch, CONV_WIDTH - 1, d_conv), F32) if gi == 0 else cache_conv[j]
                st0 = jnp.zeros((batch,) + state_ret.shape[2:], F32) if gi == 0 else state_ret[j]
                a, nbuf = _conv_branch(u, row0, batch, seq, buf0, conv_w[j], conv_b[j],
                                       conv_norm_g[j], conv_norm_b[j], d_conv)
                bo, nst = _retention_branch(u, row0, batch, seq, st0, ret_norm_g[j], 2 * d_conv // d_ret)
                a_out.append(a); b_out.append(bo); bufs.append(nbuf); states.append(nst)
            new_conv.append(bufs)
            new_ret.append(states)
            w_out = cr_w_out[j].astype(BF16)
            x = _outproj_residual([jnp.concatenate(a_out, axis=0), jnp.concatenate(b_out, axis=0)],
                                  [w_out[:d_conv], w_out[d_conv:]], x, g1, tm, "cr_out_proj")
        else:
            w_in = dsa_w_in[j].astype(BF16)
            n_main = d_q + 2 * d_kv + IDX_HEADS * IDX_DIM
            u = _modulated_matmul(x, sh1, sc1, norm_mix_g[i], w_in[:, :n_main], tm, "dsa_in_proj")
            w_x = jnp.pad(w_in[:, n_main:], ((0, 0), (0, LANES - (w_in.shape[1] - n_main))))
            ux = _modulated_matmul(x, sh1, sc1, norm_mix_g[i], w_x, tm, "dsa_in_proj_idx")
            q, k, qidx, kidx = _qk_norms(u, ux, q_norm_g[j], k_norm_g[j], kidx_norm_g[j], tm, d_q, d_kv)
            v = u[:, d_q + d_kv:d_q + 2 * d_kv]
            widx = ux[:, IDX_DIM:IDX_DIM + IDX_HEADS]
            outs, ks, vs, kis = [], [], [], []
            for gi, (row0, batch, seq) in enumerate(groups):
                rows = slice(row0, row0 + batch * seq)
                kg = k[rows].reshape(batch, seq, d_kv)
                vg = v[rows].reshape(batch, seq, d_kv)
                kig = kidx[rows].reshape(batch, seq, IDX_DIM)
                ks.append(kg); vs.append(vg); kis.append(kig)
                g_past = 0 if gi == 0 else past
                if g_past:
                    k_all = jnp.concatenate([cache_k[j].reshape(batch, past, d_kv), kg], axis=1)
                    v_all = jnp.concatenate([cache_v[j].reshape(batch, past, d_kv), vg], axis=1)
                    ki_all = jnp.concatenate([cache_kidx[j], kig], axis=1)
                else:
                    k_all, v_all, ki_all = kg, vg, kig
                n_keys = g_past + seq
                kb = 256 if n_keys >= 256 else 128
                n_pad = -(-n_keys // kb) * kb
                k_blk, vt_blk, ki_blk = _key_blocks(k_all, v_all, ki_all, n_pad, kb)
                outs.append(_sparse_attention(q, qidx, widx[rows], k_blk, vt_blk, ki_blk,
                                              row0, batch, seq, g_past, n_keys, kb))
            new_k.append(ks); new_v.append(vs); new_kidx.append(kis)
            x = _outproj_residual([jnp.concatenate(outs, axis=0)], [dsa_w_out[j].astype(BF16)],
                                  x, g1, tm, "dsa_out_proj")
        x = _hier_moe(x, sh2, sc2, g2, norm_ffn_g[i], moe_w_group[i], moe_b_group[i], moe_w_erouter[i],
                      moe_b_erouter[i], i, moe_w_gate, moe_w_up, moe_w_down, tm)

    kv_heads, hd = cache_k.shape[3], cache_k.shape[4]
    stack = lambda per_layer, gi, shape: jnp.stack([lay[gi].reshape(shape) for lay in per_layer])
    return (x[:tp].reshape(bp, lp, d), x[tp:].reshape(bs, ls, d),
            stack(new_conv, 0, (bp, CONV_WIDTH - 1, d_conv)), stack(new_conv, 1, (bs, CONV_WIDTH - 1, d_conv)),
            stack(new_ret, 0, (bp,) + state_ret.shape[2:]), stack(new_ret, 1, (bs,) + state_ret.shape[2:]),
            stack(new_k, 0, (bp, lp, kv_heads, hd)), stack(new_k, 1, (bs, ls, kv_heads, hd)),
            stack(new_v, 0, (bp, lp, kv_heads, hd)), stack(new_v, 1, (bs, ls, kv_heads, hd)),
            stack(new_kidx, 0, (bp, lp, IDX_DIM)), stack(new_kidx, 1, (bs, ls, IDX_DIM)))
```

```python
import functools
import math

import jax
import jax.numpy as jnp
from jax import lax
from jax.experimental import pallas as pl
from jax.experimental.pallas import tpu as pltpu

F32 = jnp.float32
BF16 = jnp.bfloat16

EPS = 1e-6
CHUNK = 64
ADA_BLOCK = 32
CONV_WIDTH = 31
RET_HEADS = 8
RET_DK = 128
ATT_HEADS = 16
ATT_KV_HEADS = 4
IDX_HEADS = 16
IDX_DIM = 64
TOPK_MAX = 256
N_GROUPS = 4
EXPERTS_PER_GROUP = 8
N_EXPERTS = N_GROUPS * EXPERTS_PER_GROUP
LANES = 128
VMEM_LIMIT = 56 * 1024 * 1024
BISECT_ITERS = 30
MASKED_DIST = 1e30
LOG2E = math.log2(math.e)
EXPERT_TILE = 256
GATHER_CHUNK = 256
ROUTER_TILE = 512
COMBINE_TILE = 256


def _params(sem, vmem=VMEM_LIMIT):
    return pltpu.CompilerParams(dimension_semantics=sem, vmem_limit_bytes=vmem)


def _pick(n, cands):
    for c in cands:
        if n % c == 0:
            return c
    raise ValueError(f"no tile in {cands} divides {n}")


def _dot(a, b):
    return jnp.dot(a, b, preferred_element_type=F32)


def _dot_nt(a, b):
    return lax.dot_general(a, b, (((1,), (1,)), ((), ())), preferred_element_type=F32)


def _dot_tn(a, b):
    return lax.dot_general(a, b, (((0,), (0,)), ((), ())), preferred_element_type=F32)


def _silu(x):
    return x * jax.nn.sigmoid(x)


def _ada_kernel(c_ref, w_ref, b_ref, o_ref):
    c = _silu(c_ref[...]).astype(BF16)
    o_ref[...] = _dot(c, w_ref[...].astype(BF16)) + b_ref[...]


def _ada(c_all, w_ada, b_ada):
    depth, d, n = w_ada.shape
    rows = c_all.shape[0]
    tn = _pick(n, (1024, 512, 256, 128))
    return pl.pallas_call(
        _ada_kernel,
        grid=(depth, n // tn),
        in_specs=[pl.BlockSpec((rows, d), lambda l, j: (0, 0)),
                  pl.BlockSpec((None, d, tn), lambda l, j: (l, 0, j)),
                  pl.BlockSpec((None, 1, tn), lambda l, j: (l, 0, j))],
        out_specs=pl.BlockSpec((None, rows, tn), lambda l, j: (l, 0, j)),
        out_shape=jax.ShapeDtypeStruct((depth, rows, n), F32),
        compiler_params=_params(("parallel", "parallel")),
        name="ada",
    )(c_all, w_ada, b_ada.reshape(depth, 1, n))


def _modulate_rows(x_ref, shift_ref, scale_ref, g_ref, store):
    nblk = x_ref.shape[0] // ADA_BLOCK

    def body(r, carry):
        rows = pl.ds(pl.multiple_of(r * ADA_BLOCK, ADA_BLOCK), ADA_BLOCK)
        x = x_ref[rows, :]
        y = x * lax.rsqrt(jnp.mean(x * x, axis=-1, keepdims=True) + EPS) * g_ref[...]
        y = y * (1.0 + scale_ref[pl.ds(r, 1), :]) + shift_ref[pl.ds(r, 1), :]
        store(rows, y)
        return carry

    lax.fori_loop(0, nblk, body, 0)


def _modmm_kernel(has_extra, x_ref, shift_ref, scale_ref, g_ref, w_ref, *refs):
    if has_extra:
        wx_ref, o_ref, ox_ref, h_ref = refs
    else:
        o_ref, h_ref = refs

    @pl.when(pl.program_id(1) == 0)
    def _():
        def store(rows, y):
            h_ref[rows, :] = y.astype(BF16)
        _modulate_rows(x_ref, shift_ref, scale_ref, g_ref, store)
        if has_extra:
            ox_ref[...] = _dot(h_ref[...], wx_ref[...])

    o_ref[...] = _dot(h_ref[...], w_ref[...])


def _modulated_matmul(x, shift, scale, g, w, tm, name, w_extra=None):
    t, d = x.shape
    n = w.shape[1]
    tn = _pick(n, (1024, 512, 256, 128))
    nb = tm // ADA_BLOCK
    in_specs = [pl.BlockSpec((tm, d), lambda i, j: (i, 0)),
                pl.BlockSpec((nb, d), lambda i, j: (i, 0)),
                pl.BlockSpec((nb, d), lambda i, j: (i, 0)),
                pl.BlockSpec((1, d), lambda i, j: (0, 0)),
                pl.BlockSpec((d, tn), lambda i, j: (0, j))]
    out_specs = pl.BlockSpec((tm, tn), lambda i, j: (i, j))
    out_shape = jax.ShapeDtypeStruct((t, n), F32)
    args = [x, shift, scale, g.reshape(1, d), w]
    if w_extra is not None:
        nx = w_extra.shape[1]
        in_specs.append(pl.BlockSpec((d, nx), lambda i, j: (0, 0)))
        out_specs = [out_specs, pl.BlockSpec((tm, nx), lambda i, j: (i, 0))]
        out_shape = [out_shape, jax.ShapeDtypeStruct((t, nx), F32)]
        args.append(w_extra)
    return pl.pallas_call(
        functools.partial(_modmm_kernel, w_extra is not None),
        grid=(t // tm, n // tn),
        in_specs=in_specs,
        out_specs=out_specs,
        out_shape=out_shape,
        scratch_shapes=[pltpu.VMEM((tm, d), BF16)],
        compiler_params=_params(("parallel", "arbitrary")),
        name=name,
    )(*args)


def _outproj_kernel(n_in, *refs):
    a_refs = refs[:n_in]
    w_refs = refs[n_in:2 * n_in]
    x_ref, gate_ref, o_ref, y_ref = refs[2 * n_in:]
    y = _dot(a_refs[0][...], w_refs[0][...])
    for a_ref, w_ref in zip(a_refs[1:], w_refs[1:]):
        y = y + _dot(a_ref[...], w_ref[...])
    y_ref[...] = y
    nblk = x_ref.shape[0] // ADA_BLOCK

    def body(r, carry):
        rows = pl.ds(pl.multiple_of(r * ADA_BLOCK, ADA_BLOCK), ADA_BLOCK)
        o_ref[rows, :] = x_ref[rows, :] + gate_ref[pl.ds(r, 1), :] * y_ref[rows, :]
        return carry

    lax.fori_loop(0, nblk, body, 0)


def _outproj_residual(acts, ws, x, gate, tm, name):
    t, d = x.shape
    tn = _pick(d, (1024, 512, 256, 128))
    nb = tm // ADA_BLOCK
    n_in = len(acts)
    in_specs = ([pl.BlockSpec((tm, a.shape[1]), lambda i, j: (i, 0)) for a in acts]
                + [pl.BlockSpec((w.shape[0], tn), lambda i, j: (0, j)) for w in ws]
                + [pl.BlockSpec((tm, tn), lambda i, j: (i, j)),
                   pl.BlockSpec((nb, tn), lambda i, j: (i, j))])
    return pl.pallas_call(
        functools.partial(_outproj_kernel, n_in),
        grid=(t // tm, d // tn),
        in_specs=in_specs,
        out_specs=pl.BlockSpec((tm, tn), lambda i, j: (i, j)),
        out_shape=jax.ShapeDtypeStruct((t, d), F32),
        scratch_shapes=[pltpu.VMEM((tm, tn), F32)],
        compiler_params=_params(("parallel", "parallel")),
        name=name,
    )(*acts, *ws, x, gate)


CONV_ROWS = 32
CONV_HIST = 32


def _conv_kernel(tl, val_ref, gate_ref, pval_ref, pgate_ref, buf_ref, w_ref, b_ref, ng_ref, nb_ref,
                 o_ref, nbuf_ref, up_ref, sh_ref):
    li = pl.program_id(1)
    hist = CONV_WIDTH - 1
    pad = CONV_HIST - hist
    glu = val_ref[...] * jax.nn.sigmoid(gate_ref[...])
    up_ref[CONV_HIST:CONV_HIST + tl, :] = glu
    up_ref[0:pad, :] = jnp.zeros((pad, up_ref.shape[1]), F32)

    @pl.when(li == 0)
    def _():
        up_ref[pad:CONV_HIST, :] = buf_ref[...]

    @pl.when(li > 0)
    def _():
        prev = pval_ref[...] * jax.nn.sigmoid(pgate_ref[...])
        up_ref[pad:CONV_HIST, :] = prev[pad:, :]

    n_sh = tl + CONV_HIST - 8
    for s in range(1, 8):
        sh_ref[s - 1, 0:n_sh, :] = up_ref[s:s + n_sh, :]

    def tap_rows(row):
        base, s = row - row % 8, row % 8
        src = up_ref if s == 0 else sh_ref.at[s - 1]
        return src[base:base + CONV_ROWS, :]

    for c in range(tl // CONV_ROWS):
        r0 = c * CONV_ROWS
        acc = jnp.zeros((CONV_ROWS, val_ref.shape[1]), F32)
        for j in range(CONV_WIDTH):
            acc = acc + tap_rows(r0 + pad + j) * w_ref[j:j + 1, :]
        acc = acc + b_ref[...]
        mu = jnp.mean(acc, axis=-1, keepdims=True)
        dlt = acc - mu
        y = dlt * lax.rsqrt(jnp.mean(dlt * dlt, axis=-1, keepdims=True) + EPS)
        y = y * ng_ref[...] + nb_ref[...]
        o_ref[r0:r0 + CONV_ROWS, :] = _silu(y).astype(o_ref.dtype)

    @pl.when(li == pl.num_programs(1) - 1)
    def _():
        nbuf_ref[...] = up_ref[CONV_HIST + tl - hist:CONV_HIST + tl, :]


def _conv_branch(u, row0, batch, seq, conv_buf, conv_w, conv_b, cn_g, cn_b, d_conv):
    tl = _pick(seq, (128, 64, 32))
    nl = seq // tl
    hist = CONV_WIDTH - 1
    rb = row0 // tl
    pb = tl // CONV_HIST
    cur = lambda col: pl.BlockSpec((tl, d_conv), lambda b, l: (rb + b * nl + l, col))
    prev = lambda col: pl.BlockSpec(
        (CONV_HIST, d_conv), lambda b, l: (jnp.maximum((rb + b * nl + l) * pb - 1, 0), col))
    vec = pl.BlockSpec((1, d_conv), lambda b, l: (0, 0))
    return pl.pallas_call(
        functools.partial(_conv_kernel, tl),
        grid=(batch, nl),
        in_specs=[cur(0), cur(1), prev(0), prev(1),
                  pl.BlockSpec((None, hist, d_conv), lambda b, l: (b, 0, 0)),
                  pl.BlockSpec((CONV_WIDTH, d_conv), lambda b, l: (0, 0)),
                  vec, vec, vec],
        out_specs=[pl.BlockSpec((tl, d_conv), lambda b, l: (b * nl + l, 0)),
                   pl.BlockSpec((None, hist, d_conv), lambda b, l: (b, 0, 0))],
        out_shape=[jax.ShapeDtypeStruct((batch * seq, d_conv), BF16),
                   jax.ShapeDtypeStruct((batch, hist, d_conv), F32)],
        scratch_shapes=[pltpu.VMEM((CONV_HIST + tl, d_conv), F32),
                        pltpu.VMEM((7, CONV_HIST + tl, d_conv), F32)],
        compiler_params=_params(("parallel", "arbitrary")),
        name="conv_branch",
    )(u, u, u, u, conv_buf, conv_w, conv_b.reshape(1, -1), cn_g.reshape(1, -1), cn_b.reshape(1, -1))


def _retention_kernel(q_ref, k_ref, v_ref, g_ref, s0_ref, din_ref, dq_ref, dk_ref, dblk_ref, rg_ref,
                      o_ref, s_out_ref, s_ref):
    ci = pl.program_id(1)

    @pl.when(ci == 0)
    def _():
        s_ref[...] = s0_ref[...]

    dv = s_ref.shape[2]
    for h in range(RET_HEADS):
        cols = slice(h * dv, (h + 1) * dv)
        q = q_ref[:, cols].astype(BF16)
        k = k_ref[:, cols] * (RET_DK ** -0.5)
        v = v_ref[:, cols].astype(BF16)
        s_prev = s_ref[h]
        sc = _dot_nt(q, k.astype(BF16)) * din_ref[h]
        o = _dot(sc.astype(BF16), v) + _dot(q, s_prev.astype(BF16)) * dq_ref[h]
        kd = (k * dk_ref[h]).astype(BF16)
        s_ref[h] = s_prev * dblk_ref[h] + _dot_tn(kd, v)
        o = o * lax.rsqrt(jnp.mean(o * o, axis=-1, keepdims=True) + EPS) * rg_ref[:, cols]
        o_ref[:, cols] = (o * _silu(g_ref[:, cols])).astype(o_ref.dtype)

    @pl.when(ci == pl.num_programs(1) - 1)
    def _():
        s_out_ref[...] = s_ref[...]


def _retention_branch(u, row0, batch, seq, state0, ret_g, col_q):
    heads, dk, dv = state0.shape[1:]
    d_ret = heads * dv
    c = _pick(seq, (256, 128, 64, 32))
    nc = seq // c
    rb = row0 // c
    lg = jnp.log1p(-(2.0 ** (-5.0 - jnp.arange(heads, dtype=F32))))
    pos = jnp.arange(c, dtype=F32)
    diff = pos[:, None] - pos[None, :]
    d_in = jnp.where(diff >= 0, jnp.exp(lg[:, None, None] * jnp.maximum(diff, 0.0)), 0.0)
    d_q = jnp.broadcast_to(jnp.exp(lg[:, None] * (pos[None, :] + 1.0))[:, :, None], (heads, c, dv))
    d_k = jnp.broadcast_to(jnp.exp(lg[:, None] * (c - 1.0 - pos[None, :]))[:, :, None], (heads, c, dk))
    d_blk = jnp.broadcast_to(jnp.exp(lg * c)[:, None, None], (heads, dk, dv))
    blk = lambda col: pl.BlockSpec((c, d_ret), lambda b, i: (rb + b * nc + i, col))
    const3 = lambda shape: pl.BlockSpec(shape, lambda b, i: (0, 0, 0))
    return pl.pallas_call(
        _retention_kernel,
        grid=(batch, nc),
        in_specs=[blk(col_q), blk(col_q + 1), blk(col_q + 2), blk(col_q + 3),
                  pl.BlockSpec((None, heads, dk, dv), lambda b, i: (b, 0, 0, 0)),
                  const3((heads, c, c)), const3((heads, c, dv)), const3((heads, c, dk)),
                  const3((heads, dk, dv)),
                  pl.BlockSpec((1, d_ret), lambda b, i: (0, 0))],
        out_specs=[pl.BlockSpec((c, d_ret), lambda b, i: (b * nc + i, 0)),
                   pl.BlockSpec((None, heads, dk, dv), lambda b, i: (b, 0, 0, 0))],
        out_shape=[jax.ShapeDtypeStruct((batch * seq, d_ret), BF16),
                   jax.ShapeDtypeStruct(state0.shape, F32)],
        scratch_shapes=[pltpu.VMEM((heads, dk, dv), F32)],
        compiler_params=_params(("parallel", "arbitrary")),
        name="retention_branch",
    )(u, u, u, u, state0, d_in, d_q, d_k, d_blk, ret_g.reshape(1, d_ret))


def _head_rms(x, g, hd):
    outs = []
    for h in range(x.shape[1] // hd):
        xh = x[:, h * hd:(h + 1) * hd]
        outs.append(xh * lax.rsqrt(jnp.mean(xh * xh, axis=-1, keepdims=True) + EPS) * g)
    return outs


def _qknorm_kernel(q_ref, k_ref, qi_ref, ki_ref, qg_ref, kg_ref, kig_ref, qo_ref, ko_ref, qio_ref, kio_ref):
    hd = qg_ref.shape[1]
    qio_ref[...] = qi_ref[...].astype(qio_ref.dtype)
    q_scale = hd ** -0.5 * LOG2E
    for h, qh in enumerate(_head_rms(q_ref[...], qg_ref[...], hd)):
        qo_ref[:, h * hd:(h + 1) * hd] = (qh * q_scale).astype(qo_ref.dtype)
    for h, kh in enumerate(_head_rms(k_ref[...], kg_ref[...], hd)):
        ko_ref[:, h * hd:(h + 1) * hd] = kh
    ki = ki_ref[:, :IDX_DIM]
    kio_ref[...] = ki * lax.rsqrt(jnp.mean(ki * ki, axis=-1, keepdims=True) + EPS) * kig_ref[...]


def _qk_norms(u, ux, q_g, k_g, kidx_g, tm, d_q, d_kv):
    t = u.shape[0]
    hd = q_g.shape[0]
    d_qi = IDX_HEADS * IDX_DIM
    return pl.pallas_call(
        _qknorm_kernel,
        grid=(t // tm,),
        in_specs=[pl.BlockSpec((tm, d_q), lambda i: (i, 0)),
                  pl.BlockSpec((tm, d_kv), lambda i: (i, d_q // d_kv)),
                  pl.BlockSpec((tm, d_qi), lambda i: (i, (d_q + 2 * d_kv) // d_qi)),
                  pl.BlockSpec((tm, LANES), lambda i: (i, 0)),
                  pl.BlockSpec((1, hd), lambda i: (0, 0)),
                  pl.BlockSpec((1, hd), lambda i: (0, 0)),
                  pl.BlockSpec((1, IDX_DIM), lambda i: (0, 0))],
        out_specs=[pl.BlockSpec((tm, d_q), lambda i: (i, 0)),
                   pl.BlockSpec((tm, d_kv), lambda i: (i, 0)),
                   pl.BlockSpec((tm, d_qi), lambda i: (i, 0)),
                   pl.BlockSpec((tm, IDX_DIM), lambda i: (i, 0))],
        out_shape=[jax.ShapeDtypeStruct((t, d_q), BF16),
                   jax.ShapeDtypeStruct((t, d_kv), F32),
                   jax.ShapeDtypeStruct((t, d_qi), BF16),
                   jax.ShapeDtypeStruct((t, IDX_DIM), F32)],
        compiler_params=_params(("parallel",)),
        name="qk_norms",
    )(u, u, u, ux, q_g.reshape(1, hd), k_g.reshape(1, hd), kidx_g.reshape(1, IDX_DIM))


def _dsa_kernel(tq, rep, kb, past, n_keys, n_sel, q_ref, qi_ref, wt_ref, slope_ref, tri_ref,
                k_ref, vt_ref, ki_ref, o_ref, sc_ref, dist_ref, qg_ref, acc_ref, z_ref, rel_ref):
    t0 = pl.program_id(1) * tq
    hd = k_ref.shape[2] // ATT_KV_HEADS
    grp = ATT_HEADS // ATT_KV_HEADS
    wq = rep * tq
    n_adm_tile = jnp.minimum(((past + t0 + tq - 1) // CHUNK + 1) * CHUNK, n_keys)
    nkb = (n_adm_tile + kb - 1) // kb
    q_pos = past + t0 + lax.broadcasted_iota(jnp.int32, (1, wq), 1) % tq
    q_chunk = q_pos // CHUNK
    n_adm = jnp.minimum((q_chunk + 1) * CHUNK, n_keys).astype(F32)
    need = jnp.minimum(n_adm, float(n_sel))
    neg_inf = jnp.float32(-jnp.inf)
    w_t = wt_ref[...] * (IDX_HEADS ** -0.5 * IDX_DIM ** -0.5)

    def key_pos(i):
        return i * kb + lax.broadcasted_iota(jnp.int32, (kb, 1), 0)

    def admissible(i):
        kp = key_pos(i)
        return jnp.logical_and(kp // CHUNK <= q_chunk, kp < n_keys)

    def score_body(i, carry):
        lo, hi = carry
        ki = ki_ref[i]
        for h in range(IDX_HEADS):
            qi = jnp.concatenate([qi_ref[:, h * IDX_DIM:(h + 1) * IDX_DIM]] * rep, axis=0)
            rel_ref[h] = _dot_nt(ki, qi)
        acc = jnp.zeros((kb, wq), F32)
        for h in range(IDX_HEADS):
            acc = acc + jnp.maximum(rel_ref[h], 0.0) * w_t[h:h + 1, :]
        adm = admissible(i)
        sc_ref[i] = jnp.where(adm, acc, neg_inf)
        lo = jnp.minimum(lo, jnp.min(jnp.where(adm, acc, jnp.inf), axis=0, keepdims=True))
        hi = jnp.maximum(hi, jnp.max(jnp.where(adm, acc, neg_inf), axis=0, keepdims=True))
        return lo, hi

    lo, hi = lax.fori_loop(0, nkb, score_body,
                           (jnp.full((1, wq), jnp.inf, F32), jnp.full((1, wq), neg_inf, F32)))

    def count(pred):
        def body(i, acc):
            ones = jnp.where(pred(sc_ref[i]), 1.0, 0.0)
            parts = [ones[r:r + 8, :] for r in range(0, kb, 8)]
            while len(parts) > 1:
                parts = [parts[a] + parts[a + 1] for a in range(0, len(parts), 2)]
            return acc + parts[0]
        return jnp.sum(lax.fori_loop(0, nkb, body, jnp.zeros((8, wq), F32)), axis=0, keepdims=True)

    def bisect(_, carry):
        lo, hi, c_lo = carry
        mid = 0.5 * (lo + hi)
        c_mid = count(lambda s: s >= mid)
        ge = c_mid >= need
        return jnp.where(ge, mid, lo), jnp.where(ge, hi, mid), jnp.where(ge, c_mid, c_lo)

    lo, hi, c_lo = lax.fori_loop(0, BISECT_ITERS, bisect, (lo, hi, n_adm))

    def write_dist(i, sel):
        dist = jnp.abs(q_pos - key_pos(i)).astype(F32)
        dist_ref[i] = jnp.where(sel, dist, MASKED_DIST)

    resolved = jnp.max(c_lo - need) <= 0.0

    @pl.when(resolved)
    def _():
        def body(i, carry):
            write_dist(i, sc_ref[i] >= lo)
            return carry
        lax.fori_loop(0, nkb, body, 0)

    @pl.when(jnp.logical_not(resolved))
    def _():
        n_above = count(lambda s: s > hi)
        room = need - n_above

        def body(i, seen):
            s = sc_ref[i]
            above = s > hi
            band = jnp.logical_and(s >= lo, jnp.logical_not(above))
            band_f = jnp.where(band, 1.0, 0.0)
            rank = _dot(tri_ref[...], band_f.astype(BF16)) + seen
            write_dist(i, jnp.logical_or(above, jnp.logical_and(band, rank <= room)))
            return seen + jnp.sum(band_f, axis=0, keepdims=True)
        lax.fori_loop(0, nkb, body, jnp.zeros((1, wq), F32))

    for g in range(ATT_KV_HEADS):
        qg_ref[g] = jnp.concatenate(
            [q_ref[:, (g * grp + r) * hd:(g * grp + r + 1) * hd] for r in range(grp)], axis=0)
    acc_ref[...] = jnp.zeros_like(acc_ref)

    def att_body(i, carry):
        ms, ls = carry
        dist = jnp.concatenate([dist_ref[i]] * (grp // rep), axis=1)
        new_ms, new_ls = [], []
        for g in range(ATT_KV_HEADS):
            z_ref[g] = _dot_nt(k_ref[i, :, g * hd:(g + 1) * hd], qg_ref[g])
        for g in range(ATT_KV_HEADS):
            z = z_ref[g] - slope_ref[g] * dist
            m_new = jnp.maximum(ms[g], jnp.max(z, axis=0, keepdims=True))
            alpha = jnp.exp2(ms[g] - m_new)
            p = jnp.exp2(z - m_new)
            new_ls.append(ls[g] * alpha + jnp.sum(p, axis=0, keepdims=True))
            new_ms.append(m_new)
            acc_ref[g] = acc_ref[g] * alpha + _dot(vt_ref[i, g * hd:(g + 1) * hd, :], p.astype(BF16))
        return tuple(new_ms), tuple(new_ls)

    _, ls = lax.fori_loop(
        0, nkb, att_body,
        (tuple(jnp.full((1, grp * tq), neg_inf, F32) for _ in range(ATT_KV_HEADS)),
         tuple(jnp.zeros((1, grp * tq), F32) for _ in range(ATT_KV_HEADS))))
    for g in range(ATT_KV_HEADS):
        out = (acc_ref[g] / ls[g]).T
        for r in range(grp):
            o_ref[:, (g * grp + r) * hd:(g * grp + r + 1) * hd] = out[r * tq:(r + 1) * tq, :].astype(o_ref.dtype)


def _sparse_attention(q, qidx, widx, k_blk, vt_blk, ki_blk, row0, batch, seq, past, n_keys, kb):
    d_q = q.shape[1]
    nkb_all = k_blk.shape[1]
    d_kv = k_blk.shape[3]
    tq = _pick(seq, (128, 64, 32))
    nq = seq // tq
    rb = row0 // tq
    grp = ATT_HEADS // ATT_KV_HEADS
    rep = max(1, LANES // tq)
    wq = rep * tq
    n_sel = min(TOPK_MAX, n_keys // 4)
    widx_t = jnp.tile(jnp.swapaxes(widx.reshape(batch * nq, tq, IDX_HEADS), 1, 2), (1, 1, rep))
    slopes = LOG2E * 2.0 ** (-8.0 * jnp.arange(1, ATT_HEADS + 1, dtype=F32) / ATT_HEADS)
    slope_rows = jnp.repeat(slopes.reshape(ATT_KV_HEADS, grp), tq, axis=1).reshape(ATT_KV_HEADS, 1, grp * tq)
    tri = (jnp.arange(kb)[:, None] >= jnp.arange(kb)[None, :]).astype(BF16)
    return pl.pallas_call(
        functools.partial(_dsa_kernel, tq, rep, kb, past, n_keys, n_sel),
        grid=(batch, nq),
        in_specs=[pl.BlockSpec((tq, d_q), lambda b, i: (rb + b * nq + i, 0)),
                  pl.BlockSpec((tq, qidx.shape[1]), lambda b, i: (rb + b * nq + i, 0)),
                  pl.BlockSpec((None, IDX_HEADS, wq), lambda b, i: (b * nq + i, 0, 0)),
                  pl.BlockSpec((ATT_KV_HEADS, 1, grp * tq), lambda b, i: (0, 0, 0)),
                  pl.BlockSpec((kb, kb), lambda b, i: (0, 0)),
                  pl.BlockSpec((None, nkb_all, kb, d_kv), lambda b, i: (b, 0, 0, 0)),
                  pl.BlockSpec((None, nkb_all, d_kv, kb), lambda b, i: (b, 0, 0, 0)),
                  pl.BlockSpec((None, nkb_all, kb, IDX_DIM), lambda b, i: (b, 0, 0, 0))],
        out_specs=pl.BlockSpec((tq, d_q), lambda b, i: (b * nq + i, 0)),
        out_shape=jax.ShapeDtypeStruct((batch * seq, d_q), BF16),
        scratch_shapes=[pltpu.VMEM((nkb_all, kb, wq), F32), pltpu.VMEM((nkb_all, kb, wq), F32),
                        pltpu.VMEM((ATT_KV_HEADS, grp * tq, d_kv // ATT_KV_HEADS), BF16),
                        pltpu.VMEM((ATT_KV_HEADS, d_kv // ATT_KV_HEADS, grp * tq), F32),
                        pltpu.VMEM((ATT_KV_HEADS, kb, grp * tq), F32),
                        pltpu.VMEM((IDX_HEADS, kb, wq), F32)],
        compiler_params=_params(("parallel", "arbitrary")),
        name="sparse_attention",
    )(q, qidx, widx_t, slope_rows, tri, k_blk, vt_blk, ki_blk)


META_E, META_W, META_R = 0, 2, 4


def _router_kernel(x_ref, shift_ref, scale_ref, g_ref, wr_ref, br_ref, tri_ref,
                   h_ref, meta_ref, cnt_ref, carry_ref):
    @pl.when(pl.program_id(0) == 0)
    def _():
        carry_ref[...] = jnp.zeros_like(carry_ref)

    def store(rows, y):
        h_ref[rows, :] = y
    _modulate_rows(x_ref, shift_ref, scale_ref, g_ref, store)

    logits = jnp.dot(h_ref[...], wr_ref[...], preferred_element_type=F32,
                     precision=lax.Precision.HIGHEST) + br_ref[...]
    tm = logits.shape[0]
    lane = lax.broadcasted_iota(jnp.int32, (tm, LANES), 1).astype(F32)
    neg_inf = jnp.float32(-jnp.inf)

    def first_argmax(v):
        top = jnp.max(v, axis=-1, keepdims=True)
        return top, jnp.min(jnp.where(v == top, lane, float(LANES)), axis=-1, keepdims=True)

    is_group = lane < N_GROUPS
    gl = jnp.where(is_group, logits, neg_inf)
    g_top, g_sel = first_argmax(gl)
    g_w = 1.0 / jnp.sum(jnp.where(is_group, jnp.exp(gl - g_top), 0.0), axis=-1, keepdims=True)
    first = N_GROUPS + g_sel * EXPERTS_PER_GROUP
    el = jnp.where(jnp.logical_and(lane >= first, lane < first + EXPERTS_PER_GROUP), logits, neg_inf)
    v1, i1 = first_argmax(el)
    v2, i2 = first_argmax(jnp.where(lane == i1, neg_inf, el))
    e21 = jnp.exp(v2 - v1)
    w1 = g_w / (1.0 + e21)
    w2 = g_w * e21 / (1.0 + e21)
    e1 = i1 - N_GROUPS
    e2 = i2 - N_GROUPS

    oh1 = jnp.where(lane == e1, 1.0, 0.0)
    oh2 = jnp.where(lane == e2, 1.0, 0.0)
    both = oh1 + oh2
    before = _dot(tri_ref[...], both.astype(BF16)) + carry_ref[...]
    r1 = jnp.sum(before * oh1, axis=-1, keepdims=True)
    r2 = jnp.sum(before * oh2, axis=-1, keepdims=True)
    carry_ref[...] += jnp.sum(both, axis=0, keepdims=True)

    meta = jnp.zeros((tm, LANES), F32)
    for ln, val in ((META_E, e1), (META_E + 1, e2), (META_W, w1), (META_W + 1, w2),
                    (META_R, r1), (META_R + 1, r2)):
        meta = jnp.where(lane == ln, val, meta)
    meta_ref[...] = meta
    cnt_ref[...] = carry_ref[...]


def _router(x, shift, scale, g, w_router, b_router, tm):
    t, d = x.shape
    nb = tm // ADA_BLOCK
    tri = (jnp.arange(tm)[:, None] > jnp.arange(tm)[None, :]).astype(BF16)
    return pl.pallas_call(
        _router_kernel,
        grid=(t // tm,),
        in_specs=[pl.BlockSpec((tm, d), lambda i: (i, 0)),
                  pl.BlockSpec((nb, d), lambda i: (i, 0)),
                  pl.BlockSpec((nb, d), lambda i: (i, 0)),
                  pl.BlockSpec((1, d), lambda i: (0, 0)),
                  pl.BlockSpec((d, LANES), lambda i: (0, 0)),
                  pl.BlockSpec((1, LANES), lambda i: (0, 0)),
                  pl.BlockSpec((tm, tm), lambda i: (0, 0))],
        out_specs=[pl.BlockSpec((tm, d), lambda i: (i, 0)),
                   pl.BlockSpec((tm, LANES), lambda i: (i, 0)),
                   pl.BlockSpec((1, LANES), lambda i: (0, 0))],
        out_shape=[jax.ShapeDtypeStruct((t, d), F32),
                   jax.ShapeDtypeStruct((t, LANES), F32),
                   jax.ShapeDtypeStruct((1, LANES), F32)],
        scratch_shapes=[pltpu.VMEM((1, LANES), F32)],
        compiler_params=_params(("arbitrary",)),
        name="moe_router",
    )(x, shift, scale, g.reshape(1, d), w_router, b_router, tri)


def _row_copy(src_ref, dst_ref, sem, src_row, dst_row):
    return pltpu.make_async_copy(src_ref.at[pl.ds(src_row, 1)], dst_ref.at[pl.ds(dst_row, 1)], sem)


def _wait_rows(hbm_ref, sem, n):
    rows = hbm_ref.at[pl.ds(0, n)]
    pltpu.make_async_copy(rows, rows, sem).wait()


def _dispatch_kernel(pos_ref, h_ref, xs_ref, sem):
    n = h_ref.shape[0]

    def start(r, carry):
        for k in range(2):
            _row_copy(h_ref, xs_ref, sem, r, pos_ref[0, 0, 2 * r + k]).start(priority=k)
        return carry
    lax.fori_loop(0, n, start, 0)
    _wait_rows(xs_ref, sem, 2 * n)


def _dispatch(h, pos, ch):
    t, d = h.shape
    return pl.pallas_call(
        _dispatch_kernel,
        grid=(t // ch,),
        in_specs=[pl.BlockSpec((1, 1, 2 * ch), lambda i: (i, 0, 0), memory_space=pltpu.SMEM),
                  pl.BlockSpec((ch, d), lambda i: (i, 0))],
        out_specs=pl.BlockSpec(memory_space=pl.ANY),
        out_shape=jax.ShapeDtypeStruct((2 * t, d), F32),
        scratch_shapes=[pltpu.SemaphoreType.DMA(())],
        compiler_params=_params(("arbitrary",)),
        name="moe_dispatch",
    )(pos.reshape(t // ch, 1, 2 * ch), h)


def _cast_rows(src_ref, dst_ref, rows):
    def body(c, carry):
        r = pl.ds(pl.multiple_of(c * rows, rows), rows)
        dst_ref[r, :] = src_ref[r, :].astype(dst_ref.dtype)
        return carry
    lax.fori_loop(0, src_ref.shape[0] // rows, body, 0)


def _expert_kernel(vt_ref, ve_ref, vlo_ref, vhi_ref, x_ref, wg_ref, wu_ref, wd_ref, o_ref,
                   wgb_ref, wub_ref, wdb_ref):
    v = pl.program_id(0)
    lo, hi = vlo_ref[v], vhi_ref[v]
    prev = jnp.maximum(v - 1, 0)

    @pl.when(jnp.logical_and(hi > lo, jnp.logical_or(v == 0, ve_ref[prev] != ve_ref[v])))
    def _():
        _cast_rows(wg_ref, wgb_ref, 64)
        _cast_rows(wu_ref, wub_ref, 64)
        _cast_rows(wd_ref, wdb_ref, 16)

    @pl.when(hi > lo)
    def _():
        x = x_ref[...].astype(BF16)
        a = _dot(x, wgb_ref[...])
        b = _dot(x, wub_ref[...])
        y = _dot((_silu(a) * b).astype(BF16), wdb_ref[...])
        row = lax.broadcasted_iota(jnp.int32, (x.shape[0], 1), 0)
        mine = jnp.logical_and(row >= lo, row < hi)
        first = jnp.logical_or(v == 0, vt_ref[prev] != vt_ref[v])

        @pl.when(first)
        def _():
            o_ref[...] = jnp.where(mine, y, 0.0)

        @pl.when(jnp.logical_not(first))
        def _():
            o_ref[...] = jnp.where(mine, y, o_ref[...])


def _expert_mlp(xs, visits, layer, wg, wu, wd):
    p, d = xs.shape
    de = wg.shape[3]
    tm = EXPERT_TILE
    n_visits = visits[0].shape[0]
    return pl.pallas_call(
        _expert_kernel,
        grid_spec=pltpu.PrefetchScalarGridSpec(
            num_scalar_prefetch=4,
            grid=(n_visits,),
            in_specs=[pl.BlockSpec((tm, d), lambda v, vt, ve, lo, hi: (vt[v], 0)),
                      pl.BlockSpec((None, None, d, de), lambda v, vt, ve, lo, hi: (layer, ve[v], 0, 0)),
                      pl.BlockSpec((None, None, d, de), lambda v, vt, ve, lo, hi: (layer, ve[v], 0, 0)),
                      pl.BlockSpec((None, None, de, d), lambda v, vt, ve, lo, hi: (layer, ve[v], 0, 0))],
            out_specs=pl.BlockSpec((tm, d), lambda v, vt, ve, lo, hi: (vt[v], 0)),
            scratch_shapes=[pltpu.VMEM((d, de), BF16), pltpu.VMEM((d, de), BF16), pltpu.VMEM((de, d), BF16)]),
        out_shape=jax.ShapeDtypeStruct((p, d), F32),
        compiler_params=_params(("arbitrary",)),
        name="moe_experts",
    )(*visits, xs, wg, wu, wd)


def _combine_kernel(pos_ref, npos_ref, x_ref, gate_ref, meta_ref, ys_ref, o_ref, ybuf, sems):
    i = pl.program_id(0)
    n = pl.num_programs(0)
    tm = x_ref.shape[0]

    def fetch(p_ref, slot):
        def body(r, carry):
            for k in range(2):
                _row_copy(ys_ref, ybuf.at[slot, k], sems.at[slot], p_ref[0, 0, 2 * r + k], r).start(priority=k)
            return carry
        lax.fori_loop(0, tm, body, 0)

    @pl.when(i == 0)
    def _():
        fetch(pos_ref, 0)

    @pl.when(i + 1 < n)
    def _():
        fetch(npos_ref, (i + 1) % 2)

    slot = i % 2

    _wait_rows(ys_ref, sems.at[slot], 2 * tm)

    def body(r, carry):
        rows = pl.ds(pl.multiple_of(r * ADA_BLOCK, ADA_BLOCK), ADA_BLOCK)
        meta = meta_ref[rows, :]
        y = meta[:, META_W:META_W + 1] * ybuf[slot, 0, rows, :] + meta[:, META_W + 1:META_W + 2] * ybuf[slot, 1, rows, :]
        o_ref[rows, :] = x_ref[rows, :] + gate_ref[pl.ds(r, 1), :] * y
        return carry
    lax.fori_loop(0, tm // ADA_BLOCK, body, 0)


def _combine(x, gate, meta, ys, pos, tm):
    t, d = x.shape
    nb = tm // ADA_BLOCK
    n = t // tm
    pos3 = pos.reshape(n, 1, 2 * tm)
    return pl.pallas_call(
        _combine_kernel,
        grid=(n,),
        in_specs=[pl.BlockSpec((1, 1, 2 * tm), lambda i: (i, 0, 0), memory_space=pltpu.SMEM),
                  pl.BlockSpec((1, 1, 2 * tm), lambda i: (jnp.minimum(i + 1, n - 1), 0, 0), memory_space=pltpu.SMEM),
                  pl.BlockSpec((tm, d), lambda i: (i, 0)),
                  pl.BlockSpec((nb, d), lambda i: (i, 0)),
                  pl.BlockSpec((tm, LANES), lambda i: (i, 0)),
                  pl.BlockSpec(memory_space=pl.ANY)],
        out_specs=pl.BlockSpec((tm, d), lambda i: (i, 0)),
        out_shape=jax.ShapeDtypeStruct((t, d), F32),
        scratch_shapes=[pltpu.VMEM((2, 2, tm, d), F32), pltpu.SemaphoreType.DMA((2,))],
        compiler_params=_params(("arbitrary",)),
        name="moe_combine",
    )(pos3, pos3, x, gate, meta, ys)


def _expert_visits(cnt, n_tiles):
    tile = EXPERT_TILE
    ends = jnp.cumsum(cnt)
    starts = ends - cnt
    first_tile = starts // tile
    n_vis = jnp.where(cnt > 0, (ends - 1) // tile - first_tile + 1, 0)
    vis_end = jnp.cumsum(n_vis)
    v = jnp.arange(n_tiles + N_EXPERTS - 1, dtype=jnp.int32)
    e = jnp.minimum(jnp.sum(vis_end[None, :] <= v[:, None], axis=1), N_EXPERTS - 1).astype(jnp.int32)
    real = v < vis_end[-1]
    tile_id = jnp.where(real, first_tile[e] + v - (vis_end - n_vis)[e], n_tiles - 1).astype(jnp.int32)
    lo = jnp.where(real, jnp.maximum(starts[e] - tile_id * tile, 0), 0).astype(jnp.int32)
    hi = jnp.where(real, jnp.minimum(ends[e] - tile_id * tile, tile), 0).astype(jnp.int32)
    return tile_id, e, lo, hi


def _hier_moe(x, shift, scale, gate, g, w_group, b_group, w_er, b_er, layer, wg, wu, wd, tm):
    t, d = x.shape
    w_router = jnp.concatenate([w_group, jnp.moveaxis(w_er, 0, 1).reshape(d, N_EXPERTS)], axis=1)
    b_router = jnp.concatenate([b_group, b_er.reshape(N_EXPERTS)])
    n_route = N_GROUPS + N_EXPERTS
    w_router = jnp.pad(w_router, ((0, 0), (0, LANES - n_route)))
    b_router = jnp.pad(b_router, (0, LANES - n_route)).reshape(1, LANES)
    h, meta, counts = _router(x, shift, scale, g, w_router, b_router, min(tm, ROUTER_TILE))

    expert = meta[:, META_E:META_E + 2].astype(jnp.int32)
    rank = meta[:, META_R:META_R + 2].astype(jnp.int32)
    cnt = counts[0, :N_EXPERTS].astype(jnp.int32)
    starts = jnp.cumsum(cnt) - cnt
    pos = jnp.sum(jnp.where(expert[:, :, None] == jnp.arange(N_EXPERTS), starts, 0), axis=-1) + rank
    visits = _expert_visits(cnt, 2 * t // EXPERT_TILE)

    xs = _dispatch(h, pos, min(tm, GATHER_CHUNK))
    ys = _expert_mlp(xs, visits, layer, wg, wu, wd)
    return _combine(x, gate, meta, ys, pos, min(tm, COMBINE_TILE))


def _key_blocks(k, vt_src, ki, n_pad, kb):
    b, s, _ = k.shape
    padk = lambda a: jnp.pad(a.astype(BF16), ((0, 0), (0, n_pad - s), (0, 0)))
    nkb = n_pad // kb
    k_blk = padk(k).reshape(b, nkb, kb, k.shape[2])
    vt_blk = jnp.swapaxes(padk(vt_src).reshape(b, nkb, kb, vt_src.shape[2]), 2, 3)
    ki_blk = padk(ki).reshape(b, nkb, kb, ki.shape[2])
    return k_blk, vt_blk, ki_blk


def kernel(x_prompt, x_sample, c_prompt, c_sample, cache_conv, state_ret, cache_k, cache_v, cache_kidx, norm_mix_g, norm_ffn_g, w_ada, b_ada, cr_w_in, conv_w, conv_b, conv_norm_g, conv_norm_b, ret_norm_g, cr_w_out, dsa_w_in, q_norm_g, k_norm_g, kidx_norm_g, dsa_w_out, moe_w_group, moe_b_group, moe_w_erouter, moe_b_erouter, moe_w_gate, moe_w_up, moe_w_down):
    bp, lp, d = x_prompt.shape
    bs, ls, _ = x_sample.shape
    tp, ts = bp * lp, bs * ls
    t = tp + ts
    depth = w_ada.shape[0]
    past = cache_k.shape[2]
    d_conv = conv_w.shape[2]
    d_ret = ret_norm_g.shape[1]
    d_q = dsa_w_out.shape[1]
    d_kv = cache_k.shape[3] * cache_k.shape[4]
    tm = _pick(math.gcd(tp, ts), (1024, 512, 256, 128))
    groups = ((0, bp, lp), (tp, bs, ls))

    x = jnp.concatenate([x_prompt.reshape(tp, d), x_sample.reshape(ts, d)], axis=0)

    c_all = jnp.concatenate([c_prompt, c_sample], axis=0)
    n_c = c_all.shape[0]
    c_all = jnp.pad(c_all, ((0, -n_c % 8), (0, 0)))
    ada = _ada(c_all, w_ada, b_ada)
    per_block = lambda a, n: jnp.broadcast_to(a[:, :, None, :], a.shape[:2] + (n, a.shape[2])).reshape(depth, -1, a.shape[2])
    ada_blk = jnp.concatenate([per_block(ada[:, :bp], lp // ADA_BLOCK),
                               per_block(ada[:, bp:bp + bs], ls // ADA_BLOCK)], axis=1)

    new_conv, new_ret, new_k, new_v, new_kidx = [], [], [], [], []
    for i in range(depth):
        sh1, sc1, g1, sh2, sc2, g2 = [ada_blk[i, :, m * d:(m + 1) * d] for m in range(6)]
        j = i // 2
        if i % 2 == 0:
            u = _modulated_matmul(x, sh1, sc1, norm_mix_g[i], cr_w_in[j].astype(BF16), tm, "cr_in_proj")
            a_out, b_out, bufs, states = [], [], [], []
            for gi, (row0, batch, seq) in enumerate(groups):
                buf0 = jnp.zeros((batch, CONV_WIDTH - 1, d_conv), F32) if gi == 0 else cache_conv[j]
                st0 = jnp.zeros((batch,) + state_ret.shape[2:], F32) if gi == 0 else state_ret[j]
                a, nbuf = _conv_branch(u, row0, batch, seq, buf0, conv_w[j], conv_b[j],
                                       conv_norm_g[j], conv_norm_b[j], d_conv)
                bo, nst = _retention_branch(u, row0, batch, seq, st0, ret_norm_g[j], 2 * d_conv // d_ret)
                a_out.append(a); b_out.append(bo); bufs.append(nbuf); states.append(nst)
            new_conv.append(bufs)
            new_ret.append(states)
            w_out = cr_w_out[j].astype(BF16)
            x = _outproj_residual([jnp.concatenate(a_out, axis=0), jnp.concatenate(b_out, axis=0)],
                                  [w_out[:d_conv], w_out[d_conv:]], x, g1, tm, "cr_out_proj")
        else:
            w_in = dsa_w_in[j].astype(BF16)
            n_main = d_q + 2 * d_kv + IDX_HEADS * IDX_DIM
            w_x = jnp.pad(w_in[:, n_main:], ((0, 0), (0, LANES - (w_in.shape[1] - n_main))))
            u, ux = _modulated_matmul(x, sh1, sc1, norm_mix_g[i], w_in[:, :n_main], tm, "dsa_in_proj", w_extra=w_x)
            q, k, qidx, kidx = _qk_norms(u, ux, q_norm_g[j], k_norm_g[j], kidx_norm_g[j], tm, d_q, d_kv)
            v = u[:, d_q + d_kv:d_q + 2 * d_kv]
            widx = ux[:, IDX_DIM:IDX_DIM + IDX_HEADS]
            outs, ks, vs, kis = [], [], [], []
            for gi, (row0, batch, seq) in enumerate(groups):
                rows = slice(row0, row0 + batch * seq)
                kg = k[rows].reshape(batch, seq, d_kv)
                vg = v[rows].reshape(batch, seq, d_kv)
                kig = kidx[rows].reshape(batch, seq, IDX_DIM)
                ks.append(kg); vs.append(vg); kis.append(kig)
                g_past = 0 if gi == 0 else past
                if g_past:
                    k_all = jnp.concatenate([cache_k[j].reshape(batch, past, d_kv), kg], axis=1)
                    v_all = jnp.concatenate([cache_v[j].reshape(batch, past, d_kv), vg], axis=1)
                    ki_all = jnp.concatenate([cache_kidx[j], kig], axis=1)
                else:
                    k_all, v_all, ki_all = kg, vg, kig
                n_keys = g_past + seq
                kb = 256 if n_keys >= 256 else 128
                n_pad = -(-n_keys // kb) * kb
                k_blk, vt_blk, ki_blk = _key_blocks(k_all, v_all, ki_all, n_pad, kb)
                outs.append(_sparse_attention(q, qidx, widx[rows], k_blk, vt_blk, ki_blk,
                                              row0, batch, seq, g_past, n_keys, kb))
            new_k.append(ks); new_v.append(vs); new_kidx.append(kis)
            x = _outproj_residual([jnp.concatenate(outs, axis=0)], [dsa_w_out[j].astype(BF16)],
                                  x, g1, tm, "dsa_out_proj")
        x = _hier_moe(x, sh2, sc2, g2, norm_ffn_g[i], moe_w_group[i], moe_b_group[i], moe_w_erouter[i],
                      moe_b_erouter[i], i, moe_w_gate, moe_w_up, moe_w_down, tm)

    kv_heads, hd = cache_k.shape[3], cache_k.shape[4]
    stack = lambda per_layer, gi, shape: jnp.stack([lay[gi].reshape(shape) for lay in per_layer])
    return (x[:tp].reshape(bp, lp, d), x[tp:].reshape(bs, ls, d),
            stack(new_conv, 0, (bp, CONV_WIDTH - 1, d_conv)), stack(new_conv, 1, (bs, CONV_WIDTH - 1, d_conv)),
            stack(new_ret, 0, (bp,) + state_ret.shape[2:]), stack(new_ret, 1, (bs,) + state_ret.shape[2:]),
            stack(new_k, 0, (bp, lp, kv_heads, hd)), stack(new_k, 1, (bs, ls, kv_heads, hd)),
            stack(new_v, 0, (bp, lp, kv_heads, hd)), stack(new_v, 1, (bs, ls, kv_heads, hd)),
            stack(new_kidx, 0, (bp, lp, IDX_DIM)), stack(new_kidx, 1, (bs, ls, IDX_DIM)))
```

```python
import functools
import math

import jax
import jax.numpy as jnp
from jax import lax
from jax.experimental import pallas as pl
from jax.experimental.pallas import tpu as pltpu

F32 = jnp.float32
BF16 = jnp.bfloat16

EPS = 1e-6
CHUNK = 64
ADA_BLOCK = 32
CONV_WIDTH = 31
RET_HEADS = 8
RET_DK = 128
ATT_HEADS = 16
ATT_KV_HEADS = 4
IDX_HEADS = 16
IDX_DIM = 64
TOPK_MAX = 256
N_GROUPS = 4
EXPERTS_PER_GROUP = 8
N_EXPERTS = N_GROUPS * EXPERTS_PER_GROUP
LANES = 128
VMEM_LIMIT = 56 * 1024 * 1024
BISECT_ITERS = 30
BISECT_ROUND = 5
MASKED_DIST = 1e30
LOG2E = math.log2(math.e)
EXPERT_TILE = 256
GATHER_CHUNK = 256
ROUTER_TILE = 512
COMBINE_TILE = 256


def _params(sem, vmem=VMEM_LIMIT):
    return pltpu.CompilerParams(dimension_semantics=sem, vmem_limit_bytes=vmem)


def _pick(n, cands):
    for c in cands:
        if n % c == 0:
            return c
    raise ValueError(f"no tile in {cands} divides {n}")


def _dot(a, b):
    return jnp.dot(a, b, preferred_element_type=F32)


def _dot_nt(a, b):
    return lax.dot_general(a, b, (((1,), (1,)), ((), ())), preferred_element_type=F32)


def _dot_tn(a, b):
    return lax.dot_general(a, b, (((0,), (0,)), ((), ())), preferred_element_type=F32)


def _silu(x):
    return x * jax.nn.sigmoid(x)


def _ada_kernel(c_ref, w_ref, b_ref, o_ref):
    c = _silu(c_ref[...]).astype(BF16)
    o_ref[...] = _dot(c, w_ref[...].astype(BF16)) + b_ref[...]


def _ada(c_all, w_ada, b_ada):
    depth, d, n = w_ada.shape
    rows = c_all.shape[0]
    tn = _pick(n, (1024, 512, 256, 128))
    return pl.pallas_call(
        _ada_kernel,
        grid=(depth, n // tn),
        in_specs=[pl.BlockSpec((rows, d), lambda l, j: (0, 0)),
                  pl.BlockSpec((None, d, tn), lambda l, j: (l, 0, j)),
                  pl.BlockSpec((None, 1, tn), lambda l, j: (l, 0, j))],
        out_specs=pl.BlockSpec((None, rows, tn), lambda l, j: (l, 0, j)),
        out_shape=jax.ShapeDtypeStruct((depth, rows, n), F32),
        compiler_params=_params(("parallel", "parallel")),
        name="ada",
    )(c_all, w_ada, b_ada.reshape(depth, 1, n))


def _modulate_rows(x_ref, shift_ref, scale_ref, g_ref, store):
    nblk = x_ref.shape[0] // ADA_BLOCK

    def body(r, carry):
        rows = pl.ds(pl.multiple_of(r * ADA_BLOCK, ADA_BLOCK), ADA_BLOCK)
        x = x_ref[rows, :]
        y = x * lax.rsqrt(jnp.mean(x * x, axis=-1, keepdims=True) + EPS) * g_ref[...]
        y = y * (1.0 + scale_ref[pl.ds(r, 1), :]) + shift_ref[pl.ds(r, 1), :]
        store(rows, y)
        return carry

    lax.fori_loop(0, nblk, body, 0)


def _modmm_kernel(has_extra, x_ref, shift_ref, scale_ref, g_ref, w_ref, *refs):
    if has_extra:
        wx_ref, o_ref, ox_ref, h_ref = refs
    else:
        o_ref, h_ref = refs

    @pl.when(pl.program_id(1) == 0)
    def _():
        def store(rows, y):
            h_ref[rows, :] = y.astype(BF16)
        _modulate_rows(x_ref, shift_ref, scale_ref, g_ref, store)
        if has_extra:
            ox_ref[...] = _dot(h_ref[...], wx_ref[...])

    o_ref[...] = _dot(h_ref[...], w_ref[...])


def _modulated_matmul(x, shift, scale, g, w, tm, name, w_extra=None):
    t, d = x.shape
    n = w.shape[1]
    tn = _pick(n, (1024, 512, 256, 128))
    nb = tm // ADA_BLOCK
    in_specs = [pl.BlockSpec((tm, d), lambda i, j: (i, 0)),
                pl.BlockSpec((nb, d), lambda i, j: (i, 0)),
                pl.BlockSpec((nb, d), lambda i, j: (i, 0)),
                pl.BlockSpec((1, d), lambda i, j: (0, 0)),
                pl.BlockSpec((d, tn), lambda i, j: (0, j))]
    out_specs = pl.BlockSpec((tm, tn), lambda i, j: (i, j))
    out_shape = jax.ShapeDtypeStruct((t, n), F32)
    args = [x, shift, scale, g.reshape(1, d), w]
    if w_extra is not None:
        nx = w_extra.shape[1]
        in_specs.append(pl.BlockSpec((d, nx), lambda i, j: (0, 0)))
        out_specs = [out_specs, pl.BlockSpec((tm, nx), lambda i, j: (i, 0))]
        out_shape = [out_shape, jax.ShapeDtypeStruct((t, nx), F32)]
        args.append(w_extra)
    return pl.pallas_call(
        functools.partial(_modmm_kernel, w_extra is not None),
        grid=(t // tm, n // tn),
        in_specs=in_specs,
        out_specs=out_specs,
        out_shape=out_shape,
        scratch_shapes=[pltpu.VMEM((tm, d), BF16)],
        compiler_params=_params(("parallel", "arbitrary")),
        name=name,
    )(*args)


def _outproj_kernel(n_in, *refs):
    a_refs = refs[:n_in]
    w_refs = refs[n_in:2 * n_in]
    x_ref, gate_ref, o_ref, y_ref = refs[2 * n_in:]
    y = _dot(a_refs[0][...], w_refs[0][...])
    for a_ref, w_ref in zip(a_refs[1:], w_refs[1:]):
        y = y + _dot(a_ref[...], w_ref[...])
    y_ref[...] = y
    nblk = x_ref.shape[0] // ADA_BLOCK

    def body(r, carry):
        rows = pl.ds(pl.multiple_of(r * ADA_BLOCK, ADA_BLOCK), ADA_BLOCK)
        o_ref[rows, :] = x_ref[rows, :] + gate_ref[pl.ds(r, 1), :] * y_ref[rows, :]
        return carry

    lax.fori_loop(0, nblk, body, 0)


def _outproj_residual(acts, ws, x, gate, tm, name):
    t, d = x.shape
    tn = _pick(d, (1024, 512, 256, 128))
    nb = tm // ADA_BLOCK
    n_in = len(acts)
    in_specs = ([pl.BlockSpec((tm, a.shape[1]), lambda i, j: (i, 0)) for a in acts]
                + [pl.BlockSpec((w.shape[0], tn), lambda i, j: (0, j)) for w in ws]
                + [pl.BlockSpec((tm, tn), lambda i, j: (i, j)),
                   pl.BlockSpec((nb, tn), lambda i, j: (i, j))])
    return pl.pallas_call(
        functools.partial(_outproj_kernel, n_in),
        grid=(t // tm, d // tn),
        in_specs=in_specs,
        out_specs=pl.BlockSpec((tm, tn), lambda i, j: (i, j)),
        out_shape=jax.ShapeDtypeStruct((t, d), F32),
        scratch_shapes=[pltpu.VMEM((tm, tn), F32)],
        compiler_params=_params(("parallel", "parallel")),
        name=name,
    )(*acts, *ws, x, gate)


CONV_ROWS = 32
CONV_HIST = 32


def _conv_kernel(tl, val_ref, gate_ref, pval_ref, pgate_ref, buf_ref, w_ref, b_ref, ng_ref, nb_ref,
                 o_ref, nbuf_ref, up_ref, sh_ref):
    li = pl.program_id(1)
    hist = CONV_WIDTH - 1
    pad = CONV_HIST - hist
    glu = val_ref[...] * jax.nn.sigmoid(gate_ref[...])
    up_ref[CONV_HIST:CONV_HIST + tl, :] = glu
    up_ref[0:pad, :] = jnp.zeros((pad, up_ref.shape[1]), F32)

    @pl.when(li == 0)
    def _():
        up_ref[pad:CONV_HIST, :] = buf_ref[...]

    @pl.when(li > 0)
    def _():
        prev = pval_ref[...] * jax.nn.sigmoid(pgate_ref[...])
        up_ref[pad:CONV_HIST, :] = prev[pad:, :]

    n_sh = tl + CONV_HIST - 8
    for s in range(1, 8):
        sh_ref[s - 1, 0:n_sh, :] = up_ref[s:s + n_sh, :]

    def tap_rows(row):
        base, s = row - row % 8, row % 8
        src = up_ref if s == 0 else sh_ref.at[s - 1]
        return src[base:base + CONV_ROWS, :]

    for c in range(tl // CONV_ROWS):
        r0 = c * CONV_ROWS
        acc = jnp.zeros((CONV_ROWS, val_ref.shape[1]), F32)
        for j in range(CONV_WIDTH):
            acc = acc + tap_rows(r0 + pad + j) * w_ref[j:j + 1, :]
        acc = acc + b_ref[...]
        mu = jnp.mean(acc, axis=-1, keepdims=True)
        dlt = acc - mu
        y = dlt * lax.rsqrt(jnp.mean(dlt * dlt, axis=-1, keepdims=True) + EPS)
        y = y * ng_ref[...] + nb_ref[...]
        o_ref[r0:r0 + CONV_ROWS, :] = _silu(y).astype(o_ref.dtype)

    @pl.when(li == pl.num_programs(1) - 1)
    def _():
        nbuf_ref[...] = up_ref[CONV_HIST + tl - hist:CONV_HIST + tl, :]


def _conv_branch(u, row0, batch, seq, conv_buf, conv_w, conv_b, cn_g, cn_b, d_conv):
    tl = _pick(seq, (128, 64, 32))
    nl = seq // tl
    hist = CONV_WIDTH - 1
    rb = row0 // tl
    pb = tl // CONV_HIST
    cur = lambda col: pl.BlockSpec((tl, d_conv), lambda b, l: (rb + b * nl + l, col))
    prev = lambda col: pl.BlockSpec(
        (CONV_HIST, d_conv), lambda b, l: (jnp.maximum((rb + b * nl + l) * pb - 1, 0), col))
    vec = pl.BlockSpec((1, d_conv), lambda b, l: (0, 0))
    return pl.pallas_call(
        functools.partial(_conv_kernel, tl),
        grid=(batch, nl),
        in_specs=[cur(0), cur(1), prev(0), prev(1),
                  pl.BlockSpec((None, hist, d_conv), lambda b, l: (b, 0, 0)),
                  pl.BlockSpec((CONV_WIDTH, d_conv), lambda b, l: (0, 0)),
                  vec, vec, vec],
        out_specs=[pl.BlockSpec((tl, d_conv), lambda b, l: (b * nl + l, 0)),
                   pl.BlockSpec((None, hist, d_conv), lambda b, l: (b, 0, 0))],
        out_shape=[jax.ShapeDtypeStruct((batch * seq, d_conv), BF16),
                   jax.ShapeDtypeStruct((batch, hist, d_conv), F32)],
        scratch_shapes=[pltpu.VMEM((CONV_HIST + tl, d_conv), F32),
                        pltpu.VMEM((7, CONV_HIST + tl, d_conv), F32)],
        compiler_params=_params(("parallel", "arbitrary")),
        name="conv_branch",
    )(u, u, u, u, conv_buf, conv_w, conv_b.reshape(1, -1), cn_g.reshape(1, -1), cn_b.reshape(1, -1))


def _retention_kernel(q_ref, k_ref, v_ref, g_ref, s0_ref, din_ref, dq_ref, dk_ref, dblk_ref, rg_ref,
                      o_ref, s_out_ref, s_ref):
    ci = pl.program_id(1)

    @pl.when(ci == 0)
    def _():
        s_ref[...] = s0_ref[...]

    dv = s_ref.shape[2]
    for h in range(RET_HEADS):
        cols = slice(h * dv, (h + 1) * dv)
        q = q_ref[:, cols].astype(BF16)
        k = k_ref[:, cols] * (RET_DK ** -0.5)
        v = v_ref[:, cols].astype(BF16)
        s_prev = s_ref[h]
        sc = _dot_nt(q, k.astype(BF16)) * din_ref[h]
        o = _dot(sc.astype(BF16), v) + _dot(q, s_prev.astype(BF16)) * dq_ref[h]
        kd = (k * dk_ref[h]).astype(BF16)
        s_ref[h] = s_prev * dblk_ref[h] + _dot_tn(kd, v)
        o = o * lax.rsqrt(jnp.mean(o * o, axis=-1, keepdims=True) + EPS) * rg_ref[:, cols]
        o_ref[:, cols] = (o * _silu(g_ref[:, cols])).astype(o_ref.dtype)

    @pl.when(ci == pl.num_programs(1) - 1)
    def _():
        s_out_ref[...] = s_ref[...]


def _retention_branch(u, row0, batch, seq, state0, ret_g, col_q):
    heads, dk, dv = state0.shape[1:]
    d_ret = heads * dv
    c = _pick(seq, (256, 128, 64, 32))
    nc = seq // c
    rb = row0 // c
    lg = jnp.log1p(-(2.0 ** (-5.0 - jnp.arange(heads, dtype=F32))))
    pos = jnp.arange(c, dtype=F32)
    diff = pos[:, None] - pos[None, :]
    d_in = jnp.where(diff >= 0, jnp.exp(lg[:, None, None] * jnp.maximum(diff, 0.0)), 0.0)
    d_q = jnp.broadcast_to(jnp.exp(lg[:, None] * (pos[None, :] + 1.0))[:, :, None], (heads, c, dv))
    d_k = jnp.broadcast_to(jnp.exp(lg[:, None] * (c - 1.0 - pos[None, :]))[:, :, None], (heads, c, dk))
    d_blk = jnp.broadcast_to(jnp.exp(lg * c)[:, None, None], (heads, dk, dv))
    blk = lambda col: pl.BlockSpec((c, d_ret), lambda b, i: (rb + b * nc + i, col))
    const3 = lambda shape: pl.BlockSpec(shape, lambda b, i: (0, 0, 0))
    return pl.pallas_call(
        _retention_kernel,
        grid=(batch, nc),
        in_specs=[blk(col_q), blk(col_q + 1), blk(col_q + 2), blk(col_q + 3),
                  pl.BlockSpec((None, heads, dk, dv), lambda b, i: (b, 0, 0, 0)),
                  const3((heads, c, c)), const3((heads, c, dv)), const3((heads, c, dk)),
                  const3((heads, dk, dv)),
                  pl.BlockSpec((1, d_ret), lambda b, i: (0, 0))],
        out_specs=[pl.BlockSpec((c, d_ret), lambda b, i: (b * nc + i, 0)),
                   pl.BlockSpec((None, heads, dk, dv), lambda b, i: (b, 0, 0, 0))],
        out_shape=[jax.ShapeDtypeStruct((batch * seq, d_ret), BF16),
                   jax.ShapeDtypeStruct(state0.shape, F32)],
        scratch_shapes=[pltpu.VMEM((heads, dk, dv), F32)],
        compiler_params=_params(("parallel", "arbitrary")),
        name="retention_branch",
    )(u, u, u, u, state0, d_in, d_q, d_k, d_blk, ret_g.reshape(1, d_ret))


def _head_rms(x, g, hd):
    outs = []
    for h in range(x.shape[1] // hd):
        xh = x[:, h * hd:(h + 1) * hd]
        outs.append(xh * lax.rsqrt(jnp.mean(xh * xh, axis=-1, keepdims=True) + EPS) * g)
    return outs


def _qknorm_kernel(q_ref, k_ref, v_ref, qi_ref, ki_ref, qg_ref, kg_ref, kig_ref,
                   qo_ref, ko_ref, qio_ref, kio_ref, kb_ref, kib_ref, vt_ref):
    hd = qg_ref.shape[1]
    qio_ref[...] = qi_ref[...].astype(qio_ref.dtype)
    q_scale = hd ** -0.5 * LOG2E
    for h, qh in enumerate(_head_rms(q_ref[...], qg_ref[...], hd)):
        qo_ref[:, h * hd:(h + 1) * hd] = (qh * q_scale).astype(qo_ref.dtype)
    for h, kh in enumerate(_head_rms(k_ref[...], kg_ref[...], hd)):
        ko_ref[:, h * hd:(h + 1) * hd] = kh
        kb_ref[:, h * hd:(h + 1) * hd] = kh.astype(kb_ref.dtype)
    ki = ki_ref[:, :IDX_DIM]
    ki = ki * lax.rsqrt(jnp.mean(ki * ki, axis=-1, keepdims=True) + EPS) * kig_ref[...]
    kio_ref[...] = ki
    kib_ref[...] = ki.astype(kib_ref.dtype)
    blk = vt_ref.shape[2]
    for c in range(vt_ref.shape[0]):
        vt_ref[c] = v_ref[c * blk:(c + 1) * blk, :].T.astype(vt_ref.dtype)


def _qk_norms(u, ux, q_g, k_g, kidx_g, tm, d_q, d_kv):
    t = u.shape[0]
    hd = q_g.shape[0]
    d_qi = IDX_HEADS * IDX_DIM
    blk = min(tm, 256)
    return pl.pallas_call(
        _qknorm_kernel,
        grid=(t // tm,),
        in_specs=[pl.BlockSpec((tm, d_q), lambda i: (i, 0)),
                  pl.BlockSpec((tm, d_kv), lambda i: (i, d_q // d_kv)),
                  pl.BlockSpec((tm, d_kv), lambda i: (i, d_q // d_kv + 1)),
                  pl.BlockSpec((tm, d_qi), lambda i: (i, (d_q + 2 * d_kv) // d_qi)),
                  pl.BlockSpec((tm, LANES), lambda i: (i, 0)),
                  pl.BlockSpec((1, hd), lambda i: (0, 0)),
                  pl.BlockSpec((1, hd), lambda i: (0, 0)),
                  pl.BlockSpec((1, IDX_DIM), lambda i: (0, 0))],
        out_specs=[pl.BlockSpec((tm, d_q), lambda i: (i, 0)),
                   pl.BlockSpec((tm, d_kv), lambda i: (i, 0)),
                   pl.BlockSpec((tm, d_qi), lambda i: (i, 0)),
                   pl.BlockSpec((tm, IDX_DIM), lambda i: (i, 0)),
                   pl.BlockSpec((tm, d_kv), lambda i: (i, 0)),
                   pl.BlockSpec((tm, IDX_DIM), lambda i: (i, 0)),
                   pl.BlockSpec((tm // blk, d_kv, blk), lambda i: (i, 0, 0))],
        out_shape=[jax.ShapeDtypeStruct((t, d_q), BF16),
                   jax.ShapeDtypeStruct((t, d_kv), F32),
                   jax.ShapeDtypeStruct((t, d_qi), BF16),
                   jax.ShapeDtypeStruct((t, IDX_DIM), F32),
                   jax.ShapeDtypeStruct((t, d_kv), BF16),
                   jax.ShapeDtypeStruct((t, IDX_DIM), BF16),
                   jax.ShapeDtypeStruct((t // blk, d_kv, blk), BF16)],
        compiler_params=_params(("parallel",)),
        name="qk_norms",
    )(u, u, u, u, ux, q_g.reshape(1, hd), k_g.reshape(1, hd), kidx_g.reshape(1, IDX_DIM))


def _dsa_kernel(tq, rep, kb, past, n_keys, n_sel, q_ref, qi_ref, wt_ref, slope_ref, tri_ref,
                k_ref, vt_ref, ki_ref, o_ref, sc_ref, dist_ref, qg_ref, acc_ref, z_ref, rel_ref):
    t0 = pl.program_id(1) * tq
    hd = k_ref.shape[2] // ATT_KV_HEADS
    grp = ATT_HEADS // ATT_KV_HEADS
    wq = rep * tq
    n_adm_tile = jnp.minimum(((past + t0 + tq - 1) // CHUNK + 1) * CHUNK, n_keys)
    nkb = (n_adm_tile + kb - 1) // kb
    q_pos = past + t0 + lax.broadcasted_iota(jnp.int32, (1, wq), 1) % tq
    q_chunk = q_pos // CHUNK
    n_adm = jnp.minimum((q_chunk + 1) * CHUNK, n_keys).astype(F32)
    need = jnp.minimum(n_adm, float(n_sel))
    neg_inf = jnp.float32(-jnp.inf)
    w_t = wt_ref[...] * (IDX_HEADS ** -0.5 * IDX_DIM ** -0.5)

    def key_pos(i):
        return i * kb + lax.broadcasted_iota(jnp.int32, (kb, 1), 0)

    def admissible(i):
        kp = key_pos(i)
        return jnp.logical_and(kp // CHUNK <= q_chunk, kp < n_keys)

    def score_body(i, carry):
        lo, hi = carry
        ki = ki_ref[i]
        for h in range(IDX_HEADS):
            qi = jnp.concatenate([qi_ref[:, h * IDX_DIM:(h + 1) * IDX_DIM]] * rep, axis=0)
            rel_ref[h] = _dot_nt(ki, qi)
        acc = jnp.zeros((kb, wq), F32)
        for h in range(IDX_HEADS):
            acc = acc + jnp.maximum(rel_ref[h], 0.0) * w_t[h:h + 1, :]
        adm = admissible(i)
        sc_ref[i] = jnp.where(adm, acc, neg_inf)
        lo = jnp.minimum(lo, jnp.min(jnp.where(adm, acc, jnp.inf), axis=0, keepdims=True))
        hi = jnp.maximum(hi, jnp.max(jnp.where(adm, acc, neg_inf), axis=0, keepdims=True))
        return lo, hi

    lo, hi = lax.fori_loop(0, nkb, score_body,
                           (jnp.full((1, wq), jnp.inf, F32), jnp.full((1, wq), neg_inf, F32)))

    def count(pred):
        def body(i, acc):
            ones = jnp.where(pred(sc_ref[i]), 1.0, 0.0)
            parts = [ones[r:r + 8, :] for r in range(0, kb, 8)]
            while len(parts) > 1:
                parts = [parts[a] + parts[a + 1] for a in range(0, len(parts), 2)]
            return acc + parts[0]
        return jnp.sum(lax.fori_loop(0, nkb, body, jnp.zeros((8, wq), F32)), axis=0, keepdims=True)

    def bisect(_, carry):
        lo, hi, c_lo = carry
        mid = 0.5 * (lo + hi)
        c_mid = count(lambda s: s >= mid)
        ge = c_mid >= need
        return jnp.where(ge, mid, lo), jnp.where(ge, hi, mid), jnp.where(ge, c_mid, c_lo)

    def bisect_round(carry):
        rnd, lo, hi, c_lo = carry
        lo, hi, c_lo = lax.fori_loop(0, BISECT_ROUND, bisect, (lo, hi, c_lo))
        return rnd + 1, lo, hi, c_lo

    def unresolved(carry):
        rnd, _, _, c_lo = carry
        return jnp.logical_and(rnd < BISECT_ITERS // BISECT_ROUND, jnp.max(c_lo - need) > 0.0)

    _, lo, hi, c_lo = lax.while_loop(unresolved, bisect_round, (jnp.int32(0), lo, hi, n_adm))

    def write_dist(i, sel):
        dist = jnp.abs(q_pos - key_pos(i)).astype(F32)
        dist_ref[i] = jnp.where(sel, dist, MASKED_DIST)

    resolved = jnp.max(c_lo - need) <= 0.0

    @pl.when(resolved)
    def _():
        def body(i, carry):
            write_dist(i, sc_ref[i] >= lo)
            return carry
        lax.fori_loop(0, nkb, body, 0)

    @pl.when(jnp.logical_not(resolved))
    def _():
        n_above = count(lambda s: s > hi)
        room = need - n_above

        def body(i, seen):
            s = sc_ref[i]
            above = s > hi
            band = jnp.logical_and(s >= lo, jnp.logical_not(above))
            band_f = jnp.where(band, 1.0, 0.0)
            rank = _dot(tri_ref[...], band_f.astype(BF16)) + seen
            write_dist(i, jnp.logical_or(above, jnp.logical_and(band, rank <= room)))
            return seen + jnp.sum(band_f, axis=0, keepdims=True)
        lax.fori_loop(0, nkb, body, jnp.zeros((1, wq), F32))

    for g in range(ATT_KV_HEADS):
        qg_ref[g] = jnp.concatenate(
            [q_ref[:, (g * grp + r) * hd:(g * grp + r + 1) * hd] for r in range(grp)], axis=0)
    acc_ref[...] = jnp.zeros_like(acc_ref)

    def att_body(i, carry):
        ms, ls = carry
        dist = jnp.concatenate([dist_ref[i]] * (grp // rep), axis=1)
        new_ms, new_ls = [], []
        for g in range(ATT_KV_HEADS):
            z_ref[g] = _dot_nt(k_ref[i, :, g * hd:(g + 1) * hd], qg_ref[g])
        for g in range(ATT_KV_HEADS):
            z = z_ref[g] - slope_ref[g] * dist
            m_new = jnp.maximum(ms[g], jnp.max(z, axis=0, keepdims=True))
            alpha = jnp.exp2(ms[g] - m_new)
            p = jnp.exp2(z - m_new)
            new_ls.append(ls[g] * alpha + jnp.sum(p, axis=0, keepdims=True))
            new_ms.append(m_new)
            acc_ref[g] = acc_ref[g] * alpha + _dot(vt_ref[i, g * hd:(g + 1) * hd, :], p.astype(BF16))
        return tuple(new_ms), tuple(new_ls)

    _, ls = lax.fori_loop(
        0, nkb, att_body,
        (tuple(jnp.full((1, grp * tq), neg_inf, F32) for _ in range(ATT_KV_HEADS)),
         tuple(jnp.zeros((1, grp * tq), F32) for _ in range(ATT_KV_HEADS))))
    for g in range(ATT_KV_HEADS):
        out = (acc_ref[g] / ls[g]).T
        for r in range(grp):
            o_ref[:, (g * grp + r) * hd:(g * grp + r + 1) * hd] = out[r * tq:(r + 1) * tq, :].astype(o_ref.dtype)


def _sparse_attention(q, qidx, widx, k_blk, vt_blk, ki_blk, row0, batch, seq, past, n_keys, kb):
    d_q = q.shape[1]
    nkb_all = k_blk.shape[1]
    d_kv = k_blk.shape[3]
    tq = _pick(seq, (256, 128, 64, 32))
    nq = seq // tq
    rb = row0 // tq
    grp = ATT_HEADS // ATT_KV_HEADS
    rep = max(1, LANES // tq)
    wq = rep * tq
    n_sel = min(TOPK_MAX, n_keys // 4)
    widx_t = jnp.tile(jnp.swapaxes(widx.reshape(batch * nq, tq, IDX_HEADS), 1, 2), (1, 1, rep))
    slopes = LOG2E * 2.0 ** (-8.0 * jnp.arange(1, ATT_HEADS + 1, dtype=F32) / ATT_HEADS)
    slope_rows = jnp.repeat(slopes.reshape(ATT_KV_HEADS, grp), tq, axis=1).reshape(ATT_KV_HEADS, 1, grp * tq)
    tri = (jnp.arange(kb)[:, None] >= jnp.arange(kb)[None, :]).astype(BF16)
    return pl.pallas_call(
        functools.partial(_dsa_kernel, tq, rep, kb, past, n_keys, n_sel),
        grid=(batch, nq),
        in_specs=[pl.BlockSpec((tq, d_q), lambda b, i: (rb + b * nq + i, 0)),
                  pl.BlockSpec((tq, qidx.shape[1]), lambda b, i: (rb + b * nq + i, 0)),
                  pl.BlockSpec((None, IDX_HEADS, wq), lambda b, i: (b * nq + i, 0, 0)),
                  pl.BlockSpec((ATT_KV_HEADS, 1, grp * tq), lambda b, i: (0, 0, 0)),
                  pl.BlockSpec((kb, kb), lambda b, i: (0, 0)),
                  pl.BlockSpec((None, nkb_all, kb, d_kv), lambda b, i: (b, 0, 0, 0)),
                  pl.BlockSpec((None, nkb_all, d_kv, kb), lambda b, i: (b, 0, 0, 0)),
                  pl.BlockSpec((None, nkb_all, kb, IDX_DIM), lambda b, i: (b, 0, 0, 0))],
        out_specs=pl.BlockSpec((tq, d_q), lambda b, i: (b * nq + i, 0)),
        out_shape=jax.ShapeDtypeStruct((batch * seq, d_q), BF16),
        scratch_shapes=[pltpu.VMEM((nkb_all, kb, wq), F32), pltpu.VMEM((nkb_all, kb, wq), F32),
                        pltpu.VMEM((ATT_KV_HEADS, grp * tq, d_kv // ATT_KV_HEADS), BF16),
                        pltpu.VMEM((ATT_KV_HEADS, d_kv // ATT_KV_HEADS, grp * tq), F32),
                        pltpu.VMEM((ATT_KV_HEADS, kb, grp * tq), F32),
                        pltpu.VMEM((IDX_HEADS, kb, wq), F32)],
        compiler_params=_params(("parallel", "arbitrary")),
        name="sparse_attention",
    )(q, qidx, widx_t, slope_rows, tri, k_blk, vt_blk, ki_blk)


META_E, META_W, META_R = 0, 2, 4


def _router_kernel(x_ref, shift_ref, scale_ref, g_ref, whi_ref, wlo_ref, br_ref, tri_ref,
                   h_ref, meta_ref, cnt_ref, carry_ref):
    @pl.when(pl.program_id(0) == 0)
    def _():
        carry_ref[...] = jnp.zeros_like(carry_ref)

    def store(rows, y):
        h_ref[rows, :] = y
    _modulate_rows(x_ref, shift_ref, scale_ref, g_ref, store)

    h = h_ref[...]
    h_hi = h.astype(BF16)
    h_lo = (h - h_hi.astype(F32)).astype(BF16)
    logits = (_dot(h_hi, whi_ref[...]) + (_dot(h_hi, wlo_ref[...]) + _dot(h_lo, whi_ref[...]))) + br_ref[...]
    tm = logits.shape[0]
    lane = lax.broadcasted_iota(jnp.int32, (tm, LANES), 1).astype(F32)
    neg_inf = jnp.float32(-jnp.inf)

    def first_argmax(v):
        top = jnp.max(v, axis=-1, keepdims=True)
        return top, jnp.min(jnp.where(v == top, lane, float(LANES)), axis=-1, keepdims=True)

    is_group = lane < N_GROUPS
    gl = jnp.where(is_group, logits, neg_inf)
    g_top, g_sel = first_argmax(gl)
    g_w = 1.0 / jnp.sum(jnp.where(is_group, jnp.exp(gl - g_top), 0.0), axis=-1, keepdims=True)
    first = N_GROUPS + g_sel * EXPERTS_PER_GROUP
    el = jnp.where(jnp.logical_and(lane >= first, lane < first + EXPERTS_PER_GROUP), logits, neg_inf)
    v1, i1 = first_argmax(el)
    v2, i2 = first_argmax(jnp.where(lane == i1, neg_inf, el))
    e21 = jnp.exp(v2 - v1)
    w1 = g_w / (1.0 + e21)
    w2 = g_w * e21 / (1.0 + e21)
    e1 = i1 - N_GROUPS
    e2 = i2 - N_GROUPS

    oh1 = jnp.where(lane == e1, 1.0, 0.0)
    oh2 = jnp.where(lane == e2, 1.0, 0.0)
    both = oh1 + oh2
    before = _dot(tri_ref[...], both.astype(BF16)) + carry_ref[...]
    r1 = jnp.sum(before * oh1, axis=-1, keepdims=True)
    r2 = jnp.sum(before * oh2, axis=-1, keepdims=True)
    carry_ref[...] += jnp.sum(both, axis=0, keepdims=True)

    meta = jnp.zeros((tm, LANES), F32)
    for ln, val in ((META_E, e1), (META_E + 1, e2), (META_W, w1), (META_W + 1, w2),
                    (META_R, r1), (META_R + 1, r2)):
        meta = jnp.where(lane == ln, val, meta)
    meta_ref[...] = meta
    cnt_ref[...] = carry_ref[...]


def _router(x, shift, scale, g, w_router, b_router, tm):
    t, d = x.shape
    nb = tm // ADA_BLOCK
    tri = (jnp.arange(tm)[:, None] > jnp.arange(tm)[None, :]).astype(BF16)
    w_hi = w_router.astype(BF16)
    w_lo = (w_router - w_hi.astype(F32)).astype(BF16)
    return pl.pallas_call(
        _router_kernel,
        grid=(t // tm,),
        in_specs=[pl.BlockSpec((tm, d), lambda i: (i, 0)),
                  pl.BlockSpec((nb, d), lambda i: (i, 0)),
                  pl.BlockSpec((nb, d), lambda i: (i, 0)),
                  pl.BlockSpec((1, d), lambda i: (0, 0)),
                  pl.BlockSpec((d, LANES), lambda i: (0, 0)),
                  pl.BlockSpec((d, LANES), lambda i: (0, 0)),
                  pl.BlockSpec((1, LANES), lambda i: (0, 0)),
                  pl.BlockSpec((tm, tm), lambda i: (0, 0))],
        out_specs=[pl.BlockSpec((tm, d), lambda i: (i, 0)),
                   pl.BlockSpec((tm, LANES), lambda i: (i, 0)),
                   pl.BlockSpec((1, LANES), lambda i: (0, 0))],
        out_shape=[jax.ShapeDtypeStruct((t, d), F32),
                   jax.ShapeDtypeStruct((t, LANES), F32),
                   jax.ShapeDtypeStruct((1, LANES), F32)],
        scratch_shapes=[pltpu.VMEM((1, LANES), F32)],
        compiler_params=_params(("arbitrary",)),
        name="moe_router",
    )(x, shift, scale, g.reshape(1, d), w_hi, w_lo, b_router, tri)


def _row_copy(src_ref, dst_ref, sem, src_row, dst_row):
    return pltpu.make_async_copy(src_ref.at[pl.ds(src_row, 1)], dst_ref.at[pl.ds(dst_row, 1)], sem)


def _wait_rows(hbm_ref, sem, n):
    rows = hbm_ref.at[pl.ds(0, n)]
    pltpu.make_async_copy(rows, rows, sem).wait()


def _dispatch_kernel(pos_ref, h_ref, xs_ref, sem):
    n = h_ref.shape[0]

    def start(r, carry):
        for k in range(2):
            _row_copy(h_ref, xs_ref, sem, r, pos_ref[0, 0, 2 * r + k]).start(priority=k)
        return carry
    lax.fori_loop(0, n, start, 0)
    _wait_rows(xs_ref, sem, 2 * n)


def _dispatch(h, pos, ch):
    t, d = h.shape
    return pl.pallas_call(
        _dispatch_kernel,
        grid=(t // ch,),
        in_specs=[pl.BlockSpec((1, 1, 2 * ch), lambda i: (i, 0, 0), memory_space=pltpu.SMEM),
                  pl.BlockSpec((ch, d), lambda i: (i, 0))],
        out_specs=pl.BlockSpec(memory_space=pl.ANY),
        out_shape=jax.ShapeDtypeStruct((2 * t, d), F32),
        scratch_shapes=[pltpu.SemaphoreType.DMA(())],
        compiler_params=_params(("arbitrary",)),
        name="moe_dispatch",
    )(pos.reshape(t // ch, 1, 2 * ch), h)


def _cast_rows(src_ref, dst_ref, rows):
    def body(c, carry):
        r = pl.ds(pl.multiple_of(c * rows, rows), rows)
        dst_ref[r, :] = src_ref[r, :].astype(dst_ref.dtype)
        return carry
    lax.fori_loop(0, src_ref.shape[0] // rows, body, 0)


def _expert_kernel(vt_ref, ve_ref, vlo_ref, vhi_ref, x_ref, wg_ref, wu_ref, wd_ref, o_ref,
                   wgb_ref, wub_ref, wdb_ref):
    v = pl.program_id(0)
    lo, hi = vlo_ref[v], vhi_ref[v]
    prev = jnp.maximum(v - 1, 0)

    @pl.when(jnp.logical_and(hi > lo, jnp.logical_or(v == 0, ve_ref[prev] != ve_ref[v])))
    def _():
        _cast_rows(wg_ref, wgb_ref, 64)
        _cast_rows(wu_ref, wub_ref, 64)
        _cast_rows(wd_ref, wdb_ref, 16)

    @pl.when(hi > lo)
    def _():
        x = x_ref[...].astype(BF16)
        a = _dot(x, wgb_ref[...])
        b = _dot(x, wub_ref[...])
        y = _dot((_silu(a) * b).astype(BF16), wdb_ref[...])
        row = lax.broadcasted_iota(jnp.int32, (x.shape[0], 1), 0)
        mine = jnp.logical_and(row >= lo, row < hi)
        first = jnp.logical_or(v == 0, vt_ref[prev] != vt_ref[v])

        @pl.when(first)
        def _():
            o_ref[...] = jnp.where(mine, y, 0.0)

        @pl.when(jnp.logical_not(first))
        def _():
            o_ref[...] = jnp.where(mine, y, o_ref[...])


def _expert_mlp(xs, visits, layer, wg, wu, wd):
    p, d = xs.shape
    de = wg.shape[3]
    tm = EXPERT_TILE
    n_visits = visits[0].shape[0]
    return pl.pallas_call(
        _expert_kernel,
        grid_spec=pltpu.PrefetchScalarGridSpec(
            num_scalar_prefetch=4,
            grid=(n_visits,),
            in_specs=[pl.BlockSpec((tm, d), lambda v, vt, ve, lo, hi: (vt[v], 0)),
                      pl.BlockSpec((None, None, d, de), lambda v, vt, ve, lo, hi: (layer, ve[v], 0, 0)),
                      pl.BlockSpec((None, None, d, de), lambda v, vt, ve, lo, hi: (layer, ve[v], 0, 0)),
                      pl.BlockSpec((None, None, de, d), lambda v, vt, ve, lo, hi: (layer, ve[v], 0, 0))],
            out_specs=pl.BlockSpec((tm, d), lambda v, vt, ve, lo, hi: (vt[v], 0)),
            scratch_shapes=[pltpu.VMEM((d, de), BF16), pltpu.VMEM((d, de), BF16), pltpu.VMEM((de, d), BF16)]),
        out_shape=jax.ShapeDtypeStruct((p, d), F32),
        compiler_params=_params(("arbitrary",)),
        name="moe_experts",
    )(*visits, xs, wg, wu, wd)


def _combine_kernel(pos_ref, npos_ref, x_ref, gate_ref, meta_ref, ys_ref, o_ref, ybuf, sems):
    i = pl.program_id(0)
    n = pl.num_programs(0)
    tm = x_ref.shape[0]

    def fetch(p_ref, slot):
        def body(r, carry):
            for k in range(2):
                _row_copy(ys_ref, ybuf.at[slot, k], sems.at[slot], p_ref[0, 0, 2 * r + k], r).start(priority=k)
            return carry
        lax.fori_loop(0, tm, body, 0)

    @pl.when(i == 0)
    def _():
        fetch(pos_ref, 0)

    @pl.when(i + 1 < n)
    def _():
        fetch(npos_ref, (i + 1) % 2)

    slot = i % 2

    _wait_rows(ys_ref, sems.at[slot], 2 * tm)

    def body(r, carry):
        rows = pl.ds(pl.multiple_of(r * ADA_BLOCK, ADA_BLOCK), ADA_BLOCK)
        meta = meta_ref[rows, :]
        y = meta[:, META_W:META_W + 1] * ybuf[slot, 0, rows, :] + meta[:, META_W + 1:META_W + 2] * ybuf[slot, 1, rows, :]
        o_ref[rows, :] = x_ref[rows, :] + gate_ref[pl.ds(r, 1), :] * y
        return carry
    lax.fori_loop(0, tm // ADA_BLOCK, body, 0)


def _combine(x, gate, meta, ys, pos, tm):
    t, d = x.shape
    nb = tm // ADA_BLOCK
    n = t // tm
    pos3 = pos.reshape(n, 1, 2 * tm)
    return pl.pallas_call(
        _combine_kernel,
        grid=(n,),
        in_specs=[pl.BlockSpec((1, 1, 2 * tm), lambda i: (i, 0, 0), memory_space=pltpu.SMEM),
                  pl.BlockSpec((1, 1, 2 * tm), lambda i: (jnp.minimum(i + 1, n - 1), 0, 0), memory_space=pltpu.SMEM),
                  pl.BlockSpec((tm, d), lambda i: (i, 0)),
                  pl.BlockSpec((nb, d), lambda i: (i, 0)),
                  pl.BlockSpec((tm, LANES), lambda i: (i, 0)),
                  pl.BlockSpec(memory_space=pl.ANY)],
        out_specs=pl.BlockSpec((tm, d), lambda i: (i, 0)),
        out_shape=jax.ShapeDtypeStruct((t, d), F32),
        scratch_shapes=[pltpu.VMEM((2, 2, tm, d), F32), pltpu.SemaphoreType.DMA((2,))],
        compiler_params=_params(("arbitrary",)),
        name="moe_combine",
    )(pos3, pos3, x, gate, meta, ys)


def _expert_visits(cnt, n_tiles):
    tile = EXPERT_TILE
    ends = jnp.cumsum(cnt)
    starts = ends - cnt
    first_tile = starts // tile
    n_vis = jnp.where(cnt > 0, (ends - 1) // tile - first_tile + 1, 0)
    vis_end = jnp.cumsum(n_vis)
    v = jnp.arange(n_tiles + N_EXPERTS - 1, dtype=jnp.int32)
    e = jnp.minimum(jnp.sum(vis_end[None, :] <= v[:, None], axis=1), N_EXPERTS - 1).astype(jnp.int32)
    real = v < vis_end[-1]
    tile_id = jnp.where(real, first_tile[e] + v - (vis_end - n_vis)[e], n_tiles - 1).astype(jnp.int32)
    lo = jnp.where(real, jnp.maximum(starts[e] - tile_id * tile, 0), 0).astype(jnp.int32)
    hi = jnp.where(real, jnp.minimum(ends[e] - tile_id * tile, tile), 0).astype(jnp.int32)
    return tile_id, e, lo, hi


def _hier_moe(x, shift, scale, gate, g, w_group, b_group, w_er, b_er, layer, wg, wu, wd, tm):
    t, d = x.shape
    w_router = jnp.concatenate([w_group, jnp.moveaxis(w_er, 0, 1).reshape(d, N_EXPERTS)], axis=1)
    b_router = jnp.concatenate([b_group, b_er.reshape(N_EXPERTS)])
    n_route = N_GROUPS + N_EXPERTS
    w_router = jnp.pad(w_router, ((0, 0), (0, LANES - n_route)))
    b_router = jnp.pad(b_router, (0, LANES - n_route)).reshape(1, LANES)
    h, meta, counts = _router(x, shift, scale, g, w_router, b_router, min(tm, ROUTER_TILE))

    expert = meta[:, META_E:META_E + 2].astype(jnp.int32)
    rank = meta[:, META_R:META_R + 2].astype(jnp.int32)
    cnt = counts[0, :N_EXPERTS].astype(jnp.int32)
    starts = jnp.cumsum(cnt) - cnt
    pos = jnp.sum(jnp.where(expert[:, :, None] == jnp.arange(N_EXPERTS), starts, 0), axis=-1) + rank
    visits = _expert_visits(cnt, 2 * t // EXPERT_TILE)

    xs = _dispatch(h, pos, min(tm, GATHER_CHUNK))
    ys = _expert_mlp(xs, visits, layer, wg, wu, wd)
    return _combine(x, gate, meta, ys, pos, min(tm, COMBINE_TILE))


def _cached_keys_kernel(past, seq, ck_ref, cv_ref, cki_ref, kn_ref, vn_ref, kin_ref,
                        k_ref, vt_ref, ki_ref, ks_ref, vs_ref, kis_ref):
    nkb, kb = k_ref.shape[0], k_ref.shape[1]
    n_keys = past + seq
    for cache_ref, new_ref, stage in ((ck_ref, kn_ref, ks_ref), (cv_ref, vn_ref, vs_ref), (cki_ref, kin_ref, kis_ref)):
        stage[0:past, :] = cache_ref[...]
        stage[past:n_keys, :] = new_ref[...]
        if nkb * kb > n_keys:
            stage[n_keys:, :] = jnp.zeros((nkb * kb - n_keys, stage.shape[1]), F32)
    for i in range(nkb):
        rows = slice(i * kb, (i + 1) * kb)
        k_ref[i] = ks_ref[rows, :].astype(k_ref.dtype)
        vt_ref[i] = vs_ref[rows, :].T.astype(vt_ref.dtype)
        ki_ref[i] = kis_ref[rows, :].astype(ki_ref.dtype)


def _cached_key_blocks(cache_k, cache_v, cache_ki, k_new, v_new, ki_new, kb):
    b, past, d_kv = cache_k.shape
    seq = k_new.shape[1]
    nkb = -(-(past + seq) // kb)
    per_stream = lambda a: pl.BlockSpec((None,) + a.shape[1:], lambda i: (i, 0, 0))
    blocked = lambda r, c: pl.BlockSpec((None, nkb, r, c), lambda i: (i, 0, 0, 0))
    args = (cache_k, cache_v, cache_ki, k_new, v_new, ki_new)
    return pl.pallas_call(
        functools.partial(_cached_keys_kernel, past, seq),
        grid=(b,),
        in_specs=[per_stream(a) for a in args],
        out_specs=[blocked(kb, d_kv), blocked(d_kv, kb), blocked(kb, IDX_DIM)],
        out_shape=[jax.ShapeDtypeStruct((b, nkb, kb, d_kv), BF16),
                   jax.ShapeDtypeStruct((b, nkb, d_kv, kb), BF16),
                   jax.ShapeDtypeStruct((b, nkb, kb, IDX_DIM), BF16)],
        scratch_shapes=[pltpu.VMEM((nkb * kb, d_kv), F32), pltpu.VMEM((nkb * kb, d_kv), F32),
                        pltpu.VMEM((nkb * kb, IDX_DIM), F32)],
        compiler_params=_params(("parallel",)),
        name="cached_key_blocks",
    )(*args)


def _key_blocks(k, vt_src, ki, n_pad, kb):
    b, s, _ = k.shape
    padk = lambda a: jnp.pad(a.astype(BF16), ((0, 0), (0, n_pad - s), (0, 0)))
    nkb = n_pad // kb
    k_blk = padk(k).reshape(b, nkb, kb, k.shape[2])
    vt_blk = jnp.swapaxes(padk(vt_src).reshape(b, nkb, kb, vt_src.shape[2]), 2, 3)
    ki_blk = padk(ki).reshape(b, nkb, kb, ki.shape[2])
    return k_blk, vt_blk, ki_blk


def kernel(x_prompt, x_sample, c_prompt, c_sample, cache_conv, state_ret, cache_k, cache_v, cache_kidx, norm_mix_g, norm_ffn_g, w_ada, b_ada, cr_w_in, conv_w, conv_b, conv_norm_g, conv_norm_b, ret_norm_g, cr_w_out, dsa_w_in, q_norm_g, k_norm_g, kidx_norm_g, dsa_w_out, moe_w_group, moe_b_group, moe_w_erouter, moe_b_erouter, moe_w_gate, moe_w_up, moe_w_down):
    bp, lp, d = x_prompt.shape
    bs, ls, _ = x_sample.shape
    tp, ts = bp * lp, bs * ls
    t = tp + ts
    depth = w_ada.shape[0]
    past = cache_k.shape[2]
    d_conv = conv_w.shape[2]
    d_ret = ret_norm_g.shape[1]
    d_q = dsa_w_out.shape[1]
    d_kv = cache_k.shape[3] * cache_k.shape[4]
    tm = _pick(math.gcd(tp, ts), (1024, 512, 256, 128))
    groups = ((0, bp, lp), (tp, bs, ls))

    x = jnp.concatenate([x_prompt.reshape(tp, d), x_sample.reshape(ts, d)], axis=0)

    c_all = jnp.concatenate([c_prompt, c_sample], axis=0)
    n_c = c_all.shape[0]
    c_all = jnp.pad(c_all, ((0, -n_c % 8), (0, 0)))
    ada = _ada(c_all, w_ada, b_ada)
    per_block = lambda a, n: jnp.broadcast_to(a[:, :, None, :], a.shape[:2] + (n, a.shape[2])).reshape(depth, -1, a.shape[2])
    ada_blk = jnp.concatenate([per_block(ada[:, :bp], lp // ADA_BLOCK),
                               per_block(ada[:, bp:bp + bs], ls // ADA_BLOCK)], axis=1)

    new_conv, new_ret, new_k, new_v, new_kidx = [], [], [], [], []
    for i in range(depth):
        sh1, sc1, g1, sh2, sc2, g2 = [ada_blk[i, :, m * d:(m + 1) * d] for m in range(6)]
        j = i // 2
        if i % 2 == 0:
            u = _modulated_matmul(x, sh1, sc1, norm_mix_g[i], cr_w_in[j].astype(BF16), tm, "cr_in_proj")
            a_out, b_out, bufs, states = [], [], [], []
            for gi, (row0, batch, seq) in enumerate(groups):
                buf0 = jnp.zeros((batch, CONV_WIDTH - 1, d_conv), F32) if gi == 0 else cache_conv[j]
                st0 = jnp.zeros((batch,) + state_ret.shape[2:], F32) if gi == 0 else state_ret[j]
                a, nbuf = _conv_branch(u, row0, batch, seq, buf0, conv_w[j], conv_b[j],
                                       conv_norm_g[j], conv_norm_b[j], d_conv)
                bo, nst = _retention_branch(u, row0, batch, seq, st0, ret_norm_g[j], 2 * d_conv // d_ret)
                a_out.append(a); b_out.append(bo); bufs.append(nbuf); states.append(nst)
            new_conv.append(bufs)
            new_ret.append(states)
            w_out = cr_w_out[j].astype(BF16)
            x = _outproj_residual([jnp.concatenate(a_out, axis=0), jnp.concatenate(b_out, axis=0)],
                                  [w_out[:d_conv], w_out[d_conv:]], x, g1, tm, "cr_out_proj")
        else:
            w_in = dsa_w_in[j].astype(BF16)
            n_main = d_q + 2 * d_kv + IDX_HEADS * IDX_DIM
            w_x = jnp.pad(w_in[:, n_main:], ((0, 0), (0, LANES - (w_in.shape[1] - n_main))))
            u, ux = _modulated_matmul(x, sh1, sc1, norm_mix_g[i], w_in[:, :n_main], tm, "dsa_in_proj", w_extra=w_x)
            q, k, qidx, kidx, k_bf, ki_bf, vt_bf = _qk_norms(u, ux, q_norm_g[j], k_norm_g[j], kidx_norm_g[j],
                                                             tm, d_q, d_kv)
            v = u[:, d_q + d_kv:d_q + 2 * d_kv]
            widx = ux[:, IDX_DIM:IDX_DIM + IDX_HEADS]
            outs, ks, vs, kis = [], [], [], []
            for gi, (row0, batch, seq) in enumerate(groups):
                rows = slice(row0, row0 + batch * seq)
                kg = k[rows].reshape(batch, seq, d_kv)
                vg = v[rows].reshape(batch, seq, d_kv)
                kig = kidx[rows].reshape(batch, seq, IDX_DIM)
                ks.append(kg); vs.append(vg); kis.append(kig)
                g_past = 0 if gi == 0 else past
                n_keys = g_past + seq
                kb = 256 if n_keys >= 256 else 128
                n_pad = -(-n_keys // kb) * kb
                if g_past:
                    k_blk, vt_blk, ki_blk = _cached_key_blocks(
                        cache_k[j].reshape(batch, past, d_kv), cache_v[j].reshape(batch, past, d_kv),
                        cache_kidx[j], kg, vg, kig, kb)
                elif n_pad == n_keys and vt_bf.shape[2] == kb:
                    blocks = slice(row0 // kb, (row0 + batch * seq) // kb)
                    k_blk = k_bf[rows].reshape(batch, seq // kb, kb, d_kv)
                    vt_blk = vt_bf[blocks].reshape(batch, seq // kb, d_kv, kb)
                    ki_blk = ki_bf[rows].reshape(batch, seq // kb, kb, IDX_DIM)
                else:
                    k_blk, vt_blk, ki_blk = _key_blocks(kg, vg, kig, n_pad, kb)
                outs.append(_sparse_attention(q, qidx, widx[rows], k_blk, vt_blk, ki_blk,
                                              row0, batch, seq, g_past, n_keys, kb))
            new_k.append(ks); new_v.append(vs); new_kidx.append(kis)
            x = _outproj_residual([jnp.concatenate(outs, axis=0)], [dsa_w_out[j].astype(BF16)],
                                  x, g1, tm, "dsa_out_proj")
        x = _hier_moe(x, sh2, sc2, g2, norm_ffn_g[i], moe_w_group[i], moe_b_group[i], moe_w_erouter[i],
                      moe_b_erouter[i], i, moe_w_gate, moe_w_up, moe_w_down, tm)

    kv_heads, hd = cache_k.shape[3], cache_k.shape[4]
    stack = lambda per_layer, gi, shape: jnp.stack([lay[gi].reshape(shape) for lay in per_layer])
    return (x[:tp].reshape(bp, lp, d), x[tp:].reshape(bs, ls, d),
            stack(new_conv, 0, (bp, CONV_WIDTH - 1, d_conv)), stack(new_conv, 1, (bs, CONV_WIDTH - 1, d_conv)),
            stack(new_ret, 0, (bp,) + state_ret.shape[2:]), stack(new_ret, 1, (bs,) + state_ret.shape[2:]),
            stack(new_k, 0, (bp, lp, kv_heads, hd)), stack(new_k, 1, (bs, ls, kv_heads, hd)),
            stack(new_v, 0, (bp, lp, kv_heads, hd)), stack(new_v, 1, (bs, ls, kv_heads, hd)),
            stack(new_kidx, 0, (bp, lp, IDX_DIM)), stack(new_kidx, 1, (bs, ls, IDX_DIM)))
```

```python
import functools
import math

import jax
import jax.numpy as jnp
from jax import lax
from jax.experimental import pallas as pl
from jax.experimental.pallas import tpu as pltpu

F32 = jnp.float32
BF16 = jnp.bfloat16

EPS = 1e-6
CHUNK = 64
ADA_BLOCK = 32
CONV_WIDTH = 31
RET_HEADS = 8
RET_DK = 128
ATT_HEADS = 16
ATT_KV_HEADS = 4
IDX_HEADS = 16
IDX_DIM = 64
TOPK_MAX = 256
N_GROUPS = 4
EXPERTS_PER_GROUP = 8
N_EXPERTS = N_GROUPS * EXPERTS_PER_GROUP
LANES = 128
VMEM_LIMIT = 56 * 1024 * 1024
BISECT_ITERS = 30
BISECT_ROUND = 5
MASKED_DIST = 1e30
LOG2E = math.log2(math.e)
EXPERT_TILE = 256
GATHER_CHUNK = 256
ROUTER_TILE = 512
COMBINE_TILE = 256


def _params(sem, vmem=VMEM_LIMIT):
    return pltpu.CompilerParams(dimension_semantics=sem, vmem_limit_bytes=vmem)


def _pick(n, cands):
    for c in cands:
        if n % c == 0:
            return c
    raise ValueError(f"no tile in {cands} divides {n}")


def _dot(a, b):
    return jnp.dot(a, b, preferred_element_type=F32)


def _dot_nt(a, b):
    return lax.dot_general(a, b, (((1,), (1,)), ((), ())), preferred_element_type=F32)


def _dot_tn(a, b):
    return lax.dot_general(a, b, (((0,), (0,)), ((), ())), preferred_element_type=F32)


def _silu(x):
    return x * jax.nn.sigmoid(x)


def _ada_kernel(c_ref, w_ref, b_ref, o_ref):
    c = _silu(c_ref[...]).astype(BF16)
    o_ref[...] = _dot(c, w_ref[...].astype(BF16)) + b_ref[...]


def _ada(c_all, w_ada, b_ada):
    depth, d, n = w_ada.shape
    rows = c_all.shape[0]
    tn = _pick(n, (1024, 512, 256, 128))
    return pl.pallas_call(
        _ada_kernel,
        grid=(depth, n // tn),
        in_specs=[pl.BlockSpec((rows, d), lambda l, j: (0, 0)),
                  pl.BlockSpec((None, d, tn), lambda l, j: (l, 0, j)),
                  pl.BlockSpec((None, 1, tn), lambda l, j: (l, 0, j))],
        out_specs=pl.BlockSpec((None, rows, tn), lambda l, j: (l, 0, j)),
        out_shape=jax.ShapeDtypeStruct((depth, rows, n), F32),
        compiler_params=_params(("parallel", "parallel")),
        name="ada",
    )(c_all, w_ada, b_ada.reshape(depth, 1, n))


def _modulate_rows(x_ref, shift_ref, scale_ref, g_ref, store):
    nblk = x_ref.shape[0] // ADA_BLOCK

    def body(r, carry):
        rows = pl.ds(pl.multiple_of(r * ADA_BLOCK, ADA_BLOCK), ADA_BLOCK)
        x = x_ref[rows, :]
        y = x * lax.rsqrt(jnp.mean(x * x, axis=-1, keepdims=True) + EPS) * g_ref[...]
        y = y * (1.0 + scale_ref[pl.ds(r, 1), :]) + shift_ref[pl.ds(r, 1), :]
        store(rows, y)
        return carry

    lax.fori_loop(0, nblk, body, 0)


def _modmm_kernel(has_extra, x_ref, shift_ref, scale_ref, g_ref, w_ref, *refs):
    if has_extra:
        wx_ref, o_ref, ox_ref, h_ref = refs
    else:
        o_ref, h_ref = refs

    @pl.when(pl.program_id(1) == 0)
    def _():
        def store(rows, y):
            h_ref[rows, :] = y.astype(BF16)
        _modulate_rows(x_ref, shift_ref, scale_ref, g_ref, store)
        if has_extra:
            ox_ref[...] = _dot(h_ref[...], wx_ref[...])

    o_ref[...] = _dot(h_ref[...], w_ref[...])


def _modulated_matmul(x, shift, scale, g, w, tm, name, w_extra=None):
    t, d = x.shape
    n = w.shape[1]
    tn = _pick(n, (1024, 512, 256, 128))
    nb = tm // ADA_BLOCK
    in_specs = [pl.BlockSpec((tm, d), lambda i, j: (i, 0)),
                pl.BlockSpec((nb, d), lambda i, j: (i, 0)),
                pl.BlockSpec((nb, d), lambda i, j: (i, 0)),
                pl.BlockSpec((1, d), lambda i, j: (0, 0)),
                pl.BlockSpec((d, tn), lambda i, j: (0, j))]
    out_specs = pl.BlockSpec((tm, tn), lambda i, j: (i, j))
    out_shape = jax.ShapeDtypeStruct((t, n), F32)
    args = [x, shift, scale, g.reshape(1, d), w]
    if w_extra is not None:
        nx = w_extra.shape[1]
        in_specs.append(pl.BlockSpec((d, nx), lambda i, j: (0, 0)))
        out_specs = [out_specs, pl.BlockSpec((tm, nx), lambda i, j: (i, 0))]
        out_shape = [out_shape, jax.ShapeDtypeStruct((t, nx), F32)]
        args.append(w_extra)
    return pl.pallas_call(
        functools.partial(_modmm_kernel, w_extra is not None),
        grid=(t // tm, n // tn),
        in_specs=in_specs,
        out_specs=out_specs,
        out_shape=out_shape,
        scratch_shapes=[pltpu.VMEM((tm, d), BF16)],
        compiler_params=_params(("parallel", "arbitrary")),
        name=name,
    )(*args)


def _outproj_kernel(n_in, tile_ranges, *refs):
    n_grp = len(tile_ranges)
    a_refs = [refs[p * n_grp:(p + 1) * n_grp] for p in range(n_in)]
    w_refs = refs[n_in * n_grp:n_in * n_grp + n_in]
    x_ref, gate_ref, o_ref, y_ref = refs[n_in * n_grp + n_in:]
    i = pl.program_id(0)
    for g, (lo, hi) in enumerate(tile_ranges):
        @pl.when(jnp.logical_and(i >= lo, i < hi))
        def _():
            y = _dot(a_refs[0][g][...], w_refs[0][...])
            for p in range(1, n_in):
                y = y + _dot(a_refs[p][g][...], w_refs[p][...])
            y_ref[...] = y
    nblk = x_ref.shape[0] // ADA_BLOCK

    def body(r, carry):
        rows = pl.ds(pl.multiple_of(r * ADA_BLOCK, ADA_BLOCK), ADA_BLOCK)
        o_ref[rows, :] = x_ref[rows, :] + gate_ref[pl.ds(r, 1), :] * y_ref[rows, :]
        return carry

    lax.fori_loop(0, nblk, body, 0)


def _outproj_residual(acts, ws, x, gate, tm, name):
    t, d = x.shape
    tn = _pick(d, (1024, 512, 256, 128))
    nb = tm // ADA_BLOCK
    n_in = len(acts)
    tile_ranges, lo = [], 0
    for a in acts[0]:
        tile_ranges.append((lo, lo + a.shape[0] // tm))
        lo = tile_ranges[-1][1]

    def group_spec(a, lo, hi):
        return pl.BlockSpec((tm, a.shape[1]), lambda i, j: (jnp.clip(i - lo, 0, hi - lo - 1), 0))

    in_specs = ([group_spec(a, *tile_ranges[g]) for piece in acts for g, a in enumerate(piece)]
                + [pl.BlockSpec((w.shape[0], tn), lambda i, j: (0, j)) for w in ws]
                + [pl.BlockSpec((tm, tn), lambda i, j: (i, j)),
                   pl.BlockSpec((nb, tn), lambda i, j: (i, j))])
    return pl.pallas_call(
        functools.partial(_outproj_kernel, n_in, tuple(tile_ranges)),
        grid=(t // tm, d // tn),
        in_specs=in_specs,
        out_specs=pl.BlockSpec((tm, tn), lambda i, j: (i, j)),
        out_shape=jax.ShapeDtypeStruct((t, d), F32),
        scratch_shapes=[pltpu.VMEM((tm, tn), F32)],
        compiler_params=_params(("parallel", "parallel")),
        name=name,
    )(*[a for piece in acts for a in piece], *ws, x, gate)


CONV_ROWS = 32
CONV_HIST = 32


def _conv_kernel(tl, val_ref, gate_ref, pval_ref, pgate_ref, buf_ref, w_ref, b_ref, ng_ref, nb_ref,
                 o_ref, nbuf_ref, up_ref, sh_ref):
    li = pl.program_id(1)
    hist = CONV_WIDTH - 1
    pad = CONV_HIST - hist
    glu = val_ref[...] * jax.nn.sigmoid(gate_ref[...])
    up_ref[CONV_HIST:CONV_HIST + tl, :] = glu
    up_ref[0:pad, :] = jnp.zeros((pad, up_ref.shape[1]), F32)

    @pl.when(li == 0)
    def _():
        up_ref[pad:CONV_HIST, :] = buf_ref[...]

    @pl.when(li > 0)
    def _():
        prev = pval_ref[...] * jax.nn.sigmoid(pgate_ref[...])
        up_ref[pad:CONV_HIST, :] = prev[pad:, :]

    n_sh = tl + CONV_HIST - 8
    for s in range(1, 8):
        sh_ref[s - 1, 0:n_sh, :] = up_ref[s:s + n_sh, :]

    def tap_rows(row):
        base, s = row - row % 8, row % 8
        src = up_ref if s == 0 else sh_ref.at[s - 1]
        return src[base:base + CONV_ROWS, :]

    for c in range(tl // CONV_ROWS):
        r0 = c * CONV_ROWS
        acc = jnp.zeros((CONV_ROWS, val_ref.shape[1]), F32)
        for j in range(CONV_WIDTH):
            acc = acc + tap_rows(r0 + pad + j) * w_ref[j:j + 1, :]
        acc = acc + b_ref[...]
        mu = jnp.mean(acc, axis=-1, keepdims=True)
        dlt = acc - mu
        y = dlt * lax.rsqrt(jnp.mean(dlt * dlt, axis=-1, keepdims=True) + EPS)
        y = y * ng_ref[...] + nb_ref[...]
        o_ref[r0:r0 + CONV_ROWS, :] = _silu(y).astype(o_ref.dtype)

    @pl.when(li == pl.num_programs(1) - 1)
    def _():
        nbuf_ref[...] = up_ref[CONV_HIST + tl - hist:CONV_HIST + tl, :]


def _conv_branch(u, row0, batch, seq, conv_buf, conv_w, conv_b, cn_g, cn_b, d_conv):
    tl = _pick(seq, (128, 64, 32))
    nl = seq // tl
    hist = CONV_WIDTH - 1
    rb = row0 // tl
    pb = tl // CONV_HIST
    cur = lambda col: pl.BlockSpec((tl, d_conv), lambda b, l: (rb + b * nl + l, col))
    prev = lambda col: pl.BlockSpec(
        (CONV_HIST, d_conv), lambda b, l: (jnp.maximum((rb + b * nl + l) * pb - 1, 0), col))
    vec = pl.BlockSpec((1, d_conv), lambda b, l: (0, 0))
    return pl.pallas_call(
        functools.partial(_conv_kernel, tl),
        grid=(batch, nl),
        in_specs=[cur(0), cur(1), prev(0), prev(1),
                  pl.BlockSpec((None, hist, d_conv), lambda b, l: (b, 0, 0)),
                  pl.BlockSpec((CONV_WIDTH, d_conv), lambda b, l: (0, 0)),
                  vec, vec, vec],
        out_specs=[pl.BlockSpec((tl, d_conv), lambda b, l: (b * nl + l, 0)),
                   pl.BlockSpec((None, hist, d_conv), lambda b, l: (b, 0, 0))],
        out_shape=[jax.ShapeDtypeStruct((batch * seq, d_conv), BF16),
                   jax.ShapeDtypeStruct((batch, hist, d_conv), F32)],
        scratch_shapes=[pltpu.VMEM((CONV_HIST + tl, d_conv), F32),
                        pltpu.VMEM((7, CONV_HIST + tl, d_conv), F32)],
        compiler_params=_params(("parallel", "arbitrary")),
        name="conv_branch",
    )(u, u, u, u, conv_buf, conv_w, conv_b.reshape(1, -1), cn_g.reshape(1, -1), cn_b.reshape(1, -1))


def _retention_kernel(q_ref, k_ref, v_ref, g_ref, s0_ref, din_ref, dq_ref, dk_ref, dblk_ref, rg_ref,
                      o_ref, s_out_ref, s_ref):
    ci = pl.program_id(1)

    @pl.when(ci == 0)
    def _():
        s_ref[...] = s0_ref[...]

    dv = s_ref.shape[2]
    for h in range(RET_HEADS):
        cols = slice(h * dv, (h + 1) * dv)
        q = q_ref[:, cols].astype(BF16)
        k = k_ref[:, cols] * (RET_DK ** -0.5)
        v = v_ref[:, cols].astype(BF16)
        s_prev = s_ref[h]
        sc = _dot_nt(q, k.astype(BF16)) * din_ref[h]
        o = _dot(sc.astype(BF16), v) + _dot(q, s_prev.astype(BF16)) * dq_ref[h]
        kd = (k * dk_ref[h]).astype(BF16)
        s_ref[h] = s_prev * dblk_ref[h] + _dot_tn(kd, v)
        o = o * lax.rsqrt(jnp.mean(o * o, axis=-1, keepdims=True) + EPS) * rg_ref[:, cols]
        o_ref[:, cols] = (o * _silu(g_ref[:, cols])).astype(o_ref.dtype)

    @pl.when(ci == pl.num_programs(1) - 1)
    def _():
        s_out_ref[...] = s_ref[...]


def _retention_branch(u, row0, batch, seq, state0, ret_g, col_q):
    heads, dk, dv = state0.shape[1:]
    d_ret = heads * dv
    c = _pick(seq, (256, 128, 64, 32))
    nc = seq // c
    rb = row0 // c
    lg = jnp.log1p(-(2.0 ** (-5.0 - jnp.arange(heads, dtype=F32))))
    pos = jnp.arange(c, dtype=F32)
    diff = pos[:, None] - pos[None, :]
    d_in = jnp.where(diff >= 0, jnp.exp(lg[:, None, None] * jnp.maximum(diff, 0.0)), 0.0)
    d_q = jnp.broadcast_to(jnp.exp(lg[:, None] * (pos[None, :] + 1.0))[:, :, None], (heads, c, dv))
    d_k = jnp.broadcast_to(jnp.exp(lg[:, None] * (c - 1.0 - pos[None, :]))[:, :, None], (heads, c, dk))
    d_blk = jnp.broadcast_to(jnp.exp(lg * c)[:, None, None], (heads, dk, dv))
    blk = lambda col: pl.BlockSpec((c, d_ret), lambda b, i: (rb + b * nc + i, col))
    const3 = lambda shape: pl.BlockSpec(shape, lambda b, i: (0, 0, 0))
    return pl.pallas_call(
        _retention_kernel,
        grid=(batch, nc),
        in_specs=[blk(col_q), blk(col_q + 1), blk(col_q + 2), blk(col_q + 3),
                  pl.BlockSpec((None, heads, dk, dv), lambda b, i: (b, 0, 0, 0)),
                  const3((heads, c, c)), const3((heads, c, dv)), const3((heads, c, dk)),
                  const3((heads, dk, dv)),
                  pl.BlockSpec((1, d_ret), lambda b, i: (0, 0))],
        out_specs=[pl.BlockSpec((c, d_ret), lambda b, i: (b * nc + i, 0)),
                   pl.BlockSpec((None, heads, dk, dv), lambda b, i: (b, 0, 0, 0))],
        out_shape=[jax.ShapeDtypeStruct((batch * seq, d_ret), BF16),
                   jax.ShapeDtypeStruct(state0.shape, F32)],
        scratch_shapes=[pltpu.VMEM((heads, dk, dv), F32)],
        compiler_params=_params(("parallel", "arbitrary")),
        name="retention_branch",
    )(u, u, u, u, state0, d_in, d_q, d_k, d_blk, ret_g.reshape(1, d_ret))


def _head_rms(x, g, hd):
    outs = []
    for h in range(x.shape[1] // hd):
        xh = x[:, h * hd:(h + 1) * hd]
        outs.append(xh * lax.rsqrt(jnp.mean(xh * xh, axis=-1, keepdims=True) + EPS) * g)
    return outs


def _qknorm_kernel(q_ref, k_ref, v_ref, qi_ref, ki_ref, qg_ref, kg_ref, kig_ref,
                   qo_ref, ko_ref, qio_ref, kio_ref, kb_ref, kib_ref, vt_ref):
    hd = qg_ref.shape[1]
    qio_ref[...] = qi_ref[...].astype(qio_ref.dtype)
    q_scale = hd ** -0.5 * LOG2E
    for h, qh in enumerate(_head_rms(q_ref[...], qg_ref[...], hd)):
        qo_ref[:, h * hd:(h + 1) * hd] = (qh * q_scale).astype(qo_ref.dtype)
    for h, kh in enumerate(_head_rms(k_ref[...], kg_ref[...], hd)):
        ko_ref[:, h * hd:(h + 1) * hd] = kh
        kb_ref[:, h * hd:(h + 1) * hd] = kh.astype(kb_ref.dtype)
    ki = ki_ref[:, :IDX_DIM]
    ki = ki * lax.rsqrt(jnp.mean(ki * ki, axis=-1, keepdims=True) + EPS) * kig_ref[...]
    kio_ref[...] = ki
    kib_ref[...] = ki.astype(kib_ref.dtype)
    blk = vt_ref.shape[2]
    for c in range(vt_ref.shape[0]):
        vt_ref[c] = v_ref[c * blk:(c + 1) * blk, :].T.astype(vt_ref.dtype)


def _qk_norms(u, ux, q_g, k_g, kidx_g, tm, d_q, d_kv):
    t = u.shape[0]
    hd = q_g.shape[0]
    d_qi = IDX_HEADS * IDX_DIM
    blk = min(tm, 256)
    return pl.pallas_call(
        _qknorm_kernel,
        grid=(t // tm,),
        in_specs=[pl.BlockSpec((tm, d_q), lambda i: (i, 0)),
                  pl.BlockSpec((tm, d_kv), lambda i: (i, d_q // d_kv)),
                  pl.BlockSpec((tm, d_kv), lambda i: (i, d_q // d_kv + 1)),
                  pl.BlockSpec((tm, d_qi), lambda i: (i, (d_q + 2 * d_kv) // d_qi)),
                  pl.BlockSpec((tm, LANES), lambda i: (i, 0)),
                  pl.BlockSpec((1, hd), lambda i: (0, 0)),
                  pl.BlockSpec((1, hd), lambda i: (0, 0)),
                  pl.BlockSpec((1, IDX_DIM), lambda i: (0, 0))],
        out_specs=[pl.BlockSpec((tm, d_q), lambda i: (i, 0)),
                   pl.BlockSpec((tm, d_kv), lambda i: (i, 0)),
                   pl.BlockSpec((tm, d_qi), lambda i: (i, 0)),
                   pl.BlockSpec((tm, IDX_DIM), lambda i: (i, 0)),
                   pl.BlockSpec((tm, d_kv), lambda i: (i, 0)),
                   pl.BlockSpec((tm, IDX_DIM), lambda i: (i, 0)),
                   pl.BlockSpec((tm // blk, d_kv, blk), lambda i: (i, 0, 0))],
        out_shape=[jax.ShapeDtypeStruct((t, d_q), BF16),
                   jax.ShapeDtypeStruct((t, d_kv), F32),
                   jax.ShapeDtypeStruct((t, d_qi), BF16),
                   jax.ShapeDtypeStruct((t, IDX_DIM), F32),
                   jax.ShapeDtypeStruct((t, d_kv), BF16),
                   jax.ShapeDtypeStruct((t, IDX_DIM), BF16),
                   jax.ShapeDtypeStruct((t // blk, d_kv, blk), BF16)],
        compiler_params=_params(("parallel",)),
        name="qk_norms",
    )(u, u, u, u, ux, q_g.reshape(1, hd), k_g.reshape(1, hd), kidx_g.reshape(1, IDX_DIM))


def _dsa_kernel(tq, rep, kb, past, n_keys, n_sel, q_ref, qi_ref, wt_ref, slope_ref, tri_ref,
                k_ref, vt_ref, ki_ref, o_ref, sc_ref, dist_ref, qg_ref, acc_ref, z_ref, rel_ref):
    t0 = pl.program_id(1) * tq
    hd = k_ref.shape[2] // ATT_KV_HEADS
    grp = ATT_HEADS // ATT_KV_HEADS
    wq = rep * tq
    n_adm_tile = jnp.minimum(((past + t0 + tq - 1) // CHUNK + 1) * CHUNK, n_keys)
    nkb = (n_adm_tile + kb - 1) // kb
    q_pos = past + t0 + lax.broadcasted_iota(jnp.int32, (1, wq), 1) % tq
    q_chunk = q_pos // CHUNK
    n_adm = jnp.minimum((q_chunk + 1) * CHUNK, n_keys).astype(F32)
    need = jnp.minimum(n_adm, float(n_sel))
    neg_inf = jnp.float32(-jnp.inf)
    w_t = wt_ref[...] * (IDX_HEADS ** -0.5 * IDX_DIM ** -0.5)

    def key_pos(i):
        return i * kb + lax.broadcasted_iota(jnp.int32, (kb, 1), 0)

    def admissible(i):
        kp = key_pos(i)
        return jnp.logical_and(kp // CHUNK <= q_chunk, kp < n_keys)

    def score_body(i, carry):
        lo, hi = carry
        ki = ki_ref[i]
        for h in range(IDX_HEADS):
            qi = jnp.concatenate([qi_ref[:, h * IDX_DIM:(h + 1) * IDX_DIM]] * rep, axis=0)
            rel_ref[h] = _dot_nt(ki, qi)
        acc = jnp.zeros((kb, wq), F32)
        for h in range(IDX_HEADS):
            acc = acc + jnp.maximum(rel_ref[h], 0.0) * w_t[h:h + 1, :]
        adm = admissible(i)
        sc_ref[i] = jnp.where(adm, acc, neg_inf)
        lo = jnp.minimum(lo, jnp.min(jnp.where(adm, acc, jnp.inf), axis=0, keepdims=True))
        hi = jnp.maximum(hi, jnp.max(jnp.where(adm, acc, neg_inf), axis=0, keepdims=True))
        return lo, hi

    lo, hi = lax.fori_loop(0, nkb, score_body,
                           (jnp.full((1, wq), jnp.inf, F32), jnp.full((1, wq), neg_inf, F32)))

    def count(pred):
        def body(i, acc):
            ones = jnp.where(pred(sc_ref[i]), 1.0, 0.0)
            parts = [ones[r:r + 8, :] for r in range(0, kb, 8)]
            while len(parts) > 1:
                parts = [parts[a] + parts[a + 1] for a in range(0, len(parts), 2)]
            return acc + parts[0]
        return jnp.sum(lax.fori_loop(0, nkb, body, jnp.zeros((8, wq), F32)), axis=0, keepdims=True)

    def bisect(_, carry):
        lo, hi, c_lo = carry
        mid = 0.5 * (lo + hi)
        c_mid = count(lambda s: s >= mid)
        ge = c_mid >= need
        return jnp.where(ge, mid, lo), jnp.where(ge, hi, mid), jnp.where(ge, c_mid, c_lo)

    def bisect_round(carry):
        rnd, lo, hi, c_lo = carry
        lo, hi, c_lo = lax.fori_loop(0, BISECT_ROUND, bisect, (lo, hi, c_lo))
        return rnd + 1, lo, hi, c_lo

    def unresolved(carry):
        rnd, _, _, c_lo = carry
        return jnp.logical_and(rnd < BISECT_ITERS // BISECT_ROUND, jnp.max(c_lo - need) > 0.0)

    _, lo, hi, c_lo = lax.while_loop(unresolved, bisect_round, (jnp.int32(0), lo, hi, n_adm))

    def write_dist(i, sel):
        dist = jnp.abs(q_pos - key_pos(i)).astype(F32)
        dist_ref[i] = jnp.where(sel, dist, MASKED_DIST)

    resolved = jnp.max(c_lo - need) <= 0.0

    @pl.when(resolved)
    def _():
        def body(i, carry):
            write_dist(i, sc_ref[i] >= lo)
            return carry
        lax.fori_loop(0, nkb, body, 0)

    @pl.when(jnp.logical_not(resolved))
    def _():
        n_above = count(lambda s: s > hi)
        room = need - n_above

        def body(i, seen):
            s = sc_ref[i]
            above = s > hi
            band = jnp.logical_and(s >= lo, jnp.logical_not(above))
            band_f = jnp.where(band, 1.0, 0.0)
            rank = _dot(tri_ref[...], band_f.astype(BF16)) + seen
            write_dist(i, jnp.logical_or(above, jnp.logical_and(band, rank <= room)))
            return seen + jnp.sum(band_f, axis=0, keepdims=True)
        lax.fori_loop(0, nkb, body, jnp.zeros((1, wq), F32))

    for g in range(ATT_KV_HEADS):
        qg_ref[g] = jnp.concatenate(
            [q_ref[:, (g * grp + r) * hd:(g * grp + r + 1) * hd] for r in range(grp)], axis=0)
    acc_ref[...] = jnp.zeros_like(acc_ref)

    def att_body(i, carry):
        ms, ls = carry
        dist = jnp.concatenate([dist_ref[i]] * (grp // rep), axis=1)
        new_ms, new_ls = [], []
        for g in range(ATT_KV_HEADS):
            z_ref[g] = _dot_nt(k_ref[i, :, g * hd:(g + 1) * hd], qg_ref[g])
        for g in range(ATT_KV_HEADS):
            z = z_ref[g] - slope_ref[g] * dist
            m_new = jnp.maximum(ms[g], jnp.max(z, axis=0, keepdims=True))
            alpha = jnp.exp2(ms[g] - m_new)
            p = jnp.exp2(z - m_new)
            new_ls.append(ls[g] * alpha + jnp.sum(p, axis=0, keepdims=True))
            new_ms.append(m_new)
            acc_ref[g] = acc_ref[g] * alpha + _dot(vt_ref[i, g * hd:(g + 1) * hd, :], p.astype(BF16))
        return tuple(new_ms), tuple(new_ls)

    _, ls = lax.fori_loop(
        0, nkb, att_body,
        (tuple(jnp.full((1, grp * tq), neg_inf, F32) for _ in range(ATT_KV_HEADS)),
         tuple(jnp.zeros((1, grp * tq), F32) for _ in range(ATT_KV_HEADS))))
    for g in range(ATT_KV_HEADS):
        out = (acc_ref[g] / ls[g]).T
        for r in range(grp):
            o_ref[:, (g * grp + r) * hd:(g * grp + r + 1) * hd] = out[r * tq:(r + 1) * tq, :].astype(o_ref.dtype)


def _sparse_attention(q, qidx, widx, k_blk, vt_blk, ki_blk, row0, batch, seq, past, n_keys, kb):
    d_q = q.shape[1]
    nkb_all = k_blk.shape[1]
    d_kv = k_blk.shape[3]
    tq = _pick(seq, (256, 128, 64, 32))
    nq = seq // tq
    rb = row0 // tq
    grp = ATT_HEADS // ATT_KV_HEADS
    rep = max(1, LANES // tq)
    wq = rep * tq
    n_sel = min(TOPK_MAX, n_keys // 4)
    widx_t = jnp.tile(jnp.swapaxes(widx.reshape(batch * nq, tq, IDX_HEADS), 1, 2), (1, 1, rep))
    slopes = LOG2E * 2.0 ** (-8.0 * jnp.arange(1, ATT_HEADS + 1, dtype=F32) / ATT_HEADS)
    slope_rows = jnp.repeat(slopes.reshape(ATT_KV_HEADS, grp), tq, axis=1).reshape(ATT_KV_HEADS, 1, grp * tq)
    tri = (jnp.arange(kb)[:, None] >= jnp.arange(kb)[None, :]).astype(BF16)
    return pl.pallas_call(
        functools.partial(_dsa_kernel, tq, rep, kb, past, n_keys, n_sel),
        grid=(batch, nq),
        in_specs=[pl.BlockSpec((tq, d_q), lambda b, i: (rb + b * nq + i, 0)),
                  pl.BlockSpec((tq, qidx.shape[1]), lambda b, i: (rb + b * nq + i, 0)),
                  pl.BlockSpec((None, IDX_HEADS, wq), lambda b, i: (b * nq + i, 0, 0)),
                  pl.BlockSpec((ATT_KV_HEADS, 1, grp * tq), lambda b, i: (0, 0, 0)),
                  pl.BlockSpec((kb, kb), lambda b, i: (0, 0)),
                  pl.BlockSpec((None, nkb_all, kb, d_kv), lambda b, i: (b, 0, 0, 0)),
                  pl.BlockSpec((None, nkb_all, d_kv, kb), lambda b, i: (b, 0, 0, 0)),
                  pl.BlockSpec((None, nkb_all, kb, IDX_DIM), lambda b, i: (b, 0, 0, 0))],
        out_specs=pl.BlockSpec((tq, d_q), lambda b, i: (b * nq + i, 0)),
        out_shape=jax.ShapeDtypeStruct((batch * seq, d_q), BF16),
        scratch_shapes=[pltpu.VMEM((nkb_all, kb, wq), F32), pltpu.VMEM((nkb_all, kb, wq), F32),
                        pltpu.VMEM((ATT_KV_HEADS, grp * tq, d_kv // ATT_KV_HEADS), BF16),
                        pltpu.VMEM((ATT_KV_HEADS, d_kv // ATT_KV_HEADS, grp * tq), F32),
                        pltpu.VMEM((ATT_KV_HEADS, kb, grp * tq), F32),
                        pltpu.VMEM((IDX_HEADS, kb, wq), F32)],
        compiler_params=_params(("parallel", "arbitrary")),
        name="sparse_attention",
    )(q, qidx, widx_t, slope_rows, tri, k_blk, vt_blk, ki_blk)


META_E, META_W, META_R = 0, 2, 4


def _router_kernel(x_ref, shift_ref, scale_ref, g_ref, whi_ref, wlo_ref, br_ref, tri_ref,
                   h_ref, meta_ref, cnt_ref, carry_ref):
    @pl.when(pl.program_id(0) == 0)
    def _():
        carry_ref[...] = jnp.zeros_like(carry_ref)

    def store(rows, y):
        h_ref[rows, :] = y
    _modulate_rows(x_ref, shift_ref, scale_ref, g_ref, store)

    h = h_ref[...]
    h_hi = h.astype(BF16)
    h_lo = (h - h_hi.astype(F32)).astype(BF16)
    logits = (_dot(h_hi, whi_ref[...]) + (_dot(h_hi, wlo_ref[...]) + _dot(h_lo, whi_ref[...]))) + br_ref[...]
    tm = logits.shape[0]
    lane = lax.broadcasted_iota(jnp.int32, (tm, LANES), 1).astype(F32)
    neg_inf = jnp.float32(-jnp.inf)

    def first_argmax(v):
        top = jnp.max(v, axis=-1, keepdims=True)
        return top, jnp.min(jnp.where(v == top, lane, float(LANES)), axis=-1, keepdims=True)

    is_group = lane < N_GROUPS
    gl = jnp.where(is_group, logits, neg_inf)
    g_top, g_sel = first_argmax(gl)
    g_w = 1.0 / jnp.sum(jnp.where(is_group, jnp.exp(gl - g_top), 0.0), axis=-1, keepdims=True)
    first = N_GROUPS + g_sel * EXPERTS_PER_GROUP
    el = jnp.where(jnp.logical_and(lane >= first, lane < first + EXPERTS_PER_GROUP), logits, neg_inf)
    v1, i1 = first_argmax(el)
    v2, i2 = first_argmax(jnp.where(lane == i1, neg_inf, el))
    e21 = jnp.exp(v2 - v1)
    w1 = g_w / (1.0 + e21)
    w2 = g_w * e21 / (1.0 + e21)
    e1 = i1 - N_GROUPS
    e2 = i2 - N_GROUPS

    oh1 = jnp.where(lane == e1, 1.0, 0.0)
    oh2 = jnp.where(lane == e2, 1.0, 0.0)
    both = oh1 + oh2
    before = _dot(tri_ref[...], both.astype(BF16)) + carry_ref[...]
    r1 = jnp.sum(before * oh1, axis=-1, keepdims=True)
    r2 = jnp.sum(before * oh2, axis=-1, keepdims=True)
    carry_ref[...] += jnp.sum(both, axis=0, keepdims=True)

    meta = jnp.zeros((tm, LANES), F32)
    for ln, val in ((META_E, e1), (META_E + 1, e2), (META_W, w1), (META_W + 1, w2),
                    (META_R, r1), (META_R + 1, r2)):
        meta = jnp.where(lane == ln, val, meta)
    meta_ref[...] = meta
    cnt_ref[...] = carry_ref[...]


def _router(x, shift, scale, g, w_router, b_router, tm):
    t, d = x.shape
    nb = tm // ADA_BLOCK
    tri = (jnp.arange(tm)[:, None] > jnp.arange(tm)[None, :]).astype(BF16)
    w_hi = w_router.astype(BF16)
    w_lo = (w_router - w_hi.astype(F32)).astype(BF16)
    return pl.pallas_call(
        _router_kernel,
        grid=(t // tm,),
        in_specs=[pl.BlockSpec((tm, d), lambda i: (i, 0)),
                  pl.BlockSpec((nb, d), lambda i: (i, 0)),
                  pl.BlockSpec((nb, d), lambda i: (i, 0)),
                  pl.BlockSpec((1, d), lambda i: (0, 0)),
                  pl.BlockSpec((d, LANES), lambda i: (0, 0)),
                  pl.BlockSpec((d, LANES), lambda i: (0, 0)),
                  pl.BlockSpec((1, LANES), lambda i: (0, 0)),
                  pl.BlockSpec((tm, tm), lambda i: (0, 0))],
        out_specs=[pl.BlockSpec((tm, d), lambda i: (i, 0)),
                   pl.BlockSpec((tm, LANES), lambda i: (i, 0)),
                   pl.BlockSpec((1, LANES), lambda i: (0, 0))],
        out_shape=[jax.ShapeDtypeStruct((t, d), F32),
                   jax.ShapeDtypeStruct((t, LANES), F32),
                   jax.ShapeDtypeStruct((1, LANES), F32)],
        scratch_shapes=[pltpu.VMEM((1, LANES), F32)],
        compiler_params=_params(("arbitrary",)),
        name="moe_router",
    )(x, shift, scale, g.reshape(1, d), w_hi, w_lo, b_router, tri)


def _row_copy(src_ref, dst_ref, sem, src_row, dst_row):
    return pltpu.make_async_copy(src_ref.at[pl.ds(src_row, 1)], dst_ref.at[pl.ds(dst_row, 1)], sem)


def _wait_rows(hbm_ref, sem, n):
    rows = hbm_ref.at[pl.ds(0, n)]
    pltpu.make_async_copy(rows, rows, sem).wait()


def _dispatch_kernel(pos_ref, h_ref, xs_ref, sem):
    n = h_ref.shape[0]

    def start(r, carry):
        for k in range(2):
            _row_copy(h_ref, xs_ref, sem, r, pos_ref[0, 0, 2 * r + k]).start(priority=k)
        return carry
    lax.fori_loop(0, n, start, 0)
    _wait_rows(xs_ref, sem, 2 * n)


def _dispatch(h, pos, ch):
    t, d = h.shape
    return pl.pallas_call(
        _dispatch_kernel,
        grid=(t // ch,),
        in_specs=[pl.BlockSpec((1, 1, 2 * ch), lambda i: (i, 0, 0), memory_space=pltpu.SMEM),
                  pl.BlockSpec((ch, d), lambda i: (i, 0))],
        out_specs=pl.BlockSpec(memory_space=pl.ANY),
        out_shape=jax.ShapeDtypeStruct((2 * t, d), F32),
        scratch_shapes=[pltpu.SemaphoreType.DMA(())],
        compiler_params=_params(("arbitrary",)),
        name="moe_dispatch",
    )(pos.reshape(t // ch, 1, 2 * ch), h)


def _cast_rows(src_ref, dst_ref, rows):
    def body(c, carry):
        r = pl.ds(pl.multiple_of(c * rows, rows), rows)
        dst_ref[r, :] = src_ref[r, :].astype(dst_ref.dtype)
        return carry
    lax.fori_loop(0, src_ref.shape[0] // rows, body, 0)


def _expert_kernel(vt_ref, ve_ref, vlo_ref, vhi_ref, x_ref, wg_ref, wu_ref, wd_ref, o_ref,
                   wgb_ref, wub_ref, wdb_ref):
    v = pl.program_id(0)
    lo, hi = vlo_ref[v], vhi_ref[v]
    prev = jnp.maximum(v - 1, 0)

    @pl.when(jnp.logical_and(hi > lo, jnp.logical_or(v == 0, ve_ref[prev] != ve_ref[v])))
    def _():
        _cast_rows(wg_ref, wgb_ref, 64)
        _cast_rows(wu_ref, wub_ref, 64)
        _cast_rows(wd_ref, wdb_ref, 16)

    @pl.when(hi > lo)
    def _():
        x = x_ref[...].astype(BF16)
        a = _dot(x, wgb_ref[...])
        b = _dot(x, wub_ref[...])
        y = _dot((_silu(a) * b).astype(BF16), wdb_ref[...])
        row = lax.broadcasted_iota(jnp.int32, (x.shape[0], 1), 0)
        mine = jnp.logical_and(row >= lo, row < hi)
        first = jnp.logical_or(v == 0, vt_ref[prev] != vt_ref[v])

        @pl.when(first)
        def _():
            o_ref[...] = jnp.where(mine, y, 0.0)

        @pl.when(jnp.logical_not(first))
        def _():
            o_ref[...] = jnp.where(mine, y, o_ref[...])


def _expert_mlp(xs, visits, layer, wg, wu, wd):
    p, d = xs.shape
    de = wg.shape[3]
    tm = EXPERT_TILE
    n_visits = visits[0].shape[0]
    return pl.pallas_call(
        _expert_kernel,
        grid_spec=pltpu.PrefetchScalarGridSpec(
            num_scalar_prefetch=4,
            grid=(n_visits,),
            in_specs=[pl.BlockSpec((tm, d), lambda v, vt, ve, lo, hi: (vt[v], 0)),
                      pl.BlockSpec((None, None, d, de), lambda v, vt, ve, lo, hi: (layer, ve[v], 0, 0)),
                      pl.BlockSpec((None, None, d, de), lambda v, vt, ve, lo, hi: (layer, ve[v], 0, 0)),
                      pl.BlockSpec((None, None, de, d), lambda v, vt, ve, lo, hi: (layer, ve[v], 0, 0))],
            out_specs=pl.BlockSpec((tm, d), lambda v, vt, ve, lo, hi: (vt[v], 0)),
            scratch_shapes=[pltpu.VMEM((d, de), BF16), pltpu.VMEM((d, de), BF16), pltpu.VMEM((de, d), BF16)]),
        out_shape=jax.ShapeDtypeStruct((p, d), F32),
        compiler_params=_params(("arbitrary",)),
        name="moe_experts",
    )(*visits, xs, wg, wu, wd)


def _combine_kernel(n_first, pos_ref, npos_ref, x_ref, gate_ref, meta_ref, ys_ref, *refs):
    o_refs, (ybuf, sems) = refs[:-2], refs[-2:]
    i = pl.program_id(0)
    n = pl.num_programs(0)
    tm = x_ref.shape[0]

    def fetch(p_ref, slot):
        def body(r, carry):
            for k in range(2):
                _row_copy(ys_ref, ybuf.at[slot, k], sems.at[slot], p_ref[0, 0, 2 * r + k], r).start(priority=k)
            return carry
        lax.fori_loop(0, tm, body, 0)

    @pl.when(i == 0)
    def _():
        fetch(pos_ref, 0)

    @pl.when(i + 1 < n)
    def _():
        fetch(npos_ref, (i + 1) % 2)

    slot = i % 2

    _wait_rows(ys_ref, sems.at[slot], 2 * tm)

    def write(o_ref):
        def body(r, carry):
            rows = pl.ds(pl.multiple_of(r * ADA_BLOCK, ADA_BLOCK), ADA_BLOCK)
            meta = meta_ref[rows, :]
            y = (meta[:, META_W:META_W + 1] * ybuf[slot, 0, rows, :]
                 + meta[:, META_W + 1:META_W + 2] * ybuf[slot, 1, rows, :])
            o_ref[rows, :] = x_ref[rows, :] + gate_ref[pl.ds(r, 1), :] * y
            return carry
        lax.fori_loop(0, tm // ADA_BLOCK, body, 0)

    if n_first is None:
        write(o_refs[0])
    else:
        pl.when(i < n_first)(lambda: write(o_refs[0]))
        pl.when(i >= n_first)(lambda: write(o_refs[1]))


def _combine(x, gate, meta, ys, pos, tm, split_rows=None):
    t, d = x.shape
    nb = tm // ADA_BLOCK
    n = t // tm
    pos3 = pos.reshape(n, 1, 2 * tm)
    if split_rows is None:
        n_first = None
        out_specs = pl.BlockSpec((tm, d), lambda i: (i, 0))
        out_shape = jax.ShapeDtypeStruct((t, d), F32)
    else:
        n_first = split_rows // tm
        out_specs = [pl.BlockSpec((tm, d), lambda i: (jnp.minimum(i, n_first - 1), 0)),
                     pl.BlockSpec((tm, d), lambda i: (jnp.maximum(i - n_first, 0), 0))]
        out_shape = [jax.ShapeDtypeStruct((split_rows, d), F32), jax.ShapeDtypeStruct((t - split_rows, d), F32)]
    return pl.pallas_call(
        functools.partial(_combine_kernel, n_first),
        grid=(n,),
        in_specs=[pl.BlockSpec((1, 1, 2 * tm), lambda i: (i, 0, 0), memory_space=pltpu.SMEM),
                  pl.BlockSpec((1, 1, 2 * tm), lambda i: (jnp.minimum(i + 1, n - 1), 0, 0), memory_space=pltpu.SMEM),
                  pl.BlockSpec((tm, d), lambda i: (i, 0)),
                  pl.BlockSpec((nb, d), lambda i: (i, 0)),
                  pl.BlockSpec((tm, LANES), lambda i: (i, 0)),
                  pl.BlockSpec(memory_space=pl.ANY)],
        out_specs=out_specs,
        out_shape=out_shape,
        scratch_shapes=[pltpu.VMEM((2, 2, tm, d), F32), pltpu.SemaphoreType.DMA((2,))],
        compiler_params=_params(("arbitrary",)),
        name="moe_combine",
    )(pos3, pos3, x, gate, meta, ys)


def _expert_visits(cnt, n_tiles):
    tile = EXPERT_TILE
    ends = jnp.cumsum(cnt)
    starts = ends - cnt
    first_tile = starts // tile
    n_vis = jnp.where(cnt > 0, (ends - 1) // tile - first_tile + 1, 0)
    vis_end = jnp.cumsum(n_vis)
    v = jnp.arange(n_tiles + N_EXPERTS - 1, dtype=jnp.int32)
    e = jnp.minimum(jnp.sum(vis_end[None, :] <= v[:, None], axis=1), N_EXPERTS - 1).astype(jnp.int32)
    real = v < vis_end[-1]
    tile_id = jnp.where(real, first_tile[e] + v - (vis_end - n_vis)[e], n_tiles - 1).astype(jnp.int32)
    lo = jnp.where(real, jnp.maximum(starts[e] - tile_id * tile, 0), 0).astype(jnp.int32)
    hi = jnp.where(real, jnp.minimum(ends[e] - tile_id * tile, tile), 0).astype(jnp.int32)
    return tile_id, e, lo, hi


def _hier_moe(x, shift, scale, gate, g, w_group, b_group, w_er, b_er, layer, wg, wu, wd, tm, split_rows=None):
    t, d = x.shape
    w_router = jnp.concatenate([w_group, jnp.moveaxis(w_er, 0, 1).reshape(d, N_EXPERTS)], axis=1)
    b_router = jnp.concatenate([b_group, b_er.reshape(N_EXPERTS)])
    n_route = N_GROUPS + N_EXPERTS
    w_router = jnp.pad(w_router, ((0, 0), (0, LANES - n_route)))
    b_router = jnp.pad(b_router, (0, LANES - n_route)).reshape(1, LANES)
    h, meta, counts = _router(x, shift, scale, g, w_router, b_router, min(tm, ROUTER_TILE))

    expert = meta[:, META_E:META_E + 2].astype(jnp.int32)
    rank = meta[:, META_R:META_R + 2].astype(jnp.int32)
    cnt = counts[0, :N_EXPERTS].astype(jnp.int32)
    starts = jnp.cumsum(cnt) - cnt
    pos = jnp.take(starts, expert, axis=0) + rank
    visits = _expert_visits(cnt, 2 * t // EXPERT_TILE)

    xs = _dispatch(h, pos, min(tm, GATHER_CHUNK))
    ys = _expert_mlp(xs, visits, layer, wg, wu, wd)
    return _combine(x, gate, meta, ys, pos, min(tm, COMBINE_TILE), split_rows)


def _cached_keys_kernel(past, seq, ck_ref, cv_ref, cki_ref, kn_ref, vn_ref, kin_ref,
                        k_ref, vt_ref, ki_ref, ks_ref, vs_ref, kis_ref):
    nkb, kb = k_ref.shape[0], k_ref.shape[1]
    n_keys = past + seq
    for cache_ref, new_ref, stage in ((ck_ref, kn_ref, ks_ref), (cv_ref, vn_ref, vs_ref), (cki_ref, kin_ref, kis_ref)):
        stage[0:past, :] = cache_ref[...]
        stage[past:n_keys, :] = new_ref[...]
        if nkb * kb > n_keys:
            stage[n_keys:, :] = jnp.zeros((nkb * kb - n_keys, stage.shape[1]), F32)
    for i in range(nkb):
        rows = slice(i * kb, (i + 1) * kb)
        k_ref[i] = ks_ref[rows, :].astype(k_ref.dtype)
        vt_ref[i] = vs_ref[rows, :].T.astype(vt_ref.dtype)
        ki_ref[i] = kis_ref[rows, :].astype(ki_ref.dtype)


def _cached_key_blocks(cache_k, cache_v, cache_ki, k_new, v_new, ki_new, kb):
    b, past, d_kv = cache_k.shape
    seq = k_new.shape[1]
    nkb = -(-(past + seq) // kb)
    per_stream = lambda a: pl.BlockSpec((None,) + a.shape[1:], lambda i: (i, 0, 0))
    blocked = lambda r, c: pl.BlockSpec((None, nkb, r, c), lambda i: (i, 0, 0, 0))
    args = (cache_k, cache_v, cache_ki, k_new, v_new, ki_new)
    return pl.pallas_call(
        functools.partial(_cached_keys_kernel, past, seq),
        grid=(b,),
        in_specs=[per_stream(a) for a in args],
        out_specs=[blocked(kb, d_kv), blocked(d_kv, kb), blocked(kb, IDX_DIM)],
        out_shape=[jax.ShapeDtypeStruct((b, nkb, kb, d_kv), BF16),
                   jax.ShapeDtypeStruct((b, nkb, d_kv, kb), BF16),
                   jax.ShapeDtypeStruct((b, nkb, kb, IDX_DIM), BF16)],
        scratch_shapes=[pltpu.VMEM((nkb * kb, d_kv), F32), pltpu.VMEM((nkb * kb, d_kv), F32),
                        pltpu.VMEM((nkb * kb, IDX_DIM), F32)],
        compiler_params=_params(("parallel",)),
        name="cached_key_blocks",
    )(*args)


def _key_blocks(k, vt_src, ki, n_pad, kb):
    b, s, _ = k.shape
    padk = lambda a: jnp.pad(a.astype(BF16), ((0, 0), (0, n_pad - s), (0, 0)))
    nkb = n_pad // kb
    k_blk = padk(k).reshape(b, nkb, kb, k.shape[2])
    vt_blk = jnp.swapaxes(padk(vt_src).reshape(b, nkb, kb, vt_src.shape[2]), 2, 3)
    ki_blk = padk(ki).reshape(b, nkb, kb, ki.shape[2])
    return k_blk, vt_blk, ki_blk


def kernel(x_prompt, x_sample, c_prompt, c_sample, cache_conv, state_ret, cache_k, cache_v, cache_kidx, norm_mix_g, norm_ffn_g, w_ada, b_ada, cr_w_in, conv_w, conv_b, conv_norm_g, conv_norm_b, ret_norm_g, cr_w_out, dsa_w_in, q_norm_g, k_norm_g, kidx_norm_g, dsa_w_out, moe_w_group, moe_b_group, moe_w_erouter, moe_b_erouter, moe_w_gate, moe_w_up, moe_w_down):
    bp, lp, d = x_prompt.shape
    bs, ls, _ = x_sample.shape
    tp, ts = bp * lp, bs * ls
    t = tp + ts
    depth = w_ada.shape[0]
    past = cache_k.shape[2]
    d_conv = conv_w.shape[2]
    d_ret = ret_norm_g.shape[1]
    d_q = dsa_w_out.shape[1]
    d_kv = cache_k.shape[3] * cache_k.shape[4]
    tm = _pick(math.gcd(tp, ts), (1024, 512, 256, 128))
    groups = ((0, bp, lp), (tp, bs, ls))

    x = jnp.concatenate([x_prompt.reshape(tp, d), x_sample.reshape(ts, d)], axis=0)

    c_all = jnp.concatenate([c_prompt, c_sample], axis=0)
    n_c = c_all.shape[0]
    c_all = jnp.pad(c_all, ((0, -n_c % 8), (0, 0)))
    ada = _ada(c_all, w_ada, b_ada)
    per_block = lambda a, n: jnp.broadcast_to(a[:, :, None, :], a.shape[:2] + (n, a.shape[2])).reshape(depth, -1, a.shape[2])
    ada_blk = jnp.concatenate([per_block(ada[:, :bp], lp // ADA_BLOCK),
                               per_block(ada[:, bp:bp + bs], ls // ADA_BLOCK)], axis=1)

    new_conv, new_ret, new_k, new_v, new_kidx = [], [], [], [], []
    for i in range(depth):
        sh1, sc1, g1, sh2, sc2, g2 = [ada_blk[i, :, m * d:(m + 1) * d] for m in range(6)]
        j = i // 2
        if i % 2 == 0:
            u = _modulated_matmul(x, sh1, sc1, norm_mix_g[i], cr_w_in[j].astype(BF16), tm, "cr_in_proj")
            a_out, b_out, bufs, states = [], [], [], []
            for gi, (row0, batch, seq) in enumerate(groups):
                buf0 = jnp.zeros((batch, CONV_WIDTH - 1, d_conv), F32) if gi == 0 else cache_conv[j]
                st0 = jnp.zeros((batch,) + state_ret.shape[2:], F32) if gi == 0 else state_ret[j]
                a, nbuf = _conv_branch(u, row0, batch, seq, buf0, conv_w[j], conv_b[j],
                                       conv_norm_g[j], conv_norm_b[j], d_conv)
                bo, nst = _retention_branch(u, row0, batch, seq, st0, ret_norm_g[j], 2 * d_conv // d_ret)
                a_out.append(a); b_out.append(bo); bufs.append(nbuf); states.append(nst)
            new_conv.append(bufs)
            new_ret.append(states)
            w_out = cr_w_out[j].astype(BF16)
            x = _outproj_residual([a_out, b_out], [w_out[:d_conv], w_out[d_conv:]], x, g1, tm, "cr_out_proj")
        else:
            w_in = dsa_w_in[j].astype(BF16)
            n_main = d_q + 2 * d_kv + IDX_HEADS * IDX_DIM
            w_x = jnp.pad(w_in[:, n_main:], ((0, 0), (0, LANES - (w_in.shape[1] - n_main))))
            u, ux = _modulated_matmul(x, sh1, sc1, norm_mix_g[i], w_in[:, :n_main], tm, "dsa_in_proj", w_extra=w_x)
            q, k, qidx, kidx, k_bf, ki_bf, vt_bf = _qk_norms(u, ux, q_norm_g[j], k_norm_g[j], kidx_norm_g[j],
                                                             tm, d_q, d_kv)
            v = u[:, d_q + d_kv:d_q + 2 * d_kv]
            widx = ux[:, IDX_DIM:IDX_DIM + IDX_HEADS]
            outs, ks, vs, kis = [], [], [], []
            for gi, (row0, batch, seq) in enumerate(groups):
                rows = slice(row0, row0 + batch * seq)
                kg = k[rows].reshape(batch, seq, d_kv)
                vg = v[rows].reshape(batch, seq, d_kv)
                kig = kidx[rows].reshape(batch, seq, IDX_DIM)
                ks.append(kg); vs.append(vg); kis.append(kig)
                g_past = 0 if gi == 0 else past
                n_keys = g_past + seq
                kb = 256 if n_keys >= 256 else 128
                n_pad = -(-n_keys // kb) * kb
                if g_past:
                    k_blk, vt_blk, ki_blk = _cached_key_blocks(
                        cache_k[j].reshape(batch, past, d_kv), cache_v[j].reshape(batch, past, d_kv),
                        cache_kidx[j], kg, vg, kig, kb)
                elif n_pad == n_keys and vt_bf.shape[2] == kb:
                    blocks = slice(row0 // kb, (row0 + batch * seq) // kb)
                    k_blk = k_bf[rows].reshape(batch, seq // kb, kb, d_kv)
                    vt_blk = vt_bf[blocks].reshape(batch, seq // kb, d_kv, kb)
                    ki_blk = ki_bf[rows].reshape(batch, seq // kb, kb, IDX_DIM)
                else:
                    k_blk, vt_blk, ki_blk = _key_blocks(kg, vg, kig, n_pad, kb)
                outs.append(_sparse_attention(q, qidx, widx[rows], k_blk, vt_blk, ki_blk,
                                              row0, batch, seq, g_past, n_keys, kb))
            new_k.append(ks); new_v.append(vs); new_kidx.append(kis)
            x = _outproj_residual([outs], [dsa_w_out[j].astype(BF16)], x, g1, tm, "dsa_out_proj")
        x = _hier_moe(x, sh2, sc2, g2, norm_ffn_g[i], moe_w_group[i], moe_b_group[i], moe_w_erouter[i],
                      moe_b_erouter[i], i, moe_w_gate, moe_w_up, moe_w_down, tm,
                      split_rows=tp if i == depth - 1 else None)

    y_p, y_s = x
    kv_heads, hd = cache_k.shape[3], cache_k.shape[4]
    stack = lambda per_layer, gi, shape: jnp.stack([lay[gi].reshape(shape) for lay in per_layer])
    return (y_p.reshape(bp, lp, d), y_s.reshape(bs, ls, d),
            stack(new_conv, 0, (bp, CONV_WIDTH - 1, d_conv)), stack(new_conv, 1, (bs, CONV_WIDTH - 1, d_conv)),
            stack(new_ret, 0, (bp,) + state_ret.shape[2:]), stack(new_ret, 1, (bs,) + state_ret.shape[2:]),
            stack(new_k, 0, (bp, lp, kv_heads, hd)), stack(new_k, 1, (bs, ls, kv_heads, hd)),
            stack(new_v, 0, (bp, lp, kv_heads, hd)), stack(new_v, 1, (bs, ls, kv_heads, hd)),
            stack(new_kidx, 0, (bp, lp, IDX_DIM)), stack(new_kidx, 1, (bs, ls, IDX_DIM)))
```

```python
import functools
import math

import jax
import jax.numpy as jnp
from jax import lax
from jax.experimental import pallas as pl
from jax.experimental.pallas import tpu as pltpu

F32 = jnp.float32
BF16 = jnp.bfloat16

EPS = 1e-6
CHUNK = 64
ADA_BLOCK = 32
CONV_WIDTH = 31
RET_HEADS = 8
RET_DK = 128
ATT_HEADS = 16
ATT_KV_HEADS = 4
IDX_HEADS = 16
IDX_DIM = 64
TOPK_MAX = 256
N_GROUPS = 4
EXPERTS_PER_GROUP = 8
N_EXPERTS = N_GROUPS * EXPERTS_PER_GROUP
LANES = 128
VMEM_LIMIT = 56 * 1024 * 1024
BISECT_ITERS = 30
BISECT_ROUND = 5
MASKED_DIST = 1e30
LOG2E = math.log2(math.e)
EXPERT_TILE = 256
GATHER_CHUNK = 256
ROUTER_TILE = 512
COMBINE_TILE = 256


def _params(sem, vmem=VMEM_LIMIT):
    return pltpu.CompilerParams(dimension_semantics=sem, vmem_limit_bytes=vmem)


def _pick(n, cands):
    for c in cands:
        if n % c == 0:
            return c
    raise ValueError(f"no tile in {cands} divides {n}")


def _dot(a, b):
    return jnp.dot(a, b, preferred_element_type=F32)


def _dot_nt(a, b):
    return lax.dot_general(a, b, (((1,), (1,)), ((), ())), preferred_element_type=F32)


def _dot_tn(a, b):
    return lax.dot_general(a, b, (((0,), (0,)), ((), ())), preferred_element_type=F32)


def _silu(x):
    return x * jax.nn.sigmoid(x)


def _ada_kernel(c_ref, w_ref, b_ref, o_ref):
    c = _silu(c_ref[...]).astype(BF16)
    o_ref[...] = _dot(c, w_ref[...].astype(BF16)) + b_ref[...]


def _ada(c_all, w_ada, b_ada):
    depth, d, n = w_ada.shape
    rows = c_all.shape[0]
    tn = _pick(n, (1024, 512, 256, 128))
    return pl.pallas_call(
        _ada_kernel,
        grid=(depth, n // tn),
        in_specs=[pl.BlockSpec((rows, d), lambda l, j: (0, 0)),
                  pl.BlockSpec((None, d, tn), lambda l, j: (l, 0, j)),
                  pl.BlockSpec((None, 1, tn), lambda l, j: (l, 0, j))],
        out_specs=pl.BlockSpec((None, rows, tn), lambda l, j: (l, 0, j)),
        out_shape=jax.ShapeDtypeStruct((depth, rows, n), F32),
        compiler_params=_params(("parallel", "parallel")),
        name="ada",
    )(c_all, w_ada, b_ada.reshape(depth, 1, n))


def _modulate_rows(x_ref, shift_ref, scale_ref, g_ref, store):
    nblk = x_ref.shape[0] // ADA_BLOCK

    def body(r, carry):
        rows = pl.ds(pl.multiple_of(r * ADA_BLOCK, ADA_BLOCK), ADA_BLOCK)
        x = x_ref[rows, :]
        y = x * lax.rsqrt(jnp.mean(x * x, axis=-1, keepdims=True) + EPS) * g_ref[...]
        y = y * (1.0 + scale_ref[pl.ds(r, 1), :]) + shift_ref[pl.ds(r, 1), :]
        store(rows, y)
        return carry

    lax.fori_loop(0, nblk, body, 0)


def _modmm_kernel(has_extra, x_ref, shift_ref, scale_ref, g_ref, w_ref, *refs):
    if has_extra:
        wx_ref, o_ref, ox_ref, h_ref = refs
    else:
        o_ref, h_ref = refs

    @pl.when(pl.program_id(1) == 0)
    def _():
        def store(rows, y):
            h_ref[rows, :] = y.astype(BF16)
        _modulate_rows(x_ref, shift_ref, scale_ref, g_ref, store)
        if has_extra:
            ox_ref[...] = _dot(h_ref[...], wx_ref[...])

    o_ref[...] = _dot(h_ref[...], w_ref[...])


def _modulated_matmul(x, shift, scale, g, w, tm, name, w_extra=None):
    t, d = x.shape
    n = w.shape[1]
    tn = _pick(n, (1024, 512, 256, 128))
    nb = tm // ADA_BLOCK
    in_specs = [pl.BlockSpec((tm, d), lambda i, j: (i, 0)),
                pl.BlockSpec((nb, d), lambda i, j: (i, 0)),
                pl.BlockSpec((nb, d), lambda i, j: (i, 0)),
                pl.BlockSpec((1, d), lambda i, j: (0, 0)),
                pl.BlockSpec((d, tn), lambda i, j: (0, j))]
    out_specs = pl.BlockSpec((tm, tn), lambda i, j: (i, j))
    out_shape = jax.ShapeDtypeStruct((t, n), F32)
    args = [x, shift, scale, g.reshape(1, d), w]
    if w_extra is not None:
        nx = w_extra.shape[1]
        in_specs.append(pl.BlockSpec((d, nx), lambda i, j: (0, 0)))
        out_specs = [out_specs, pl.BlockSpec((tm, nx), lambda i, j: (i, 0))]
        out_shape = [out_shape, jax.ShapeDtypeStruct((t, nx), F32)]
        args.append(w_extra)
    return pl.pallas_call(
        functools.partial(_modmm_kernel, w_extra is not None),
        grid=(t // tm, n // tn),
        in_specs=in_specs,
        out_specs=out_specs,
        out_shape=out_shape,
        scratch_shapes=[pltpu.VMEM((tm, d), BF16)],
        compiler_params=_params(("parallel", "arbitrary")),
        name=name,
    )(*args)


def _outproj_kernel(n_in, tile_ranges, *refs):
    n_grp = len(tile_ranges)
    a_refs = [refs[p * n_grp:(p + 1) * n_grp] for p in range(n_in)]
    w_refs = refs[n_in * n_grp:n_in * n_grp + n_in]
    x_ref, gate_ref, o_ref, y_ref = refs[n_in * n_grp + n_in:]
    i = pl.program_id(0)
    for g, (lo, hi) in enumerate(tile_ranges):
        @pl.when(jnp.logical_and(i >= lo, i < hi))
        def _():
            y = _dot(a_refs[0][g][...], w_refs[0][...])
            for p in range(1, n_in):
                y = y + _dot(a_refs[p][g][...], w_refs[p][...])
            y_ref[...] = y
    nblk = x_ref.shape[0] // ADA_BLOCK

    def body(r, carry):
        rows = pl.ds(pl.multiple_of(r * ADA_BLOCK, ADA_BLOCK), ADA_BLOCK)
        o_ref[rows, :] = x_ref[rows, :] + gate_ref[pl.ds(r, 1), :] * y_ref[rows, :]
        return carry

    lax.fori_loop(0, nblk, body, 0)


def _outproj_residual(acts, ws, x, gate, tm, name):
    t, d = x.shape
    tn = _pick(d, (1024, 512, 256, 128))
    nb = tm // ADA_BLOCK
    n_in = len(acts)
    tile_ranges, lo = [], 0
    for a in acts[0]:
        tile_ranges.append((lo, lo + a.shape[0] // tm))
        lo = tile_ranges[-1][1]

    def group_spec(a, lo, hi):
        return pl.BlockSpec((tm, a.shape[1]), lambda i, j: (jnp.clip(i - lo, 0, hi - lo - 1), 0))

    in_specs = ([group_spec(a, *tile_ranges[g]) for piece in acts for g, a in enumerate(piece)]
                + [pl.BlockSpec((w.shape[0], tn), lambda i, j: (0, j)) for w in ws]
                + [pl.BlockSpec((tm, tn), lambda i, j: (i, j)),
                   pl.BlockSpec((nb, tn), lambda i, j: (i, j))])
    return pl.pallas_call(
        functools.partial(_outproj_kernel, n_in, tuple(tile_ranges)),
        grid=(t // tm, d // tn),
        in_specs=in_specs,
        out_specs=pl.BlockSpec((tm, tn), lambda i, j: (i, j)),
        out_shape=jax.ShapeDtypeStruct((t, d), F32),
        scratch_shapes=[pltpu.VMEM((tm, tn), F32)],
        compiler_params=_params(("parallel", "parallel")),
        name=name,
    )(*[a for piece in acts for a in piece], *ws, x, gate)


CONV_ROWS = 32
CONV_HIST = 32


def _conv_kernel(tl, val_ref, gate_ref, pval_ref, pgate_ref, buf_ref, w_ref, b_ref, ng_ref, nb_ref,
                 o_ref, nbuf_ref, up_ref, sh_ref):
    li = pl.program_id(1)
    hist = CONV_WIDTH - 1
    pad = CONV_HIST - hist
    glu = val_ref[...] * jax.nn.sigmoid(gate_ref[...])
    up_ref[CONV_HIST:CONV_HIST + tl, :] = glu
    up_ref[0:pad, :] = jnp.zeros((pad, up_ref.shape[1]), F32)

    @pl.when(li == 0)
    def _():
        up_ref[pad:CONV_HIST, :] = buf_ref[...]

    @pl.when(li > 0)
    def _():
        prev = pval_ref[...] * jax.nn.sigmoid(pgate_ref[...])
        up_ref[pad:CONV_HIST, :] = prev[pad:, :]

    n_sh = tl + CONV_HIST - 8
    for s in range(1, 8):
        sh_ref[s - 1, 0:n_sh, :] = up_ref[s:s + n_sh, :]

    def tap_rows(row):
        base, s = row - row % 8, row % 8
        src = up_ref if s == 0 else sh_ref.at[s - 1]
        return src[base:base + CONV_ROWS, :]

    for c in range(tl // CONV_ROWS):
        r0 = c * CONV_ROWS
        acc = jnp.zeros((CONV_ROWS, val_ref.shape[1]), F32)
        for j in range(CONV_WIDTH):
            acc = acc + tap_rows(r0 + pad + j) * w_ref[j:j + 1, :]
        acc = acc + b_ref[...]
        mu = jnp.mean(acc, axis=-1, keepdims=True)
        dlt = acc - mu
        y = dlt * lax.rsqrt(jnp.mean(dlt * dlt, axis=-1, keepdims=True) + EPS)
        y = y * ng_ref[...] + nb_ref[...]
        o_ref[r0:r0 + CONV_ROWS, :] = _silu(y).astype(o_ref.dtype)

    @pl.when(li == pl.num_programs(1) - 1)
    def _():
        nbuf_ref[...] = up_ref[CONV_HIST + tl - hist:CONV_HIST + tl, :]


def _conv_branch(u, row0, batch, seq, conv_buf, conv_w, conv_b, cn_g, cn_b, d_conv):
    tl = _pick(seq, (128, 64, 32))
    nl = seq // tl
    hist = CONV_WIDTH - 1
    rb = row0 // tl
    pb = tl // CONV_HIST
    cur = lambda col: pl.BlockSpec((tl, d_conv), lambda b, l: (rb + b * nl + l, col))
    prev = lambda col: pl.BlockSpec(
        (CONV_HIST, d_conv), lambda b, l: (jnp.maximum((rb + b * nl + l) * pb - 1, 0), col))
    vec = pl.BlockSpec((1, d_conv), lambda b, l: (0, 0))
    return pl.pallas_call(
        functools.partial(_conv_kernel, tl),
        grid=(batch, nl),
        in_specs=[cur(0), cur(1), prev(0), prev(1),
                  pl.BlockSpec((None, hist, d_conv), lambda b, l: (b, 0, 0)),
                  pl.BlockSpec((CONV_WIDTH, d_conv), lambda b, l: (0, 0)),
                  vec, vec, vec],
        out_specs=[pl.BlockSpec((tl, d_conv), lambda b, l: (b * nl + l, 0)),
                   pl.BlockSpec((None, hist, d_conv), lambda b, l: (b, 0, 0))],
        out_shape=[jax.ShapeDtypeStruct((batch * seq, d_conv), BF16),
                   jax.ShapeDtypeStruct((batch, hist, d_conv), F32)],
        scratch_shapes=[pltpu.VMEM((CONV_HIST + tl, d_conv), F32),
                        pltpu.VMEM((7, CONV_HIST + tl, d_conv), F32)],
        compiler_params=_params(("parallel", "arbitrary")),
        name="conv_branch",
    )(u, u, u, u, conv_buf, conv_w, conv_b.reshape(1, -1), cn_g.reshape(1, -1), cn_b.reshape(1, -1))


def _retention_kernel(q_ref, k_ref, v_ref, g_ref, s0_ref, din_ref, dq_ref, dk_ref, dblk_ref, rg_ref,
                      o_ref, s_out_ref, s_ref):
    ci = pl.program_id(1)

    @pl.when(ci == 0)
    def _():
        s_ref[...] = s0_ref[...]

    dv = s_ref.shape[2]
    for h in range(RET_HEADS):
        cols = slice(h * dv, (h + 1) * dv)
        q = q_ref[:, cols].astype(BF16)
        k = k_ref[:, cols] * (RET_DK ** -0.5)
        v = v_ref[:, cols].astype(BF16)
        s_prev = s_ref[h]
        sc = _dot_nt(q, k.astype(BF16)) * din_ref[h]
        o = _dot(sc.astype(BF16), v) + _dot(q, s_prev.astype(BF16)) * dq_ref[h]
        kd = (k * dk_ref[h]).astype(BF16)
        s_ref[h] = s_prev * dblk_ref[h] + _dot_tn(kd, v)
        o = o * lax.rsqrt(jnp.mean(o * o, axis=-1, keepdims=True) + EPS) * rg_ref[:, cols]
        o_ref[:, cols] = (o * _silu(g_ref[:, cols])).astype(o_ref.dtype)

    @pl.when(ci == pl.num_programs(1) - 1)
    def _():
        s_out_ref[...] = s_ref[...]


def _retention_branch(u, row0, batch, seq, state0, ret_g, col_q):
    heads, dk, dv = state0.shape[1:]
    d_ret = heads * dv
    c = _pick(seq, (256, 128, 64, 32))
    nc = seq // c
    rb = row0 // c
    lg = jnp.log1p(-(2.0 ** (-5.0 - jnp.arange(heads, dtype=F32))))
    pos = jnp.arange(c, dtype=F32)
    diff = pos[:, None] - pos[None, :]
    d_in = jnp.where(diff >= 0, jnp.exp(lg[:, None, None] * jnp.maximum(diff, 0.0)), 0.0)
    d_q = jnp.broadcast_to(jnp.exp(lg[:, None] * (pos[None, :] + 1.0))[:, :, None], (heads, c, dv))
    d_k = jnp.broadcast_to(jnp.exp(lg[:, None] * (c - 1.0 - pos[None, :]))[:, :, None], (heads, c, dk))
    d_blk = jnp.broadcast_to(jnp.exp(lg * c)[:, None, None], (heads, dk, dv))
    blk = lambda col: pl.BlockSpec((c, d_ret), lambda b, i: (rb + b * nc + i, col))
    const3 = lambda shape: pl.BlockSpec(shape, lambda b, i: (0, 0, 0))
    return pl.pallas_call(
        _retention_kernel,
        grid=(batch, nc),
        in_specs=[blk(col_q), blk(col_q + 1), blk(col_q + 2), blk(col_q + 3),
                  pl.BlockSpec((None, heads, dk, dv), lambda b, i: (b, 0, 0, 0)),
                  const3((heads, c, c)), const3((heads, c, dv)), const3((heads, c, dk)),
                  const3((heads, dk, dv)),
                  pl.BlockSpec((1, d_ret), lambda b, i: (0, 0))],
        out_specs=[pl.BlockSpec((c, d_ret), lambda b, i: (b * nc + i, 0)),
                   pl.BlockSpec((None, heads, dk, dv), lambda b, i: (b, 0, 0, 0))],
        out_shape=[jax.ShapeDtypeStruct((batch * seq, d_ret), BF16),
                   jax.ShapeDtypeStruct(state0.shape, F32)],
        scratch_shapes=[pltpu.VMEM((heads, dk, dv), F32)],
        compiler_params=_params(("parallel", "arbitrary")),
        name="retention_branch",
    )(u, u, u, u, state0, d_in, d_q, d_k, d_blk, ret_g.reshape(1, d_ret))


def _head_rms(x, g, hd):
    outs = []
    for h in range(x.shape[1] // hd):
        xh = x[:, h * hd:(h + 1) * hd]
        outs.append(xh * lax.rsqrt(jnp.mean(xh * xh, axis=-1, keepdims=True) + EPS) * g)
    return outs


def _qknorm_kernel(q_ref, k_ref, v_ref, qi_ref, ki_ref, qg_ref, kg_ref, kig_ref,
                   qo_ref, ko_ref, qio_ref, kio_ref, kb_ref, kib_ref, vt_ref):
    hd = qg_ref.shape[1]
    qio_ref[...] = qi_ref[...].astype(qio_ref.dtype)
    q_scale = hd ** -0.5 * LOG2E
    for h, qh in enumerate(_head_rms(q_ref[...], qg_ref[...], hd)):
        qo_ref[:, h * hd:(h + 1) * hd] = (qh * q_scale).astype(qo_ref.dtype)
    for h, kh in enumerate(_head_rms(k_ref[...], kg_ref[...], hd)):
        ko_ref[:, h * hd:(h + 1) * hd] = kh
        kb_ref[:, h * hd:(h + 1) * hd] = kh.astype(kb_ref.dtype)
    ki = ki_ref[:, :IDX_DIM]
    ki = ki * lax.rsqrt(jnp.mean(ki * ki, axis=-1, keepdims=True) + EPS) * kig_ref[...]
    kio_ref[...] = ki
    kib_ref[...] = ki.astype(kib_ref.dtype)
    blk = vt_ref.shape[2]
    for c in range(vt_ref.shape[0]):
        vt_ref[c] = v_ref[c * blk:(c + 1) * blk, :].T.astype(vt_ref.dtype)


def _qk_norms(u, ux, q_g, k_g, kidx_g, tm, d_q, d_kv):
    t = u.shape[0]
    hd = q_g.shape[0]
    d_qi = IDX_HEADS * IDX_DIM
    blk = min(tm, 256)
    return pl.pallas_call(
        _qknorm_kernel,
        grid=(t // tm,),
        in_specs=[pl.BlockSpec((tm, d_q), lambda i: (i, 0)),
                  pl.BlockSpec((tm, d_kv), lambda i: (i, d_q // d_kv)),
                  pl.BlockSpec((tm, d_kv), lambda i: (i, d_q // d_kv + 1)),
                  pl.BlockSpec((tm, d_qi), lambda i: (i, (d_q + 2 * d_kv) // d_qi)),
                  pl.BlockSpec((tm, LANES), lambda i: (i, 0)),
                  pl.BlockSpec((1, hd), lambda i: (0, 0)),
                  pl.BlockSpec((1, hd), lambda i: (0, 0)),
                  pl.BlockSpec((1, IDX_DIM), lambda i: (0, 0))],
        out_specs=[pl.BlockSpec((tm, d_q), lambda i: (i, 0)),
                   pl.BlockSpec((tm, d_kv), lambda i: (i, 0)),
                   pl.BlockSpec((tm, d_qi), lambda i: (i, 0)),
                   pl.BlockSpec((tm, IDX_DIM), lambda i: (i, 0)),
                   pl.BlockSpec((tm, d_kv), lambda i: (i, 0)),
                   pl.BlockSpec((tm, IDX_DIM), lambda i: (i, 0)),
                   pl.BlockSpec((tm // blk, d_kv, blk), lambda i: (i, 0, 0))],
        out_shape=[jax.ShapeDtypeStruct((t, d_q), BF16),
                   jax.ShapeDtypeStruct((t, d_kv), F32),
                   jax.ShapeDtypeStruct((t, d_qi), BF16),
                   jax.ShapeDtypeStruct((t, IDX_DIM), F32),
                   jax.ShapeDtypeStruct((t, d_kv), BF16),
                   jax.ShapeDtypeStruct((t, IDX_DIM), BF16),
                   jax.ShapeDtypeStruct((t // blk, d_kv, blk), BF16)],
        compiler_params=_params(("parallel",)),
        name="qk_norms",
    )(u, u, u, u, ux, q_g.reshape(1, hd), k_g.reshape(1, hd), kidx_g.reshape(1, IDX_DIM))


def _dsa_kernel(tq, rep, kb, past, n_keys, n_sel, q_ref, qi_ref, wt_ref, slope_ref, tri_ref,
                k_ref, vt_ref, ki_ref, o_ref, sc_ref, dist_ref, qg_ref, acc_ref, z_ref, rel_ref):
    t0 = pl.program_id(1) * tq
    hd = k_ref.shape[2] // ATT_KV_HEADS
    grp = ATT_HEADS // ATT_KV_HEADS
    wq = rep * tq
    n_adm_tile = jnp.minimum(((past + t0 + tq - 1) // CHUNK + 1) * CHUNK, n_keys)
    nkb = (n_adm_tile + kb - 1) // kb
    q_pos = past + t0 + lax.broadcasted_iota(jnp.int32, (1, wq), 1) % tq
    q_chunk = q_pos // CHUNK
    n_adm = jnp.minimum((q_chunk + 1) * CHUNK, n_keys).astype(F32)
    need = jnp.minimum(n_adm, float(n_sel))
    neg_inf = jnp.float32(-jnp.inf)
    w_t = wt_ref[...] * (IDX_HEADS ** -0.5 * IDX_DIM ** -0.5)

    def key_pos(i):
        return i * kb + lax.broadcasted_iota(jnp.int32, (kb, 1), 0)

    def admissible(i):
        kp = key_pos(i)
        return jnp.logical_and(kp // CHUNK <= q_chunk, kp < n_keys)

    def score_body(i, carry):
        lo, hi = carry
        ki = ki_ref[i]
        for h in range(IDX_HEADS):
            qi = jnp.concatenate([qi_ref[:, h * IDX_DIM:(h + 1) * IDX_DIM]] * rep, axis=0)
            rel_ref[h] = _dot_nt(ki, qi)
        acc = jnp.zeros((kb, wq), F32)
        for h in range(IDX_HEADS):
            acc = acc + jnp.maximum(rel_ref[h], 0.0) * w_t[h:h + 1, :]
        adm = admissible(i)
        sc_ref[i] = jnp.where(adm, acc, neg_inf)
        lo = jnp.minimum(lo, jnp.min(jnp.where(adm, acc, jnp.inf), axis=0, keepdims=True))
        hi = jnp.maximum(hi, jnp.max(jnp.where(adm, acc, neg_inf), axis=0, keepdims=True))
        return lo, hi

    lo, hi = lax.fori_loop(0, nkb, score_body,
                           (jnp.full((1, wq), jnp.inf, F32), jnp.full((1, wq), neg_inf, F32)))

    def count(pred):
        def body(i, acc):
            ones = jnp.where(pred(sc_ref[i]), 1.0, 0.0)
            parts = [ones[r:r + 8, :] for r in range(0, kb, 8)]
            while len(parts) > 1:
                parts = [parts[a] + parts[a + 1] for a in range(0, len(parts), 2)]
            return acc + parts[0]
        return jnp.sum(lax.fori_loop(0, nkb, body, jnp.zeros((8, wq), F32)), axis=0, keepdims=True)

    def bisect(_, carry):
        lo, hi, c_lo = carry
        mid = 0.5 * (lo + hi)
        c_mid = count(lambda s: s >= mid)
        ge = c_mid >= need
        return jnp.where(ge, mid, lo), jnp.where(ge, hi, mid), jnp.where(ge, c_mid, c_lo)

    def bisect_round(carry):
        rnd, lo, hi, c_lo = carry
        lo, hi, c_lo = lax.fori_loop(0, BISECT_ROUND, bisect, (lo, hi, c_lo))
        return rnd + 1, lo, hi, c_lo

    def unresolved(carry):
        rnd, _, _, c_lo = carry
        return jnp.logical_and(rnd < BISECT_ITERS // BISECT_ROUND, jnp.max(c_lo - need) > 0.0)

    _, lo, hi, c_lo = lax.while_loop(unresolved, bisect_round, (jnp.int32(0), lo, hi, n_adm))

    def write_dist(i, sel):
        dist = jnp.abs(q_pos - key_pos(i)).astype(F32)
        dist_ref[i] = jnp.where(sel, dist, MASKED_DIST)

    resolved = jnp.max(c_lo - need) <= 0.0

    @pl.when(resolved)
    def _():
        def body(i, carry):
            write_dist(i, sc_ref[i] >= lo)
            return carry
        lax.fori_loop(0, nkb, body, 0)

    @pl.when(jnp.logical_not(resolved))
    def _():
        n_above = count(lambda s: s > hi)
        room = need - n_above

        def body(i, seen):
            s = sc_ref[i]
            above = s > hi
            band = jnp.logical_and(s >= lo, jnp.logical_not(above))
            band_f = jnp.where(band, 1.0, 0.0)
            rank = _dot(tri_ref[...], band_f.astype(BF16)) + seen
            write_dist(i, jnp.logical_or(above, jnp.logical_and(band, rank <= room)))
            return seen + jnp.sum(band_f, axis=0, keepdims=True)
        lax.fori_loop(0, nkb, body, jnp.zeros((1, wq), F32))

    for g in range(ATT_KV_HEADS):
        qg_ref[g] = jnp.concatenate(
            [q_ref[:, (g * grp + r) * hd:(g * grp + r + 1) * hd] for r in range(grp)], axis=0)
    acc_ref[...] = jnp.zeros_like(acc_ref)

    def att_body(i, carry):
        ms, ls = carry
        dist = jnp.concatenate([dist_ref[i]] * (grp // rep), axis=1)
        new_ms, new_ls = [], []
        for g in range(ATT_KV_HEADS):
            z_ref[g] = _dot_nt(k_ref[i, :, g * hd:(g + 1) * hd], qg_ref[g])
        for g in range(ATT_KV_HEADS):
            z = z_ref[g] - slope_ref[g] * dist
            m_new = jnp.maximum(ms[g], jnp.max(z, axis=0, keepdims=True))
            alpha = jnp.exp2(ms[g] - m_new)
            p = jnp.exp2(z - m_new)
            new_ls.append(ls[g] * alpha + jnp.sum(p, axis=0, keepdims=True))
            new_ms.append(m_new)
            acc_ref[g] = acc_ref[g] * alpha + _dot(vt_ref[i, g * hd:(g + 1) * hd, :], p.astype(BF16))
        return tuple(new_ms), tuple(new_ls)

    _, ls = lax.fori_loop(
        0, nkb, att_body,
        (tuple(jnp.full((1, grp * tq), neg_inf, F32) for _ in range(ATT_KV_HEADS)),
         tuple(jnp.zeros((1, grp * tq), F32) for _ in range(ATT_KV_HEADS))))
    for g in range(ATT_KV_HEADS):
        out = (acc_ref[g] / ls[g]).T
        for r in range(grp):
            o_ref[:, (g * grp + r) * hd:(g * grp + r + 1) * hd] = out[r * tq:(r + 1) * tq, :].astype(o_ref.dtype)


def _sparse_attention(q, qidx, widx, k_blk, vt_blk, ki_blk, row0, batch, seq, past, n_keys, kb):
    d_q = q.shape[1]
    nkb_all = k_blk.shape[1]
    d_kv = k_blk.shape[3]
    tq = _pick(seq, (256, 128, 64, 32))
    nq = seq // tq
    rb = row0 // tq
    grp = ATT_HEADS // ATT_KV_HEADS
    rep = max(1, LANES // tq)
    wq = rep * tq
    n_sel = min(TOPK_MAX, n_keys // 4)
    widx_t = jnp.tile(jnp.swapaxes(widx.reshape(batch * nq, tq, IDX_HEADS), 1, 2), (1, 1, rep))
    slopes = LOG2E * 2.0 ** (-8.0 * jnp.arange(1, ATT_HEADS + 1, dtype=F32) / ATT_HEADS)
    slope_rows = jnp.repeat(slopes.reshape(ATT_KV_HEADS, grp), tq, axis=1).reshape(ATT_KV_HEADS, 1, grp * tq)
    tri = (jnp.arange(kb)[:, None] >= jnp.arange(kb)[None, :]).astype(BF16)
    return pl.pallas_call(
        functools.partial(_dsa_kernel, tq, rep, kb, past, n_keys, n_sel),
        grid=(batch, nq),
        in_specs=[pl.BlockSpec((tq, d_q), lambda b, i: (rb + b * nq + i, 0)),
                  pl.BlockSpec((tq, qidx.shape[1]), lambda b, i: (rb + b * nq + i, 0)),
                  pl.BlockSpec((None, IDX_HEADS, wq), lambda b, i: (b * nq + i, 0, 0)),
                  pl.BlockSpec((ATT_KV_HEADS, 1, grp * tq), lambda b, i: (0, 0, 0)),
                  pl.BlockSpec((kb, kb), lambda b, i: (0, 0)),
                  pl.BlockSpec((None, nkb_all, kb, d_kv), lambda b, i: (b, 0, 0, 0)),
                  pl.BlockSpec((None, nkb_all, d_kv, kb), lambda b, i: (b, 0, 0, 0)),
                  pl.BlockSpec((None, nkb_all, kb, IDX_DIM), lambda b, i: (b, 0, 0, 0))],
        out_specs=pl.BlockSpec((tq, d_q), lambda b, i: (b * nq + i, 0)),
        out_shape=jax.ShapeDtypeStruct((batch * seq, d_q), BF16),
        scratch_shapes=[pltpu.VMEM((nkb_all, kb, wq), F32), pltpu.VMEM((nkb_all, kb, wq), F32),
                        pltpu.VMEM((ATT_KV_HEADS, grp * tq, d_kv // ATT_KV_HEADS), BF16),
                        pltpu.VMEM((ATT_KV_HEADS, d_kv // ATT_KV_HEADS, grp * tq), F32),
                        pltpu.VMEM((ATT_KV_HEADS, kb, grp * tq), F32),
                        pltpu.VMEM((IDX_HEADS, kb, wq), F32)],
        compiler_params=_params(("parallel", "arbitrary")),
        name="sparse_attention",
    )(q, qidx, widx_t, slope_rows, tri, k_blk, vt_blk, ki_blk)


META_E, META_W, META_R = 0, 2, 4


def _router_kernel(x_ref, shift_ref, scale_ref, g_ref, whi_ref, wlo_ref, br_ref, tri_ref,
                   h_ref, meta_ref, cnt_ref, carry_ref):
    @pl.when(pl.program_id(0) == 0)
    def _():
        carry_ref[...] = jnp.zeros_like(carry_ref)

    def store(rows, y):
        h_ref[rows, :] = y
    _modulate_rows(x_ref, shift_ref, scale_ref, g_ref, store)

    h = h_ref[...]
    h_hi = h.astype(BF16)
    h_lo = (h - h_hi.astype(F32)).astype(BF16)
    logits = (_dot(h_hi, whi_ref[...]) + (_dot(h_hi, wlo_ref[...]) + _dot(h_lo, whi_ref[...]))) + br_ref[...]
    tm = logits.shape[0]
    lane = lax.broadcasted_iota(jnp.int32, (tm, LANES), 1).astype(F32)
    neg_inf = jnp.float32(-jnp.inf)

    def first_argmax(v):
        top = jnp.max(v, axis=-1, keepdims=True)
        return top, jnp.min(jnp.where(v == top, lane, float(LANES)), axis=-1, keepdims=True)

    is_group = lane < N_GROUPS
    gl = jnp.where(is_group, logits, neg_inf)
    g_top, g_sel = first_argmax(gl)
    g_w = 1.0 / jnp.sum(jnp.where(is_group, jnp.exp(gl - g_top), 0.0), axis=-1, keepdims=True)
    first = N_GROUPS + g_sel * EXPERTS_PER_GROUP
    el = jnp.where(jnp.logical_and(lane >= first, lane < first + EXPERTS_PER_GROUP), logits, neg_inf)
    v1, i1 = first_argmax(el)
    v2, i2 = first_argmax(jnp.where(lane == i1, neg_inf, el))
    e21 = jnp.exp(v2 - v1)
    w1 = g_w / (1.0 + e21)
    w2 = g_w * e21 / (1.0 + e21)
    e1 = i1 - N_GROUPS
    e2 = i2 - N_GROUPS

    oh1 = jnp.where(lane == e1, 1.0, 0.0)
    oh2 = jnp.where(lane == e2, 1.0, 0.0)
    both = oh1 + oh2
    before = _dot(tri_ref[...], both.astype(BF16)) + carry_ref[...]
    r1 = jnp.sum(before * oh1, axis=-1, keepdims=True)
    r2 = jnp.sum(before * oh2, axis=-1, keepdims=True)
    carry_ref[...] += jnp.sum(both, axis=0, keepdims=True)

    meta = jnp.zeros((tm, LANES), F32)
    for ln, val in ((META_E, e1), (META_E + 1, e2), (META_W, w1), (META_W + 1, w2),
                    (META_R, r1), (META_R + 1, r2)):
        meta = jnp.where(lane == ln, val, meta)
    meta_ref[...] = meta
    cnt_ref[...] = carry_ref[...].astype(cnt_ref.dtype)


def _router(x, shift, scale, g, w_router, b_router, tm):
    t, d = x.shape
    nb = tm // ADA_BLOCK
    tri = (jnp.arange(tm)[:, None] > jnp.arange(tm)[None, :]).astype(BF16)
    w_hi = w_router.astype(BF16)
    w_lo = (w_router - w_hi.astype(F32)).astype(BF16)
    return pl.pallas_call(
        _router_kernel,
        grid=(t // tm,),
        in_specs=[pl.BlockSpec((tm, d), lambda i: (i, 0)),
                  pl.BlockSpec((nb, d), lambda i: (i, 0)),
                  pl.BlockSpec((nb, d), lambda i: (i, 0)),
                  pl.BlockSpec((1, d), lambda i: (0, 0)),
                  pl.BlockSpec((d, LANES), lambda i: (0, 0)),
                  pl.BlockSpec((d, LANES), lambda i: (0, 0)),
                  pl.BlockSpec((1, LANES), lambda i: (0, 0)),
                  pl.BlockSpec((tm, tm), lambda i: (0, 0))],
        out_specs=[pl.BlockSpec((tm, d), lambda i: (i, 0)),
                   pl.BlockSpec((tm, LANES), lambda i: (i, 0)),
                   pl.BlockSpec((1, LANES), lambda i: (0, 0))],
        out_shape=[jax.ShapeDtypeStruct((t, d), F32),
                   jax.ShapeDtypeStruct((t, LANES), F32),
                   jax.ShapeDtypeStruct((1, LANES), jnp.int32)],
        scratch_shapes=[pltpu.VMEM((1, LANES), F32)],
        compiler_params=_params(("arbitrary",)),
        name="moe_router",
    )(x, shift, scale, g.reshape(1, d), w_hi, w_lo, b_router, tri)


def _row_copy(src_ref, dst_ref, sem, src_row, dst_row):
    return pltpu.make_async_copy(src_ref.at[pl.ds(src_row, 1)], dst_ref.at[pl.ds(dst_row, 1)], sem)


def _wait_rows(hbm_ref, sem, n):
    rows = hbm_ref.at[pl.ds(0, n)]
    pltpu.make_async_copy(rows, rows, sem).wait()


DMA_UNROLL = 8


def _dispatch_kernel(pos_ref, h_ref, xs_ref, sem):
    n = h_ref.shape[0]

    def start(r, carry):
        for k in range(2):
            _row_copy(h_ref, xs_ref, sem, r, pos_ref[0, 0, 2 * r + k]).start(priority=k)
        return carry
    lax.fori_loop(0, n, start, 0, unroll=DMA_UNROLL)
    _wait_rows(xs_ref, sem, 2 * n)


def _dispatch(h, pos, ch):
    t, d = h.shape
    return pl.pallas_call(
        _dispatch_kernel,
        grid=(t // ch,),
        in_specs=[pl.BlockSpec((1, 1, 2 * ch), lambda i: (i, 0, 0), memory_space=pltpu.SMEM),
                  pl.BlockSpec((ch, d), lambda i: (i, 0))],
        out_specs=pl.BlockSpec(memory_space=pl.ANY),
        out_shape=jax.ShapeDtypeStruct((2 * t, d), F32),
        scratch_shapes=[pltpu.SemaphoreType.DMA(())],
        compiler_params=_params(("arbitrary",)),
        name="moe_dispatch",
    )(pos.reshape(t // ch, 1, 2 * ch), h)


def _cast_rows(src_ref, dst_ref, rows):
    def body(c, carry):
        r = pl.ds(pl.multiple_of(c * rows, rows), rows)
        dst_ref[r, :] = src_ref[r, :].astype(dst_ref.dtype)
        return carry
    lax.fori_loop(0, src_ref.shape[0] // rows, body, 0)


def _expert_kernel(vt_ref, ve_ref, vlo_ref, vhi_ref, x_ref, wg_ref, wu_ref, wd_ref, o_ref,
                   wgb_ref, wub_ref, wdb_ref):
    v = pl.program_id(0)
    lo, hi = vlo_ref[v], vhi_ref[v]
    prev = jnp.maximum(v - 1, 0)

    @pl.when(jnp.logical_and(hi > lo, jnp.logical_or(v == 0, ve_ref[prev] != ve_ref[v])))
    def _():
        _cast_rows(wg_ref, wgb_ref, 64)
        _cast_rows(wu_ref, wub_ref, 64)
        _cast_rows(wd_ref, wdb_ref, 16)

    @pl.when(hi > lo)
    def _():
        x = x_ref[...].astype(BF16)
        a = _dot(x, wgb_ref[...])
        b = _dot(x, wub_ref[...])
        y = _dot((_silu(a) * b).astype(BF16), wdb_ref[...])
        row = lax.broadcasted_iota(jnp.int32, (x.shape[0], 1), 0)
        mine = jnp.logical_and(row >= lo, row < hi)
        first = jnp.logical_or(v == 0, vt_ref[prev] != vt_ref[v])

        @pl.when(first)
        def _():
            o_ref[...] = jnp.where(mine, y, 0.0)

        @pl.when(jnp.logical_not(first))
        def _():
            o_ref[...] = jnp.where(mine, y, o_ref[...])


def _expert_mlp(xs, visits, layer, wg, wu, wd):
    p, d = xs.shape
    de = wg.shape[3]
    tm = EXPERT_TILE
    n_visits = visits[0].shape[0]
    return pl.pallas_call(
        _expert_kernel,
        grid_spec=pltpu.PrefetchScalarGridSpec(
            num_scalar_prefetch=4,
            grid=(n_visits,),
            in_specs=[pl.BlockSpec((tm, d), lambda v, vt, ve, lo, hi: (vt[v], 0)),
                      pl.BlockSpec((None, None, d, de), lambda v, vt, ve, lo, hi: (layer, ve[v], 0, 0)),
                      pl.BlockSpec((None, None, d, de), lambda v, vt, ve, lo, hi: (layer, ve[v], 0, 0)),
                      pl.BlockSpec((None, None, de, d), lambda v, vt, ve, lo, hi: (layer, ve[v], 0, 0))],
            out_specs=pl.BlockSpec((tm, d), lambda v, vt, ve, lo, hi: (vt[v], 0)),
            scratch_shapes=[pltpu.VMEM((d, de), BF16), pltpu.VMEM((d, de), BF16), pltpu.VMEM((de, d), BF16)]),
        out_shape=jax.ShapeDtypeStruct((p, d), F32),
        compiler_params=_params(("arbitrary",)),
        name="moe_experts",
    )(*visits, xs, wg, wu, wd)


def _combine_kernel(n_first, pos_ref, npos_ref, x_ref, gate_ref, meta_ref, ys_ref, *refs):
    o_refs, (ybuf, sems) = refs[:-2], refs[-2:]
    i = pl.program_id(0)
    n = pl.num_programs(0)
    tm = x_ref.shape[0]

    def fetch(p_ref, slot):
        def body(r, carry):
            for k in range(2):
                _row_copy(ys_ref, ybuf.at[slot, k], sems.at[slot], p_ref[0, 0, 2 * r + k], r).start(priority=k)
            return carry
        lax.fori_loop(0, tm, body, 0, unroll=DMA_UNROLL)

    @pl.when(i == 0)
    def _():
        fetch(pos_ref, 0)

    @pl.when(i + 1 < n)
    def _():
        fetch(npos_ref, (i + 1) % 2)

    slot = i % 2

    _wait_rows(ys_ref, sems.at[slot], 2 * tm)

    def write(o_ref):
        def body(r, carry):
            rows = pl.ds(pl.multiple_of(r * ADA_BLOCK, ADA_BLOCK), ADA_BLOCK)
            meta = meta_ref[rows, :]
            y = (meta[:, META_W:META_W + 1] * ybuf[slot, 0, rows, :]
                 + meta[:, META_W + 1:META_W + 2] * ybuf[slot, 1, rows, :])
            o_ref[rows, :] = x_ref[rows, :] + gate_ref[pl.ds(r, 1), :] * y
            return carry
        lax.fori_loop(0, tm // ADA_BLOCK, body, 0)

    if n_first is None:
        write(o_refs[0])
    else:
        pl.when(i < n_first)(lambda: write(o_refs[0]))
        pl.when(i >= n_first)(lambda: write(o_refs[1]))


def _combine(x, gate, meta, ys, pos, tm, split_rows=None):
    t, d = x.shape
    nb = tm // ADA_BLOCK
    n = t // tm
    pos3 = pos.reshape(n, 1, 2 * tm)
    if split_rows is None:
        n_first = None
        out_specs = pl.BlockSpec((tm, d), lambda i: (i, 0))
        out_shape = jax.ShapeDtypeStruct((t, d), F32)
    else:
        n_first = split_rows // tm
        out_specs = [pl.BlockSpec((tm, d), lambda i: (jnp.minimum(i, n_first - 1), 0)),
                     pl.BlockSpec((tm, d), lambda i: (jnp.maximum(i - n_first, 0), 0))]
        out_shape = [jax.ShapeDtypeStruct((split_rows, d), F32), jax.ShapeDtypeStruct((t - split_rows, d), F32)]
    return pl.pallas_call(
        functools.partial(_combine_kernel, n_first),
        grid=(n,),
        in_specs=[pl.BlockSpec((1, 1, 2 * tm), lambda i: (i, 0, 0), memory_space=pltpu.SMEM),
                  pl.BlockSpec((1, 1, 2 * tm), lambda i: (jnp.minimum(i + 1, n - 1), 0, 0), memory_space=pltpu.SMEM),
                  pl.BlockSpec((tm, d), lambda i: (i, 0)),
                  pl.BlockSpec((nb, d), lambda i: (i, 0)),
                  pl.BlockSpec((tm, LANES), lambda i: (i, 0)),
                  pl.BlockSpec(memory_space=pl.ANY)],
        out_specs=out_specs,
        out_shape=out_shape,
        scratch_shapes=[pltpu.VMEM((2, 2, tm, d), F32), pltpu.SemaphoreType.DMA((2,))],
        compiler_params=_params(("arbitrary",)),
        name="moe_combine",
    )(pos3, pos3, x, gate, meta, ys)


def _plan_kernel(n_tiles, cnt_ref, start_ref, vt_ref, ve_ref, vlo_ref, vhi_ref):
    tile = EXPERT_TILE
    n_visits = vt_ref.shape[0]

    def per_expert(e, carry):
        run, v = carry
        c = cnt_ref[0, e]
        end = run + c
        start_ref[e] = run
        first_tile = run // tile
        n_vis = jnp.where(c > 0, (end - 1) // tile - first_tile + 1, 0)

        def per_visit(j, _):
            t = first_tile + j
            vt_ref[v + j] = t
            ve_ref[v + j] = e
            vlo_ref[v + j] = jnp.maximum(run - t * tile, 0)
            vhi_ref[v + j] = jnp.minimum(end - t * tile, tile)
            return 0
        lax.fori_loop(0, n_vis, per_visit, 0)
        return end, v + n_vis

    _, n_real = lax.fori_loop(0, N_EXPERTS, per_expert, (jnp.int32(0), jnp.int32(0)))

    def trailing(v, _):
        vt_ref[v] = n_tiles - 1
        ve_ref[v] = ve_ref[jnp.maximum(n_real - 1, 0)]
        vlo_ref[v] = 0
        vhi_ref[v] = 0
        return 0
    lax.fori_loop(n_real, n_visits, trailing, 0)


def _moe_plan(counts, n_tiles):
    n_visits = n_tiles + N_EXPERTS - 1
    smem = pl.BlockSpec(memory_space=pltpu.SMEM)
    vis = jax.ShapeDtypeStruct((n_visits,), jnp.int32)
    return pl.pallas_call(
        functools.partial(_plan_kernel, n_tiles),
        in_specs=[smem],
        out_specs=[smem] * 5,
        out_shape=[jax.ShapeDtypeStruct((N_EXPERTS,), jnp.int32), vis, vis, vis, vis],
        name="moe_plan",
    )(counts)


def _hier_moe(x, shift, scale, gate, g, w_group, b_group, w_er, b_er, layer, wg, wu, wd, tm, split_rows=None):
    t, d = x.shape
    w_router = jnp.concatenate([w_group, jnp.moveaxis(w_er, 0, 1).reshape(d, N_EXPERTS)], axis=1)
    b_router = jnp.concatenate([b_group, b_er.reshape(N_EXPERTS)])
    n_route = N_GROUPS + N_EXPERTS
    w_router = jnp.pad(w_router, ((0, 0), (0, LANES - n_route)))
    b_router = jnp.pad(b_router, (0, LANES - n_route)).reshape(1, LANES)
    h, meta, counts = _router(x, shift, scale, g, w_router, b_router, min(tm, ROUTER_TILE))

    starts, *visits = _moe_plan(counts, 2 * t // EXPERT_TILE)
    expert = meta[:, META_E:META_E + 2].astype(jnp.int32)
    rank = meta[:, META_R:META_R + 2].astype(jnp.int32)
    pos = jnp.sum(jnp.where(expert[:, :, None] == jnp.arange(N_EXPERTS), starts, 0), axis=-1) + rank

    xs = _dispatch(h, pos, min(tm, GATHER_CHUNK))
    ys = _expert_mlp(xs, visits, layer, wg, wu, wd)
    return _combine(x, gate, meta, ys, pos, min(tm, COMBINE_TILE), split_rows)


def _cached_keys_kernel(past, seq, ck_ref, cv_ref, cki_ref, kn_ref, vn_ref, kin_ref,
                        k_ref, vt_ref, ki_ref, ks_ref, vs_ref, kis_ref):
    nkb, kb = k_ref.shape[0], k_ref.shape[1]
    n_keys = past + seq
    for cache_ref, new_ref, stage in ((ck_ref, kn_ref, ks_ref), (cv_ref, vn_ref, vs_ref), (cki_ref, kin_ref, kis_ref)):
        stage[0:past, :] = cache_ref[...]
        stage[past:n_keys, :] = new_ref[...]
        if nkb * kb > n_keys:
            stage[n_keys:, :] = jnp.zeros((nkb * kb - n_keys, stage.shape[1]), F32)
    for i in range(nkb):
        rows = slice(i * kb, (i + 1) * kb)
        k_ref[i] = ks_ref[rows, :].astype(k_ref.dtype)
        vt_ref[i] = vs_ref[rows, :].T.astype(vt_ref.dtype)
        ki_ref[i] = kis_ref[rows, :].astype(ki_ref.dtype)


def _cached_key_blocks(cache_k, cache_v, cache_ki, k_new, v_new, ki_new, kb):
    b, past, d_kv = cache_k.shape
    seq = k_new.shape[1]
    nkb = -(-(past + seq) // kb)
    per_stream = lambda a: pl.BlockSpec((None,) + a.shape[1:], lambda i: (i, 0, 0))
    blocked = lambda r, c: pl.BlockSpec((None, nkb, r, c), lambda i: (i, 0, 0, 0))
    args = (cache_k, cache_v, cache_ki, k_new, v_new, ki_new)
    return pl.pallas_call(
        functools.partial(_cached_keys_kernel, past, seq),
        grid=(b,),
        in_specs=[per_stream(a) for a in args],
        out_specs=[blocked(kb, d_kv), blocked(d_kv, kb), blocked(kb, IDX_DIM)],
        out_shape=[jax.ShapeDtypeStruct((b, nkb, kb, d_kv), BF16),
                   jax.ShapeDtypeStruct((b, nkb, d_kv, kb), BF16),
                   jax.ShapeDtypeStruct((b, nkb, kb, IDX_DIM), BF16)],
        scratch_shapes=[pltpu.VMEM((nkb * kb, d_kv), F32), pltpu.VMEM((nkb * kb, d_kv), F32),
                        pltpu.VMEM((nkb * kb, IDX_DIM), F32)],
        compiler_params=_params(("parallel",)),
        name="cached_key_blocks",
    )(*args)


def _key_blocks(k, vt_src, ki, n_pad, kb):
    b, s, _ = k.shape
    padk = lambda a: jnp.pad(a.astype(BF16), ((0, 0), (0, n_pad - s), (0, 0)))
    nkb = n_pad // kb
    k_blk = padk(k).reshape(b, nkb, kb, k.shape[2])
    vt_blk = jnp.swapaxes(padk(vt_src).reshape(b, nkb, kb, vt_src.shape[2]), 2, 3)
    ki_blk = padk(ki).reshape(b, nkb, kb, ki.shape[2])
    return k_blk, vt_blk, ki_blk


def kernel(x_prompt, x_sample, c_prompt, c_sample, cache_conv, state_ret, cache_k, cache_v, cache_kidx, norm_mix_g, norm_ffn_g, w_ada, b_ada, cr_w_in, conv_w, conv_b, conv_norm_g, conv_norm_b, ret_norm_g, cr_w_out, dsa_w_in, q_norm_g, k_norm_g, kidx_norm_g, dsa_w_out, moe_w_group, moe_b_group, moe_w_erouter, moe_b_erouter, moe_w_gate, moe_w_up, moe_w_down):
    bp, lp, d = x_prompt.shape
    bs, ls, _ = x_sample.shape
    tp, ts = bp * lp, bs * ls
    t = tp + ts
    depth = w_ada.shape[0]
    past = cache_k.shape[2]
    d_conv = conv_w.shape[2]
    d_ret = ret_norm_g.shape[1]
    d_q = dsa_w_out.shape[1]
    d_kv = cache_k.shape[3] * cache_k.shape[4]
    tm = _pick(math.gcd(tp, ts), (1024, 512, 256, 128))
    groups = ((0, bp, lp), (tp, bs, ls))

    x = jnp.concatenate([x_prompt.reshape(tp, d), x_sample.reshape(ts, d)], axis=0)

    c_all = jnp.concatenate([c_prompt, c_sample], axis=0)
    n_c = c_all.shape[0]
    c_all = jnp.pad(c_all, ((0, -n_c % 8), (0, 0)))
    ada = _ada(c_all, w_ada, b_ada)
    per_block = lambda a, n: jnp.broadcast_to(a[:, :, None, :], a.shape[:2] + (n, a.shape[2])).reshape(depth, -1, a.shape[2])
    ada_blk = jnp.concatenate([per_block(ada[:, :bp], lp // ADA_BLOCK),
                               per_block(ada[:, bp:bp + bs], ls // ADA_BLOCK)], axis=1)

    new_conv, new_ret, new_k, new_v, new_kidx = [], [], [], [], []
    for i in range(depth):
        sh1, sc1, g1, sh2, sc2, g2 = [ada_blk[i, :, m * d:(m + 1) * d] for m in range(6)]
        j = i // 2
        if i % 2 == 0:
            u = _modulated_matmul(x, sh1, sc1, norm_mix_g[i], cr_w_in[j].astype(BF16), tm, "cr_in_proj")
            a_out, b_out, bufs, states = [], [], [], []
            for gi, (row0, batch, seq) in enumerate(groups):
                buf0 = jnp.zeros((batch, CONV_WIDTH - 1, d_conv), F32) if gi == 0 else cache_conv[j]
                st0 = jnp.zeros((batch,) + state_ret.shape[2:], F32) if gi == 0 else state_ret[j]
                a, nbuf = _conv_branch(u, row0, batch, seq, buf0, conv_w[j], conv_b[j],
                                       conv_norm_g[j], conv_norm_b[j], d_conv)
                bo, nst = _retention_branch(u, row0, batch, seq, st0, ret_norm_g[j], 2 * d_conv // d_ret)
                a_out.append(a); b_out.append(bo); bufs.append(nbuf); states.append(nst)
            new_conv.append(bufs)
            new_ret.append(states)
            w_out = cr_w_out[j].astype(BF16)
            x = _outproj_residual([a_out, b_out], [w_out[:d_conv], w_out[d_conv:]], x, g1, tm, "cr_out_proj")
        else:
            w_in = dsa_w_in[j].astype(BF16)
            n_main = d_q + 2 * d_kv + IDX_HEADS * IDX_DIM
            w_x = jnp.pad(w_in[:, n_main:], ((0, 0), (0, LANES - (w_in.shape[1] - n_main))))
            u, ux = _modulated_matmul(x, sh1, sc1, norm_mix_g[i], w_in[:, :n_main], tm, "dsa_in_proj", w_extra=w_x)
            q, k, qidx, kidx, k_bf, ki_bf, vt_bf = _qk_norms(u, ux, q_norm_g[j], k_norm_g[j], kidx_norm_g[j],
                                                             tm, d_q, d_kv)
            v = u[:, d_q + d_kv:d_q + 2 * d_kv]
            widx = ux[:, IDX_DIM:IDX_DIM + IDX_HEADS]
            outs, ks, vs, kis = [], [], [], []
            for gi, (row0, batch, seq) in enumerate(groups):
                rows = slice(row0, row0 + batch * seq)
                kg = k[rows].reshape(batch, seq, d_kv)
                vg = v[rows].reshape(batch, seq, d_kv)
                kig = kidx[rows].reshape(batch, seq, IDX_DIM)
                ks.append(kg); vs.append(vg); kis.append(kig)
                g_past = 0 if gi == 0 else past
                n_keys = g_past + seq
                kb = 256 if n_keys >= 256 else 128
                n_pad = -(-n_keys // kb) * kb
                if g_past:
                    k_blk, vt_blk, ki_blk = _cached_key_blocks(
                        cache_k[j].reshape(batch, past, d_kv), cache_v[j].reshape(batch, past, d_kv),
                        cache_kidx[j], kg, vg, kig, kb)
                elif n_pad == n_keys and vt_bf.shape[2] == kb:
                    blocks = slice(row0 // kb, (row0 + batch * seq) // kb)
                    k_blk = k_bf[rows].reshape(batch, seq // kb, kb, d_kv)
                    vt_blk = vt_bf[blocks].reshape(batch, seq // kb, d_kv, kb)
                    ki_blk = ki_bf[rows].reshape(batch, seq // kb, kb, IDX_DIM)
                else:
                    k_blk, vt_blk, ki_blk = _key_blocks(kg, vg, kig, n_pad, kb)
                outs.append(_sparse_attention(q, qidx, widx[rows], k_blk, vt_blk, ki_blk,
                                              row0, batch, seq, g_past, n_keys, kb))
            new_k.append(ks); new_v.append(vs); new_kidx.append(kis)
            x = _outproj_residual([outs], [dsa_w_out[j].astype(BF16)], x, g1, tm, "dsa_out_proj")
        x = _hier_moe(x, sh2, sc2, g2, norm_ffn_g[i], moe_w_group[i], moe_b_group[i], moe_w_erouter[i],
                      moe_b_erouter[i], i, moe_w_gate, moe_w_up, moe_w_down, tm,
                      split_rows=tp if i == depth - 1 else None)

    y_p, y_s = x
    kv_heads, hd = cache_k.shape[3], cache_k.shape[4]
    stack = lambda per_layer, gi, shape: jnp.stack([lay[gi].reshape(shape) for lay in per_layer])
    return (y_p.reshape(bp, lp, d), y_s.reshape(bs, ls, d),
            stack(new_conv, 0, (bp, CONV_WIDTH - 1, d_conv)), stack(new_conv, 1, (bs, CONV_WIDTH - 1, d_conv)),
            stack(new_ret, 0, (bp,) + state_ret.shape[2:]), stack(new_ret, 1, (bs,) + state_ret.shape[2:]),
            stack(new_k, 0, (bp, lp, kv_heads, hd)), stack(new_k, 1, (bs, ls, kv_heads, hd)),
            stack(new_v, 0, (bp, lp, kv_heads, hd)), stack(new_v, 1, (bs, ls, kv_heads, hd)),
            stack(new_kidx, 0, (bp, lp, IDX_DIM)), stack(new_kidx, 1, (bs, ls, IDX_DIM)))
```

```python
import functools
import math

import jax
import jax.numpy as jnp
from jax import lax
from jax.experimental import pallas as pl
from jax.experimental.pallas import tpu as pltpu

F32 = jnp.float32
BF16 = jnp.bfloat16

EPS = 1e-6
CHUNK = 64
ADA_BLOCK = 32
CONV_WIDTH = 31
RET_HEADS = 8
RET_DK = 128
ATT_HEADS = 16
ATT_KV_HEADS = 4
IDX_HEADS = 16
IDX_DIM = 64
TOPK_MAX = 256
N_GROUPS = 4
EXPERTS_PER_GROUP = 8
N_EXPERTS = N_GROUPS * EXPERTS_PER_GROUP
LANES = 128
VMEM_LIMIT = 56 * 1024 * 1024
BISECT_ITERS = 30
BISECT_ROUND = 5
MASKED_DIST = 1e30
LOG2E = math.log2(math.e)
EXPERT_TILE = 256
GATHER_CHUNK = 1024
ROUTER_TILE = 512
COMBINE_TILE = 256


def _params(sem, vmem=VMEM_LIMIT):
    return pltpu.CompilerParams(dimension_semantics=sem, vmem_limit_bytes=vmem)


def _pick(n, cands):
    for c in cands:
        if n % c == 0:
            return c
    raise ValueError(f"no tile in {cands} divides {n}")


def _dot(a, b):
    return jnp.dot(a, b, preferred_element_type=F32)


def _dot_nt(a, b):
    return lax.dot_general(a, b, (((1,), (1,)), ((), ())), preferred_element_type=F32)


def _dot_tn(a, b):
    return lax.dot_general(a, b, (((0,), (0,)), ((), ())), preferred_element_type=F32)


def _silu(x):
    return x * jax.nn.sigmoid(x)


def _ada_kernel(c_ref, w_ref, b_ref, o_ref):
    c = _silu(c_ref[...]).astype(BF16)
    o_ref[...] = _dot(c, w_ref[...].astype(BF16)) + b_ref[...]


def _ada(c_all, w_ada, b_ada):
    depth, d, n = w_ada.shape
    rows = c_all.shape[0]
    tn = _pick(n, (1024, 512, 256, 128))
    return pl.pallas_call(
        _ada_kernel,
        grid=(depth, n // tn),
        in_specs=[pl.BlockSpec((rows, d), lambda l, j: (0, 0)),
                  pl.BlockSpec((None, d, tn), lambda l, j: (l, 0, j)),
                  pl.BlockSpec((None, 1, tn), lambda l, j: (l, 0, j))],
        out_specs=pl.BlockSpec((None, rows, tn), lambda l, j: (l, 0, j)),
        out_shape=jax.ShapeDtypeStruct((depth, rows, n), F32),
        compiler_params=_params(("parallel", "parallel")),
        name="ada",
    )(c_all, w_ada, b_ada.reshape(depth, 1, n))


def _modulate_rows(x_ref, shift_ref, scale_ref, g_ref, store):
    nblk = x_ref.shape[0] // ADA_BLOCK

    def body(r, carry):
        rows = pl.ds(pl.multiple_of(r * ADA_BLOCK, ADA_BLOCK), ADA_BLOCK)
        x = x_ref[rows, :]
        y = x * lax.rsqrt(jnp.mean(x * x, axis=-1, keepdims=True) + EPS) * g_ref[...]
        y = y * (1.0 + scale_ref[pl.ds(r, 1), :]) + shift_ref[pl.ds(r, 1), :]
        store(rows, y)
        return carry

    lax.fori_loop(0, nblk, body, 0)


def _modmm_kernel(has_extra, x_ref, shift_ref, scale_ref, g_ref, w_ref, *refs):
    if has_extra:
        wx_ref, o_ref, ox_ref, h_ref = refs
    else:
        o_ref, h_ref = refs

    @pl.when(pl.program_id(1) == 0)
    def _():
        def store(rows, y):
            h_ref[rows, :] = y.astype(BF16)
        _modulate_rows(x_ref, shift_ref, scale_ref, g_ref, store)
        if has_extra:
            ox_ref[...] = _dot(h_ref[...], wx_ref[...])

    o_ref[...] = _dot(h_ref[...], w_ref[...])


def _modulated_matmul(x, shift, scale, g, w, tm, name, w_extra=None):
    t, d = x.shape
    n = w.shape[1]
    tn = _pick(n, (1024, 512, 256, 128))
    nb = tm // ADA_BLOCK
    in_specs = [pl.BlockSpec((tm, d), lambda i, j: (i, 0)),
                pl.BlockSpec((nb, d), lambda i, j: (i, 0)),
                pl.BlockSpec((nb, d), lambda i, j: (i, 0)),
                pl.BlockSpec((1, d), lambda i, j: (0, 0)),
                pl.BlockSpec((d, tn), lambda i, j: (0, j))]
    out_specs = pl.BlockSpec((tm, tn), lambda i, j: (i, j))
    out_shape = jax.ShapeDtypeStruct((t, n), F32)
    args = [x, shift, scale, g.reshape(1, d), w]
    if w_extra is not None:
        nx = w_extra.shape[1]
        in_specs.append(pl.BlockSpec((d, nx), lambda i, j: (0, 0)))
        out_specs = [out_specs, pl.BlockSpec((tm, nx), lambda i, j: (i, 0))]
        out_shape = [out_shape, jax.ShapeDtypeStruct((t, nx), F32)]
        args.append(w_extra)
    return pl.pallas_call(
        functools.partial(_modmm_kernel, w_extra is not None),
        grid=(t // tm, n // tn),
        in_specs=in_specs,
        out_specs=out_specs,
        out_shape=out_shape,
        scratch_shapes=[pltpu.VMEM((tm, d), BF16)],
        compiler_params=_params(("parallel", "arbitrary")),
        name=name,
    )(*args)


def _outproj_kernel(n_in, tile_ranges, *refs):
    n_grp = len(tile_ranges)
    a_refs = [refs[p * n_grp:(p + 1) * n_grp] for p in range(n_in)]
    w_refs = refs[n_in * n_grp:n_in * n_grp + n_in]
    x_ref, gate_ref, o_ref, y_ref = refs[n_in * n_grp + n_in:]
    i = pl.program_id(0)
    for g, (lo, hi) in enumerate(tile_ranges):
        @pl.when(jnp.logical_and(i >= lo, i < hi))
        def _():
            y = _dot(a_refs[0][g][...], w_refs[0][...])
            for p in range(1, n_in):
                y = y + _dot(a_refs[p][g][...], w_refs[p][...])
            y_ref[...] = y
    nblk = x_ref.shape[0] // ADA_BLOCK

    def body(r, carry):
        rows = pl.ds(pl.multiple_of(r * ADA_BLOCK, ADA_BLOCK), ADA_BLOCK)
        o_ref[rows, :] = x_ref[rows, :] + gate_ref[pl.ds(r, 1), :] * y_ref[rows, :]
        return carry

    lax.fori_loop(0, nblk, body, 0)


def _outproj_residual(acts, ws, x, gate, tm, name):
    t, d = x.shape
    tn = _pick(d, (1024, 512, 256, 128))
    nb = tm // ADA_BLOCK
    n_in = len(acts)
    tile_ranges, lo = [], 0
    for a in acts[0]:
        tile_ranges.append((lo, lo + a.shape[0] // tm))
        lo = tile_ranges[-1][1]

    def group_spec(a, lo, hi):
        return pl.BlockSpec((tm, a.shape[1]), lambda i, j: (jnp.clip(i - lo, 0, hi - lo - 1), 0))

    in_specs = ([group_spec(a, *tile_ranges[g]) for piece in acts for g, a in enumerate(piece)]
                + [pl.BlockSpec((w.shape[0], tn), lambda i, j: (0, j)) for w in ws]
                + [pl.BlockSpec((tm, tn), lambda i, j: (i, j)),
                   pl.BlockSpec((nb, tn), lambda i, j: (i, j))])
    return pl.pallas_call(
        functools.partial(_outproj_kernel, n_in, tuple(tile_ranges)),
        grid=(t // tm, d // tn),
        in_specs=in_specs,
        out_specs=pl.BlockSpec((tm, tn), lambda i, j: (i, j)),
        out_shape=jax.ShapeDtypeStruct((t, d), F32),
        scratch_shapes=[pltpu.VMEM((tm, tn), F32)],
        compiler_params=_params(("parallel", "parallel")),
        name=name,
    )(*[a for piece in acts for a in piece], *ws, x, gate)


CONV_ROWS = 32
CONV_HIST = 32


def _conv_kernel(tl, val_ref, gate_ref, pval_ref, pgate_ref, buf_ref, w_ref, b_ref, ng_ref, nb_ref,
                 o_ref, nbuf_ref, up_ref, sh_ref):
    li = pl.program_id(1)
    hist = CONV_WIDTH - 1
    pad = CONV_HIST - hist
    glu = val_ref[...] * jax.nn.sigmoid(gate_ref[...])
    up_ref[CONV_HIST:CONV_HIST + tl, :] = glu
    up_ref[0:pad, :] = jnp.zeros((pad, up_ref.shape[1]), F32)

    @pl.when(li == 0)
    def _():
        up_ref[pad:CONV_HIST, :] = buf_ref[...]

    @pl.when(li > 0)
    def _():
        prev = pval_ref[...] * jax.nn.sigmoid(pgate_ref[...])
        up_ref[pad:CONV_HIST, :] = prev[pad:, :]

    n_sh = tl + CONV_HIST - 8
    for s in range(1, 8):
        sh_ref[s - 1, 0:n_sh, :] = up_ref[s:s + n_sh, :]

    def tap_rows(row):
        base, s = row - row % 8, row % 8
        src = up_ref if s == 0 else sh_ref.at[s - 1]
        return src[base:base + CONV_ROWS, :]

    for c in range(tl // CONV_ROWS):
        r0 = c * CONV_ROWS
        acc = jnp.zeros((CONV_ROWS, val_ref.shape[1]), F32)
        for j in range(CONV_WIDTH):
            acc = acc + tap_rows(r0 + pad + j) * w_ref[j:j + 1, :]
        acc = acc + b_ref[...]
        mu = jnp.mean(acc, axis=-1, keepdims=True)
        dlt = acc - mu
        y = dlt * lax.rsqrt(jnp.mean(dlt * dlt, axis=-1, keepdims=True) + EPS)
        y = y * ng_ref[...] + nb_ref[...]
        o_ref[r0:r0 + CONV_ROWS, :] = _silu(y).astype(o_ref.dtype)

    @pl.when(li == pl.num_programs(1) - 1)
    def _():
        nbuf_ref[...] = up_ref[CONV_HIST + tl - hist:CONV_HIST + tl, :]


def _conv_branch(u, row0, batch, seq, conv_buf, conv_w, conv_b, cn_g, cn_b, d_conv):
    tl = _pick(seq, (128, 64, 32))
    nl = seq // tl
    hist = CONV_WIDTH - 1
    rb = row0 // tl
    pb = tl // CONV_HIST
    cur = lambda col: pl.BlockSpec((tl, d_conv), lambda b, l: (rb + b * nl + l, col))
    prev = lambda col: pl.BlockSpec(
        (CONV_HIST, d_conv), lambda b, l: (jnp.maximum((rb + b * nl + l) * pb - 1, 0), col))
    vec = pl.BlockSpec((1, d_conv), lambda b, l: (0, 0))
    return pl.pallas_call(
        functools.partial(_conv_kernel, tl),
        grid=(batch, nl),
        in_specs=[cur(0), cur(1), prev(0), prev(1),
                  pl.BlockSpec((None, hist, d_conv), lambda b, l: (b, 0, 0)),
                  pl.BlockSpec((CONV_WIDTH, d_conv), lambda b, l: (0, 0)),
                  vec, vec, vec],
        out_specs=[pl.BlockSpec((tl, d_conv), lambda b, l: (b * nl + l, 0)),
                   pl.BlockSpec((None, hist, d_conv), lambda b, l: (b, 0, 0))],
        out_shape=[jax.ShapeDtypeStruct((batch * seq, d_conv), BF16),
                   jax.ShapeDtypeStruct((batch, hist, d_conv), F32)],
        scratch_shapes=[pltpu.VMEM((CONV_HIST + tl, d_conv), F32),
                        pltpu.VMEM((7, CONV_HIST + tl, d_conv), F32)],
        compiler_params=_params(("parallel", "arbitrary")),
        name="conv_branch",
    )(u, u, u, u, conv_buf, conv_w, conv_b.reshape(1, -1), cn_g.reshape(1, -1), cn_b.reshape(1, -1))


def _retention_kernel(q_ref, k_ref, v_ref, g_ref, s0_ref, din_ref, dq_ref, dk_ref, dblk_ref, rg_ref,
                      o_ref, s_out_ref, s_ref):
    ci = pl.program_id(1)

    @pl.when(ci == 0)
    def _():
        s_ref[...] = s0_ref[...]

    dv = s_ref.shape[2]
    for h in range(RET_HEADS):
        cols = slice(h * dv, (h + 1) * dv)
        q = q_ref[:, cols].astype(BF16)
        k = k_ref[:, cols] * (RET_DK ** -0.5)
        v = v_ref[:, cols].astype(BF16)
        s_prev = s_ref[h]
        sc = _dot_nt(q, k.astype(BF16)) * din_ref[h]
        o = _dot(sc.astype(BF16), v) + _dot(q, s_prev.astype(BF16)) * dq_ref[h]
        kd = (k * dk_ref[h]).astype(BF16)
        s_ref[h] = s_prev * dblk_ref[h] + _dot_tn(kd, v)
        o = o * lax.rsqrt(jnp.mean(o * o, axis=-1, keepdims=True) + EPS) * rg_ref[:, cols]
        o_ref[:, cols] = (o * _silu(g_ref[:, cols])).astype(o_ref.dtype)

    @pl.when(ci == pl.num_programs(1) - 1)
    def _():
        s_out_ref[...] = s_ref[...]


def _retention_branch(u, row0, batch, seq, state0, ret_g, col_q):
    heads, dk, dv = state0.shape[1:]
    d_ret = heads * dv
    c = _pick(seq, (256, 128, 64, 32))
    nc = seq // c
    rb = row0 // c
    lg = jnp.log1p(-(2.0 ** (-5.0 - jnp.arange(heads, dtype=F32))))
    pos = jnp.arange(c, dtype=F32)
    diff = pos[:, None] - pos[None, :]
    d_in = jnp.where(diff >= 0, jnp.exp(lg[:, None, None] * jnp.maximum(diff, 0.0)), 0.0)
    d_q = jnp.broadcast_to(jnp.exp(lg[:, None] * (pos[None, :] + 1.0))[:, :, None], (heads, c, dv))
    d_k = jnp.broadcast_to(jnp.exp(lg[:, None] * (c - 1.0 - pos[None, :]))[:, :, None], (heads, c, dk))
    d_blk = jnp.broadcast_to(jnp.exp(lg * c)[:, None, None], (heads, dk, dv))
    blk = lambda col: pl.BlockSpec((c, d_ret), lambda b, i: (rb + b * nc + i, col))
    const3 = lambda shape: pl.BlockSpec(shape, lambda b, i: (0, 0, 0))
    return pl.pallas_call(
        _retention_kernel,
        grid=(batch, nc),
        in_specs=[blk(col_q), blk(col_q + 1), blk(col_q + 2), blk(col_q + 3),
                  pl.BlockSpec((None, heads, dk, dv), lambda b, i: (b, 0, 0, 0)),
                  const3((heads, c, c)), const3((heads, c, dv)), const3((heads, c, dk)),
                  const3((heads, dk, dv)),
                  pl.BlockSpec((1, d_ret), lambda b, i: (0, 0))],
        out_specs=[pl.BlockSpec((c, d_ret), lambda b, i: (b * nc + i, 0)),
                   pl.BlockSpec((None, heads, dk, dv), lambda b, i: (b, 0, 0, 0))],
        out_shape=[jax.ShapeDtypeStruct((batch * seq, d_ret), BF16),
                   jax.ShapeDtypeStruct(state0.shape, F32)],
        scratch_shapes=[pltpu.VMEM((heads, dk, dv), F32)],
        compiler_params=_params(("parallel", "arbitrary")),
        name="retention_branch",
    )(u, u, u, u, state0, d_in, d_q, d_k, d_blk, ret_g.reshape(1, d_ret))


def _head_rms(x, g, hd):
    outs = []
    for h in range(x.shape[1] // hd):
        xh = x[:, h * hd:(h + 1) * hd]
        outs.append(xh * lax.rsqrt(jnp.mean(xh * xh, axis=-1, keepdims=True) + EPS) * g)
    return outs


def _qknorm_kernel(q_ref, k_ref, v_ref, qi_ref, ki_ref, qg_ref, kg_ref, kig_ref,
                   qo_ref, ko_ref, qio_ref, kio_ref, kb_ref, kib_ref, vt_ref):
    hd = qg_ref.shape[1]
    qio_ref[...] = qi_ref[...].astype(qio_ref.dtype)
    q_scale = hd ** -0.5 * LOG2E
    for h, qh in enumerate(_head_rms(q_ref[...], qg_ref[...], hd)):
        qo_ref[:, h * hd:(h + 1) * hd] = (qh * q_scale).astype(qo_ref.dtype)
    for h, kh in enumerate(_head_rms(k_ref[...], kg_ref[...], hd)):
        ko_ref[:, h * hd:(h + 1) * hd] = kh
        kb_ref[:, h * hd:(h + 1) * hd] = kh.astype(kb_ref.dtype)
    ki = ki_ref[:, :IDX_DIM]
    ki = ki * lax.rsqrt(jnp.mean(ki * ki, axis=-1, keepdims=True) + EPS) * kig_ref[...]
    kio_ref[...] = ki
    kib_ref[...] = ki.astype(kib_ref.dtype)
    blk = vt_ref.shape[2]
    for c in range(vt_ref.shape[0]):
        vt_ref[c] = v_ref[c * blk:(c + 1) * blk, :].T.astype(vt_ref.dtype)


def _qk_norms(u, ux, q_g, k_g, kidx_g, tm, d_q, d_kv):
    t = u.shape[0]
    hd = q_g.shape[0]
    d_qi = IDX_HEADS * IDX_DIM
    blk = min(tm, 256)
    return pl.pallas_call(
        _qknorm_kernel,
        grid=(t // tm,),
        in_specs=[pl.BlockSpec((tm, d_q), lambda i: (i, 0)),
                  pl.BlockSpec((tm, d_kv), lambda i: (i, d_q // d_kv)),
                  pl.BlockSpec((tm, d_kv), lambda i: (i, d_q // d_kv + 1)),
                  pl.BlockSpec((tm, d_qi), lambda i: (i, (d_q + 2 * d_kv) // d_qi)),
                  pl.BlockSpec((tm, LANES), lambda i: (i, 0)),
                  pl.BlockSpec((1, hd), lambda i: (0, 0)),
                  pl.BlockSpec((1, hd), lambda i: (0, 0)),
                  pl.BlockSpec((1, IDX_DIM), lambda i: (0, 0))],
        out_specs=[pl.BlockSpec((tm, d_q), lambda i: (i, 0)),
                   pl.BlockSpec((tm, d_kv), lambda i: (i, 0)),
                   pl.BlockSpec((tm, d_qi), lambda i: (i, 0)),
                   pl.BlockSpec((tm, IDX_DIM), lambda i: (i, 0)),
                   pl.BlockSpec((tm, d_kv), lambda i: (i, 0)),
                   pl.BlockSpec((tm, IDX_DIM), lambda i: (i, 0)),
                   pl.BlockSpec((tm // blk, d_kv, blk), lambda i: (i, 0, 0))],
        out_shape=[jax.ShapeDtypeStruct((t, d_q), BF16),
                   jax.ShapeDtypeStruct((t, d_kv), F32),
                   jax.ShapeDtypeStruct((t, d_qi), BF16),
                   jax.ShapeDtypeStruct((t, IDX_DIM), F32),
                   jax.ShapeDtypeStruct((t, d_kv), BF16),
                   jax.ShapeDtypeStruct((t, IDX_DIM), BF16),
                   jax.ShapeDtypeStruct((t // blk, d_kv, blk), BF16)],
        compiler_params=_params(("parallel",)),
        name="qk_norms",
    )(u, u, u, u, ux, q_g.reshape(1, hd), k_g.reshape(1, hd), kidx_g.reshape(1, IDX_DIM))


def _dsa_kernel(tq, rep, kb, past, n_keys, n_sel, q_ref, qi_ref, wt_ref, slope_ref, tri_ref,
                k_ref, vt_ref, ki_ref, o_ref, sc_ref, dist_ref, qg_ref, acc_ref, z_ref, rel_ref):
    t0 = pl.program_id(1) * tq
    hd = k_ref.shape[2] // ATT_KV_HEADS
    grp = ATT_HEADS // ATT_KV_HEADS
    wq = rep * tq
    n_adm_tile = jnp.minimum(((past + t0 + tq - 1) // CHUNK + 1) * CHUNK, n_keys)
    nkb = (n_adm_tile + kb - 1) // kb
    q_pos = past + t0 + lax.broadcasted_iota(jnp.int32, (1, wq), 1) % tq
    q_chunk = q_pos // CHUNK
    n_adm = jnp.minimum((q_chunk + 1) * CHUNK, n_keys).astype(F32)
    need = jnp.minimum(n_adm, float(n_sel))
    neg_inf = jnp.float32(-jnp.inf)
    w_t = wt_ref[...] * (IDX_HEADS ** -0.5 * IDX_DIM ** -0.5)

    def key_pos(i):
        return i * kb + lax.broadcasted_iota(jnp.int32, (kb, 1), 0)

    def admissible(i):
        kp = key_pos(i)
        return jnp.logical_and(kp // CHUNK <= q_chunk, kp < n_keys)

    def score_body(i, carry):
        lo, hi = carry
        ki = ki_ref[i]
        for h in range(IDX_HEADS):
            qi = jnp.concatenate([qi_ref[:, h * IDX_DIM:(h + 1) * IDX_DIM]] * rep, axis=0)
            rel_ref[h] = _dot_nt(ki, qi)
        acc = jnp.zeros((kb, wq), F32)
        for h in range(IDX_HEADS):
            acc = acc + jnp.maximum(rel_ref[h], 0.0) * w_t[h:h + 1, :]
        adm = admissible(i)
        sc_ref[i] = jnp.where(adm, acc, neg_inf)
        lo = jnp.minimum(lo, jnp.min(jnp.where(adm, acc, jnp.inf), axis=0, keepdims=True))
        hi = jnp.maximum(hi, jnp.max(jnp.where(adm, acc, neg_inf), axis=0, keepdims=True))
        return lo, hi

    lo, hi = lax.fori_loop(0, nkb, score_body,
                           (jnp.full((1, wq), jnp.inf, F32), jnp.full((1, wq), neg_inf, F32)))

    def count(pred):
        def body(i, acc):
            ones = jnp.where(pred(sc_ref[i]), 1.0, 0.0)
            parts = [ones[r:r + 8, :] for r in range(0, kb, 8)]
            while len(parts) > 1:
                parts = [parts[a] + parts[a + 1] for a in range(0, len(parts), 2)]
            return acc + parts[0]
        return jnp.sum(lax.fori_loop(0, nkb, body, jnp.zeros((8, wq), F32)), axis=0, keepdims=True)

    def bisect(_, carry):
        lo, hi, c_lo = carry
        mid = 0.5 * (lo + hi)
        c_mid = count(lambda s: s >= mid)
        ge = c_mid >= need
        return jnp.where(ge, mid, lo), jnp.where(ge, hi, mid), jnp.where(ge, c_mid, c_lo)

    def bisect_round(carry):
        rnd, lo, hi, c_lo = carry
        lo, hi, c_lo = lax.fori_loop(0, BISECT_ROUND, bisect, (lo, hi, c_lo))
        return rnd + 1, lo, hi, c_lo

    def unresolved(carry):
        rnd, _, _, c_lo = carry
        return jnp.logical_and(rnd < BISECT_ITERS // BISECT_ROUND, jnp.max(c_lo - need) > 0.0)

    _, lo, hi, c_lo = lax.while_loop(unresolved, bisect_round, (jnp.int32(0), lo, hi, n_adm))

    def write_dist(i, sel):
        dist = jnp.abs(q_pos - key_pos(i)).astype(F32)
        dist_ref[i] = jnp.where(sel, dist, MASKED_DIST)

    resolved = jnp.max(c_lo - need) <= 0.0

    @pl.when(resolved)
    def _():
        def body(i, carry):
            write_dist(i, sc_ref[i] >= lo)
            return carry
        lax.fori_loop(0, nkb, body, 0)

    @pl.when(jnp.logical_not(resolved))
    def _():
        n_above = count(lambda s: s > hi)
        room = need - n_above

        def body(i, seen):
            s = sc_ref[i]
            above = s > hi
            band = jnp.logical_and(s >= lo, jnp.logical_not(above))
            band_f = jnp.where(band, 1.0, 0.0)
            rank = _dot(tri_ref[...], band_f.astype(BF16)) + seen
            write_dist(i, jnp.logical_or(above, jnp.logical_and(band, rank <= room)))
            return seen + jnp.sum(band_f, axis=0, keepdims=True)
        lax.fori_loop(0, nkb, body, jnp.zeros((1, wq), F32))

    for g in range(ATT_KV_HEADS):
        qg_ref[g] = jnp.concatenate(
            [q_ref[:, (g * grp + r) * hd:(g * grp + r + 1) * hd] for r in range(grp)], axis=0)
    acc_ref[...] = jnp.zeros_like(acc_ref)

    def att_body(i, carry):
        ms, ls = carry
        dist = jnp.concatenate([dist_ref[i]] * (grp // rep), axis=1)
        new_ms, new_ls = [], []
        for g in range(ATT_KV_HEADS):
            z_ref[g] = _dot_nt(k_ref[i, :, g * hd:(g + 1) * hd], qg_ref[g])
        for g in range(ATT_KV_HEADS):
            z = z_ref[g] - slope_ref[g] * dist
            m_new = jnp.maximum(ms[g], jnp.max(z, axis=0, keepdims=True))
            alpha = jnp.exp2(ms[g] - m_new)
            p = jnp.exp2(z - m_new)
            new_ls.append(ls[g] * alpha + jnp.sum(p, axis=0, keepdims=True))
            new_ms.append(m_new)
            acc_ref[g] = acc_ref[g] * alpha + _dot(vt_ref[i, g * hd:(g + 1) * hd, :], p.astype(BF16))
        return tuple(new_ms), tuple(new_ls)

    _, ls = lax.fori_loop(
        0, nkb, att_body,
        (tuple(jnp.full((1, grp * tq), neg_inf, F32) for _ in range(ATT_KV_HEADS)),
         tuple(jnp.zeros((1, grp * tq), F32) for _ in range(ATT_KV_HEADS))))
    for g in range(ATT_KV_HEADS):
        out = (acc_ref[g] / ls[g]).T
        for r in range(grp):
            o_ref[:, (g * grp + r) * hd:(g * grp + r + 1) * hd] = out[r * tq:(r + 1) * tq, :].astype(o_ref.dtype)


def _sparse_attention(q, qidx, widx, k_blk, vt_blk, ki_blk, row0, batch, seq, past, n_keys, kb):
    d_q = q.shape[1]
    nkb_all = k_blk.shape[1]
    d_kv = k_blk.shape[3]
    tq = _pick(seq, (256, 128, 64, 32))
    nq = seq // tq
    rb = row0 // tq
    grp = ATT_HEADS // ATT_KV_HEADS
    rep = max(1, LANES // tq)
    wq = rep * tq
    n_sel = min(TOPK_MAX, n_keys // 4)
    widx_t = jnp.tile(jnp.swapaxes(widx.reshape(batch * nq, tq, IDX_HEADS), 1, 2), (1, 1, rep))
    slopes = LOG2E * 2.0 ** (-8.0 * jnp.arange(1, ATT_HEADS + 1, dtype=F32) / ATT_HEADS)
    slope_rows = jnp.repeat(slopes.reshape(ATT_KV_HEADS, grp), tq, axis=1).reshape(ATT_KV_HEADS, 1, grp * tq)
    tri = (jnp.arange(kb)[:, None] >= jnp.arange(kb)[None, :]).astype(BF16)
    return pl.pallas_call(
        functools.partial(_dsa_kernel, tq, rep, kb, past, n_keys, n_sel),
        grid=(batch, nq),
        in_specs=[pl.BlockSpec((tq, d_q), lambda b, i: (rb + b * nq + i, 0)),
                  pl.BlockSpec((tq, qidx.shape[1]), lambda b, i: (rb + b * nq + i, 0)),
                  pl.BlockSpec((None, IDX_HEADS, wq), lambda b, i: (b * nq + i, 0, 0)),
                  pl.BlockSpec((ATT_KV_HEADS, 1, grp * tq), lambda b, i: (0, 0, 0)),
                  pl.BlockSpec((kb, kb), lambda b, i: (0, 0)),
                  pl.BlockSpec((None, nkb_all, kb, d_kv), lambda b, i: (b, 0, 0, 0)),
                  pl.BlockSpec((None, nkb_all, d_kv, kb), lambda b, i: (b, 0, 0, 0)),
                  pl.BlockSpec((None, nkb_all, kb, IDX_DIM), lambda b, i: (b, 0, 0, 0))],
        out_specs=pl.BlockSpec((tq, d_q), lambda b, i: (b * nq + i, 0)),
        out_shape=jax.ShapeDtypeStruct((batch * seq, d_q), BF16),
        scratch_shapes=[pltpu.VMEM((nkb_all, kb, wq), F32), pltpu.VMEM((nkb_all, kb, wq), F32),
                        pltpu.VMEM((ATT_KV_HEADS, grp * tq, d_kv // ATT_KV_HEADS), BF16),
                        pltpu.VMEM((ATT_KV_HEADS, d_kv // ATT_KV_HEADS, grp * tq), F32),
                        pltpu.VMEM((ATT_KV_HEADS, kb, grp * tq), F32),
                        pltpu.VMEM((IDX_HEADS, kb, wq), F32)],
        compiler_params=_params(("parallel", "arbitrary")),
        name="sparse_attention",
    )(q, qidx, widx_t, slope_rows, tri, k_blk, vt_blk, ki_blk)


META_E, META_W, META_R = 0, 2, 4


def _router_kernel(x_ref, shift_ref, scale_ref, g_ref, whi_ref, wlo_ref, br_ref, tri_ref,
                   h_ref, meta_ref, cnt_ref, carry_ref):
    @pl.when(pl.program_id(0) == 0)
    def _():
        carry_ref[...] = jnp.zeros_like(carry_ref)

    def store(rows, y):
        h_ref[rows, :] = y
    _modulate_rows(x_ref, shift_ref, scale_ref, g_ref, store)

    h = h_ref[...]
    h_hi = h.astype(BF16)
    h_lo = (h - h_hi.astype(F32)).astype(BF16)
    logits = (_dot(h_hi, whi_ref[...]) + (_dot(h_hi, wlo_ref[...]) + _dot(h_lo, whi_ref[...]))) + br_ref[...]
    tm = logits.shape[0]
    lane = lax.broadcasted_iota(jnp.int32, (tm, LANES), 1).astype(F32)
    neg_inf = jnp.float32(-jnp.inf)

    def first_argmax(v):
        top = jnp.max(v, axis=-1, keepdims=True)
        return top, jnp.min(jnp.where(v == top, lane, float(LANES)), axis=-1, keepdims=True)

    is_group = lane < N_GROUPS
    gl = jnp.where(is_group, logits, neg_inf)
    g_top, g_sel = first_argmax(gl)
    g_w = 1.0 / jnp.sum(jnp.where(is_group, jnp.exp(gl - g_top), 0.0), axis=-1, keepdims=True)
    first = N_GROUPS + g_sel * EXPERTS_PER_GROUP
    el = jnp.where(jnp.logical_and(lane >= first, lane < first + EXPERTS_PER_GROUP), logits, neg_inf)
    v1, i1 = first_argmax(el)
    v2, i2 = first_argmax(jnp.where(lane == i1, neg_inf, el))
    e21 = jnp.exp(v2 - v1)
    w1 = g_w / (1.0 + e21)
    w2 = g_w * e21 / (1.0 + e21)
    e1 = i1 - N_GROUPS
    e2 = i2 - N_GROUPS

    oh1 = jnp.where(lane == e1, 1.0, 0.0)
    oh2 = jnp.where(lane == e2, 1.0, 0.0)
    both = oh1 + oh2
    before = _dot(tri_ref[...], both.astype(BF16)) + carry_ref[...]
    r1 = jnp.sum(before * oh1, axis=-1, keepdims=True)
    r2 = jnp.sum(before * oh2, axis=-1, keepdims=True)
    carry_ref[...] += jnp.sum(both, axis=0, keepdims=True)

    meta = jnp.zeros((tm, LANES), F32)
    for ln, val in ((META_E, e1), (META_E + 1, e2), (META_W, w1), (META_W + 1, w2),
                    (META_R, r1), (META_R + 1, r2)):
        meta = jnp.where(lane == ln, val, meta)
    meta_ref[...] = meta
    cnt_ref[...] = carry_ref[...].astype(cnt_ref.dtype)


def _router(x, shift, scale, g, w_router, b_router, tm):
    t, d = x.shape
    nb = tm // ADA_BLOCK
    tri = (jnp.arange(tm)[:, None] > jnp.arange(tm)[None, :]).astype(BF16)
    w_hi = w_router.astype(BF16)
    w_lo = (w_router - w_hi.astype(F32)).astype(BF16)
    return pl.pallas_call(
        _router_kernel,
        grid=(t // tm,),
        in_specs=[pl.BlockSpec((tm, d), lambda i: (i, 0)),
                  pl.BlockSpec((nb, d), lambda i: (i, 0)),
                  pl.BlockSpec((nb, d), lambda i: (i, 0)),
                  pl.BlockSpec((1, d), lambda i: (0, 0)),
                  pl.BlockSpec((d, LANES), lambda i: (0, 0)),
                  pl.BlockSpec((d, LANES), lambda i: (0, 0)),
                  pl.BlockSpec((1, LANES), lambda i: (0, 0)),
                  pl.BlockSpec((tm, tm), lambda i: (0, 0))],
        out_specs=[pl.BlockSpec((tm, d), lambda i: (i, 0)),
                   pl.BlockSpec((tm, LANES), lambda i: (i, 0)),
                   pl.BlockSpec((1, LANES), lambda i: (0, 0))],
        out_shape=[jax.ShapeDtypeStruct((t, d), F32),
                   jax.ShapeDtypeStruct((t, LANES), F32),
                   jax.ShapeDtypeStruct((1, LANES), jnp.int32)],
        scratch_shapes=[pltpu.VMEM((1, LANES), F32)],
        compiler_params=_params(("arbitrary",)),
        name="moe_router",
    )(x, shift, scale, g.reshape(1, d), w_hi, w_lo, b_router, tri)


def _row_copy(src_ref, dst_ref, sem, src_row, dst_row):
    return pltpu.make_async_copy(src_ref.at[pl.ds(src_row, 1)], dst_ref.at[pl.ds(dst_row, 1)], sem)


def _wait_rows(hbm_ref, sem, n):
    rows = hbm_ref.at[pl.ds(0, n)]
    pltpu.make_async_copy(rows, rows, sem).wait()


DMA_UNROLL = 8


def _dispatch_kernel(pos_ref, h_ref, xs_ref, sem):
    n = h_ref.shape[0]

    def start(r, carry):
        for k in range(2):
            _row_copy(h_ref, xs_ref, sem, r, pos_ref[0, 0, 2 * r + k]).start(priority=k)
        return carry
    lax.fori_loop(0, n, start, 0, unroll=DMA_UNROLL)
    _wait_rows(xs_ref, sem, 2 * n)


def _dispatch(h, pos, ch):
    t, d = h.shape
    return pl.pallas_call(
        _dispatch_kernel,
        grid=(t // ch,),
        in_specs=[pl.BlockSpec((1, 1, 2 * ch), lambda i: (i, 0, 0), memory_space=pltpu.SMEM),
                  pl.BlockSpec((ch, d), lambda i: (i, 0))],
        out_specs=pl.BlockSpec(memory_space=pl.ANY),
        out_shape=jax.ShapeDtypeStruct((2 * t, d), F32),
        scratch_shapes=[pltpu.SemaphoreType.DMA(())],
        compiler_params=_params(("arbitrary",)),
        name="moe_dispatch",
    )(pos.reshape(t // ch, 1, 2 * ch), h)


def _cast_rows(src_ref, dst_ref, rows):
    def body(c, carry):
        r = pl.ds(pl.multiple_of(c * rows, rows), rows)
        dst_ref[r, :] = src_ref[r, :].astype(dst_ref.dtype)
        return carry
    lax.fori_loop(0, src_ref.shape[0] // rows, body, 0)


def _expert_kernel(vt_ref, ve_ref, vlo_ref, vhi_ref, x_ref, wg_ref, wu_ref, wd_ref, o_ref,
                   wgb_ref, wub_ref, wdb_ref):
    v = pl.program_id(0)
    lo, hi = vlo_ref[v], vhi_ref[v]
    prev = jnp.maximum(v - 1, 0)

    @pl.when(jnp.logical_and(hi > lo, jnp.logical_or(v == 0, ve_ref[prev] != ve_ref[v])))
    def _():
        _cast_rows(wg_ref, wgb_ref, 64)
        _cast_rows(wu_ref, wub_ref, 64)
        _cast_rows(wd_ref, wdb_ref, 16)

    @pl.when(hi > lo)
    def _():
        x = x_ref[...].astype(BF16)
        a = _dot(x, wgb_ref[...])
        b = _dot(x, wub_ref[...])
        y = _dot((_silu(a) * b).astype(BF16), wdb_ref[...])
        row = lax.broadcasted_iota(jnp.int32, (x.shape[0], 1), 0)
        mine = jnp.logical_and(row >= lo, row < hi)
        first = jnp.logical_or(v == 0, vt_ref[prev] != vt_ref[v])

        @pl.when(first)
        def _():
            o_ref[...] = jnp.where(mine, y, 0.0)

        @pl.when(jnp.logical_not(first))
        def _():
            o_ref[...] = jnp.where(mine, y, o_ref[...])


def _expert_mlp(xs, visits, layer, wg, wu, wd):
    p, d = xs.shape
    de = wg.shape[3]
    tm = EXPERT_TILE
    n_visits = visits[0].shape[0]
    return pl.pallas_call(
        _expert_kernel,
        grid_spec=pltpu.PrefetchScalarGridSpec(
            num_scalar_prefetch=4,
            grid=(n_visits,),
            in_specs=[pl.BlockSpec((tm, d), lambda v, vt, ve, lo, hi: (vt[v], 0)),
                      pl.BlockSpec((None, None, d, de), lambda v, vt, ve, lo, hi: (layer, ve[v], 0, 0)),
                      pl.BlockSpec((None, None, d, de), lambda v, vt, ve, lo, hi: (layer, ve[v], 0, 0)),
                      pl.BlockSpec((None, None, de, d), lambda v, vt, ve, lo, hi: (layer, ve[v], 0, 0))],
            out_specs=pl.BlockSpec((tm, d), lambda v, vt, ve, lo, hi: (vt[v], 0)),
            scratch_shapes=[pltpu.VMEM((d, de), BF16), pltpu.VMEM((d, de), BF16), pltpu.VMEM((de, d), BF16)]),
        out_shape=jax.ShapeDtypeStruct((p, d), F32),
        compiler_params=_params(("arbitrary",)),
        name="moe_experts",
    )(*visits, xs, wg, wu, wd)


def _combine_kernel(n_first, pos_ref, npos_ref, x_ref, gate_ref, meta_ref, ys_ref, *refs):
    o_refs, (ybuf, sems) = refs[:-2], refs[-2:]
    i = pl.program_id(0)
    n = pl.num_programs(0)
    tm = x_ref.shape[0]

    def fetch(p_ref, slot):
        def body(r, carry):
            for k in range(2):
                _row_copy(ys_ref, ybuf.at[slot, k], sems.at[slot], p_ref[0, 0, 2 * r + k], r).start(priority=k)
            return carry
        lax.fori_loop(0, tm, body, 0, unroll=DMA_UNROLL)

    @pl.when(i == 0)
    def _():
        fetch(pos_ref, 0)

    @pl.when(i + 1 < n)
    def _():
        fetch(npos_ref, (i + 1) % 2)

    slot = i % 2

    _wait_rows(ys_ref, sems.at[slot], 2 * tm)

    def write(o_ref):
        def body(r, carry):
            rows = pl.ds(pl.multiple_of(r * ADA_BLOCK, ADA_BLOCK), ADA_BLOCK)
            meta = meta_ref[rows, :]
            y = (meta[:, META_W:META_W + 1] * ybuf[slot, 0, rows, :]
                 + meta[:, META_W + 1:META_W + 2] * ybuf[slot, 1, rows, :])
            o_ref[rows, :] = x_ref[rows, :] + gate_ref[pl.ds(r, 1), :] * y
            return carry
        lax.fori_loop(0, tm // ADA_BLOCK, body, 0)

    if n_first is None:
        write(o_refs[0])
    else:
        pl.when(i < n_first)(lambda: write(o_refs[0]))
        pl.when(i >= n_first)(lambda: write(o_refs[1]))


def _combine(x, gate, meta, ys, pos, tm, split_rows=None):
    t, d = x.shape
    nb = tm // ADA_BLOCK
    n = t // tm
    pos3 = pos.reshape(n, 1, 2 * tm)
    if split_rows is None:
        n_first = None
        out_specs = pl.BlockSpec((tm, d), lambda i: (i, 0))
        out_shape = jax.ShapeDtypeStruct((t, d), F32)
    else:
        n_first = split_rows // tm
        out_specs = [pl.BlockSpec((tm, d), lambda i: (jnp.minimum(i, n_first - 1), 0)),
                     pl.BlockSpec((tm, d), lambda i: (jnp.maximum(i - n_first, 0), 0))]
        out_shape = [jax.ShapeDtypeStruct((split_rows, d), F32), jax.ShapeDtypeStruct((t - split_rows, d), F32)]
    return pl.pallas_call(
        functools.partial(_combine_kernel, n_first),
        grid=(n,),
        in_specs=[pl.BlockSpec((1, 1, 2 * tm), lambda i: (i, 0, 0), memory_space=pltpu.SMEM),
                  pl.BlockSpec((1, 1, 2 * tm), lambda i: (jnp.minimum(i + 1, n - 1), 0, 0), memory_space=pltpu.SMEM),
                  pl.BlockSpec((tm, d), lambda i: (i, 0)),
                  pl.BlockSpec((nb, d), lambda i: (i, 0)),
                  pl.BlockSpec((tm, LANES), lambda i: (i, 0)),
                  pl.BlockSpec(memory_space=pl.ANY)],
        out_specs=out_specs,
        out_shape=out_shape,
        scratch_shapes=[pltpu.VMEM((2, 2, tm, d), F32), pltpu.SemaphoreType.DMA((2,))],
        compiler_params=_params(("arbitrary",)),
        name="moe_combine",
    )(pos3, pos3, x, gate, meta, ys)


def _plan_kernel(n_tiles, cnt_ref, start_ref, vt_ref, ve_ref, vlo_ref, vhi_ref):
    tile = EXPERT_TILE
    n_visits = vt_ref.shape[0]

    def per_expert(e, carry):
        run, v = carry
        c = cnt_ref[0, e]
        end = run + c
        start_ref[e] = run
        first_tile = run // tile
        n_vis = jnp.where(c > 0, (end - 1) // tile - first_tile + 1, 0)

        def per_visit(j, _):
            t = first_tile + j
            vt_ref[v + j] = t
            ve_ref[v + j] = e
            vlo_ref[v + j] = jnp.maximum(run - t * tile, 0)
            vhi_ref[v + j] = jnp.minimum(end - t * tile, tile)
            return 0
        lax.fori_loop(0, n_vis, per_visit, 0)
        return end, v + n_vis

    _, n_real = lax.fori_loop(0, N_EXPERTS, per_expert, (jnp.int32(0), jnp.int32(0)))

    def trailing(v, _):
        vt_ref[v] = n_tiles - 1
        ve_ref[v] = ve_ref[jnp.maximum(n_real - 1, 0)]
        vlo_ref[v] = 0
        vhi_ref[v] = 0
        return 0
    lax.fori_loop(n_real, n_visits, trailing, 0)


def _moe_plan(counts, n_tiles):
    n_visits = n_tiles + N_EXPERTS - 1
    smem = pl.BlockSpec(memory_space=pltpu.SMEM)
    vis = jax.ShapeDtypeStruct((n_visits,), jnp.int32)
    return pl.pallas_call(
        functools.partial(_plan_kernel, n_tiles),
        in_specs=[smem],
        out_specs=[smem] * 5,
        out_shape=[jax.ShapeDtypeStruct((N_EXPERTS,), jnp.int32), vis, vis, vis, vis],
        name="moe_plan",
    )(counts)


def _hier_moe(x, shift, scale, gate, g, w_group, b_group, w_er, b_er, layer, wg, wu, wd, tm, split_rows=None):
    t, d = x.shape
    w_router = jnp.concatenate([w_group, jnp.moveaxis(w_er, 0, 1).reshape(d, N_EXPERTS)], axis=1)
    b_router = jnp.concatenate([b_group, b_er.reshape(N_EXPERTS)])
    n_route = N_GROUPS + N_EXPERTS
    w_router = jnp.pad(w_router, ((0, 0), (0, LANES - n_route)))
    b_router = jnp.pad(b_router, (0, LANES - n_route)).reshape(1, LANES)
    h, meta, counts = _router(x, shift, scale, g, w_router, b_router, min(tm, ROUTER_TILE))

    starts, *visits = _moe_plan(counts, 2 * t // EXPERT_TILE)
    expert = meta[:, META_E:META_E + 2].astype(jnp.int32)
    rank = meta[:, META_R:META_R + 2].astype(jnp.int32)
    pos = jnp.sum(jnp.where(expert[:, :, None] == jnp.arange(N_EXPERTS), starts, 0), axis=-1) + rank

    xs = _dispatch(h, pos, min(tm, GATHER_CHUNK))
    ys = _expert_mlp(xs, visits, layer, wg, wu, wd)
    return _combine(x, gate, meta, ys, pos, min(tm, COMBINE_TILE), split_rows)


def _cached_keys_kernel(past, seq, ck_ref, cv_ref, cki_ref, kn_ref, vn_ref, kin_ref,
                        k_ref, vt_ref, ki_ref, ks_ref, vs_ref, kis_ref):
    nkb, kb = k_ref.shape[0], k_ref.shape[1]
    n_keys = past + seq
    for cache_ref, new_ref, stage in ((ck_ref, kn_ref, ks_ref), (cv_ref, vn_ref, vs_ref), (cki_ref, kin_ref, kis_ref)):
        stage[0:past, :] = cache_ref[...]
        stage[past:n_keys, :] = new_ref[...]
        if nkb * kb > n_keys:
            stage[n_keys:, :] = jnp.zeros((nkb * kb - n_keys, stage.shape[1]), F32)
    for i in range(nkb):
        rows = slice(i * kb, (i + 1) * kb)
        k_ref[i] = ks_ref[rows, :].astype(k_ref.dtype)
        vt_ref[i] = vs_ref[rows, :].T.astype(vt_ref.dtype)
        ki_ref[i] = kis_ref[rows, :].astype(ki_ref.dtype)


def _cached_key_blocks(cache_k, cache_v, cache_ki, k_new, v_new, ki_new, kb):
    b, past, d_kv = cache_k.shape
    seq = k_new.shape[1]
    nkb = -(-(past + seq) // kb)
    per_stream = lambda a: pl.BlockSpec((None,) + a.shape[1:], lambda i: (i, 0, 0))
    blocked = lambda r, c: pl.BlockSpec((None, nkb, r, c), lambda i: (i, 0, 0, 0))
    args = (cache_k, cache_v, cache_ki, k_new, v_new, ki_new)
    return pl.pallas_call(
        functools.partial(_cached_keys_kernel, past, seq),
        grid=(b,),
        in_specs=[per_stream(a) for a in args],
        out_specs=[blocked(kb, d_kv), blocked(d_kv, kb), blocked(kb, IDX_DIM)],
        out_shape=[jax.ShapeDtypeStruct((b, nkb, kb, d_kv), BF16),
                   jax.ShapeDtypeStruct((b, nkb, d_kv, kb), BF16),
                   jax.ShapeDtypeStruct((b, nkb, kb, IDX_DIM), BF16)],
        scratch_shapes=[pltpu.VMEM((nkb * kb, d_kv), F32), pltpu.VMEM((nkb * kb, d_kv), F32),
                        pltpu.VMEM((nkb * kb, IDX_DIM), F32)],
        compiler_params=_params(("parallel",)),
        name="cached_key_blocks",
    )(*args)


def _key_blocks(k, vt_src, ki, n_pad, kb):
    b, s, _ = k.shape
    padk = lambda a: jnp.pad(a.astype(BF16), ((0, 0), (0, n_pad - s), (0, 0)))
    nkb = n_pad // kb
    k_blk = padk(k).reshape(b, nkb, kb, k.shape[2])
    vt_blk = jnp.swapaxes(padk(vt_src).reshape(b, nkb, kb, vt_src.shape[2]), 2, 3)
    ki_blk = padk(ki).reshape(b, nkb, kb, ki.shape[2])
    return k_blk, vt_blk, ki_blk


def kernel(x_prompt, x_sample, c_prompt, c_sample, cache_conv, state_ret, cache_k, cache_v, cache_kidx, norm_mix_g, norm_ffn_g, w_ada, b_ada, cr_w_in, conv_w, conv_b, conv_norm_g, conv_norm_b, ret_norm_g, cr_w_out, dsa_w_in, q_norm_g, k_norm_g, kidx_norm_g, dsa_w_out, moe_w_group, moe_b_group, moe_w_erouter, moe_b_erouter, moe_w_gate, moe_w_up, moe_w_down):
    bp, lp, d = x_prompt.shape
    bs, ls, _ = x_sample.shape
    tp, ts = bp * lp, bs * ls
    t = tp + ts
    depth = w_ada.shape[0]
    past = cache_k.shape[2]
    d_conv = conv_w.shape[2]
    d_ret = ret_norm_g.shape[1]
    d_q = dsa_w_out.shape[1]
    d_kv = cache_k.shape[3] * cache_k.shape[4]
    tm = _pick(math.gcd(tp, ts), (1024, 512, 256, 128))
    groups = ((0, bp, lp), (tp, bs, ls))

    x = jnp.concatenate([x_prompt.reshape(tp, d), x_sample.reshape(ts, d)], axis=0)

    c_all = jnp.concatenate([c_prompt, c_sample], axis=0)
    n_c = c_all.shape[0]
    c_all = jnp.pad(c_all, ((0, -n_c % 8), (0, 0)))
    ada = _ada(c_all, w_ada, b_ada)
    per_block = lambda a, n: jnp.broadcast_to(a[:, :, None, :], a.shape[:2] + (n, a.shape[2])).reshape(depth, -1, a.shape[2])
    ada_blk = jnp.concatenate([per_block(ada[:, :bp], lp // ADA_BLOCK),
                               per_block(ada[:, bp:bp + bs], ls // ADA_BLOCK)], axis=1)

    new_conv, new_ret, new_k, new_v, new_kidx = [], [], [], [], []
    for i in range(depth):
        sh1, sc1, g1, sh2, sc2, g2 = [ada_blk[i, :, m * d:(m + 1) * d] for m in range(6)]
        j = i // 2
        if i % 2 == 0:
            u = _modulated_matmul(x, sh1, sc1, norm_mix_g[i], cr_w_in[j].astype(BF16), tm, "cr_in_proj")
            a_out, b_out, bufs, states = [], [], [], []
            for gi, (row0, batch, seq) in enumerate(groups):
                buf0 = jnp.zeros((batch, CONV_WIDTH - 1, d_conv), F32) if gi == 0 else cache_conv[j]
                st0 = jnp.zeros((batch,) + state_ret.shape[2:], F32) if gi == 0 else state_ret[j]
                a, nbuf = _conv_branch(u, row0, batch, seq, buf0, conv_w[j], conv_b[j],
                                       conv_norm_g[j], conv_norm_b[j], d_conv)
                bo, nst = _retention_branch(u, row0, batch, seq, st0, ret_norm_g[j], 2 * d_conv // d_ret)
                a_out.append(a); b_out.append(bo); bufs.append(nbuf); states.append(nst)
            new_conv.append(bufs)
            new_ret.append(states)
            w_out = cr_w_out[j].astype(BF16)
            x = _outproj_residual([a_out, b_out], [w_out[:d_conv], w_out[d_conv:]], x, g1, tm, "cr_out_proj")
        else:
            w_in = dsa_w_in[j].astype(BF16)
            n_main = d_q + 2 * d_kv + IDX_HEADS * IDX_DIM
            w_x = jnp.pad(w_in[:, n_main:], ((0, 0), (0, LANES - (w_in.shape[1] - n_main))))
            u, ux = _modulated_matmul(x, sh1, sc1, norm_mix_g[i], w_in[:, :n_main], tm, "dsa_in_proj", w_extra=w_x)
            q, k, qidx, kidx, k_bf, ki_bf, vt_bf = _qk_norms(u, ux, q_norm_g[j], k_norm_g[j], kidx_norm_g[j],
                                                             tm, d_q, d_kv)
            v = u[:, d_q + d_kv:d_q + 2 * d_kv]
            widx = ux[:, IDX_DIM:IDX_DIM + IDX_HEADS]
            outs, ks, vs, kis = [], [], [], []
            for gi, (row0, batch, seq) in enumerate(groups):
                rows = slice(row0, row0 + batch * seq)
                kg = k[rows].reshape(batch, seq, d_kv)
                vg = v[rows].reshape(batch, seq, d_kv)
                kig = kidx[rows].reshape(batch, seq, IDX_DIM)
                ks.append(kg); vs.append(vg); kis.append(kig)
                g_past = 0 if gi == 0 else past
                n_keys = g_past + seq
                kb = 256 if n_keys >= 256 else 128
                n_pad = -(-n_keys // kb) * kb
                if g_past:
                    k_blk, vt_blk, ki_blk = _cached_key_blocks(
                        cache_k[j].reshape(batch, past, d_kv), cache_v[j].reshape(batch, past, d_kv),
                        cache_kidx[j], kg, vg, kig, kb)
                elif n_pad == n_keys and vt_bf.shape[2] == kb:
                    blocks = slice(row0 // kb, (row0 + batch * seq) // kb)
                    k_blk = k_bf[rows].reshape(batch, seq // kb, kb, d_kv)
                    vt_blk = vt_bf[blocks].reshape(batch, seq // kb, d_kv, kb)
                    ki_blk = ki_bf[rows].reshape(batch, seq // kb, kb, IDX_DIM)
                else:
                    k_blk, vt_blk, ki_blk = _key_blocks(kg, vg, kig, n_pad, kb)
                outs.append(_sparse_attention(q, qidx, widx[rows], k_blk, vt_blk, ki_blk,
                                              row0, batch, seq, g_past, n_keys, kb))
            new_k.append(ks); new_v.append(vs); new_kidx.append(kis)
            x = _outproj_residual([outs], [dsa_w_out[j].astype(BF16)], x, g1, tm, "dsa_out_proj")
        x = _hier_moe(x, sh2, sc2, g2, norm_ffn_g[i], moe_w_group[i], moe_b_group[i], moe_w_erouter[i],
                      moe_b_erouter[i], i, moe_w_gate, moe_w_up, moe_w_down, tm,
                      split_rows=tp if i == depth - 1 else None)

    y_p, y_s = x
    kv_heads, hd = cache_k.shape[3], cache_k.shape[4]
    stack = lambda per_layer, gi, shape: jnp.stack([lay[gi].reshape(shape) for lay in per_layer])
    return (y_p.reshape(bp, lp, d), y_s.reshape(bs, ls, d),
            stack(new_conv, 0, (bp, CONV_WIDTH - 1, d_conv)), stack(new_conv, 1, (bs, CONV_WIDTH - 1, d_conv)),
            stack(new_ret, 0, (bp,) + state_ret.shape[2:]), stack(new_ret, 1, (bs,) + state_ret.shape[2:]),
            stack(new_k, 0, (bp, lp, kv_heads, hd)), stack(new_k, 1, (bs, ls, kv_heads, hd)),
            stack(new_v, 0, (bp, lp, kv_heads, hd)), stack(new_v, 1, (bs, ls, kv_heads, hd)),
            stack(new_kidx, 0, (bp, lp, IDX_DIM)), stack(new_kidx, 1, (bs, ls, IDX_DIM)))
```

```python
import functools
import math

import jax
import jax.numpy as jnp
from jax import lax
from jax.experimental import pallas as pl
from jax.experimental.pallas import tpu as pltpu

F32 = jnp.float32
BF16 = jnp.bfloat16

EPS = 1e-6
CHUNK = 64
ADA_BLOCK = 32
CONV_WIDTH = 31
RET_HEADS = 8
RET_DK = 128
ATT_HEADS = 16
ATT_KV_HEADS = 4
IDX_HEADS = 16
IDX_DIM = 64
TOPK_MAX = 256
N_GROUPS = 4
EXPERTS_PER_GROUP = 8
N_EXPERTS = N_GROUPS * EXPERTS_PER_GROUP
LANES = 128
VMEM_LIMIT = 56 * 1024 * 1024
BISECT_ITERS = 30
BISECT_ROUND = 5
MASKED_DIST = 1e30
LOG2E = math.log2(math.e)
EXPERT_TILE = 256
GATHER_CHUNK = 1024
ROUTER_TILE = 512
COMBINE_TILE = 256


def _params(sem, vmem=VMEM_LIMIT):
    return pltpu.CompilerParams(dimension_semantics=sem, vmem_limit_bytes=vmem)


def _pick(n, cands):
    for c in cands:
        if n % c == 0:
            return c
    raise ValueError(f"no tile in {cands} divides {n}")


def _dot(a, b):
    return jnp.dot(a, b, preferred_element_type=F32)


def _dot_nt(a, b):
    return lax.dot_general(a, b, (((1,), (1,)), ((), ())), preferred_element_type=F32)


def _dot_tn(a, b):
    return lax.dot_general(a, b, (((0,), (0,)), ((), ())), preferred_element_type=F32)


def _silu(x):
    return x * jax.nn.sigmoid(x)


def _ada_kernel(c_ref, w_ref, b_ref, o_ref):
    c = _silu(c_ref[...]).astype(BF16)
    o_ref[...] = _dot(c, w_ref[...].astype(BF16)) + b_ref[...]


def _ada(c_all, w_ada, b_ada):
    depth, d, n = w_ada.shape
    rows = c_all.shape[0]
    tn = _pick(n, (1024, 512, 256, 128))
    return pl.pallas_call(
        _ada_kernel,
        grid=(depth, n // tn),
        in_specs=[pl.BlockSpec((rows, d), lambda l, j: (0, 0)),
                  pl.BlockSpec((None, d, tn), lambda l, j: (l, 0, j)),
                  pl.BlockSpec((None, 1, tn), lambda l, j: (l, 0, j))],
        out_specs=pl.BlockSpec((None, rows, tn), lambda l, j: (l, 0, j)),
        out_shape=jax.ShapeDtypeStruct((depth, rows, n), F32),
        compiler_params=_params(("parallel", "parallel")),
        name="ada",
    )(c_all, w_ada, b_ada.reshape(depth, 1, n))


def _modulate_rows(x_ref, shift_ref, scale_ref, g_ref, store):
    nblk = x_ref.shape[0] // ADA_BLOCK

    def body(r, carry):
        rows = pl.ds(pl.multiple_of(r * ADA_BLOCK, ADA_BLOCK), ADA_BLOCK)
        x = x_ref[rows, :]
        y = x * lax.rsqrt(jnp.mean(x * x, axis=-1, keepdims=True) + EPS) * g_ref[...]
        y = y * (1.0 + scale_ref[pl.ds(r, 1), :]) + shift_ref[pl.ds(r, 1), :]
        store(rows, y)
        return carry

    lax.fori_loop(0, nblk, body, 0)


def _modmm_kernel(has_extra, x_ref, shift_ref, scale_ref, g_ref, w_ref, *refs):
    if has_extra:
        wx_ref, o_ref, ox_ref, h_ref = refs
    else:
        o_ref, h_ref = refs

    @pl.when(pl.program_id(1) == 0)
    def _():
        def store(rows, y):
            h_ref[rows, :] = y.astype(BF16)
        _modulate_rows(x_ref, shift_ref, scale_ref, g_ref, store)
        if has_extra:
            ox_ref[...] = _dot(h_ref[...], wx_ref[...])

    o_ref[...] = _dot(h_ref[...], w_ref[...])


def _premod_mm_kernel(has_extra, h_ref, w_ref, *refs):
    if has_extra:
        wx_ref, o_ref, ox_ref = refs

        @pl.when(pl.program_id(1) == 0)
        def _():
            ox_ref[...] = _dot(h_ref[...], wx_ref[...])
    else:
        (o_ref,) = refs
    o_ref[...] = _dot(h_ref[...], w_ref[...])


def _premodulated_matmul(h, w, tm, name, w_extra=None):
    t, d = h.shape
    n = w.shape[1]
    tn = _pick(n, (1024, 512, 256, 128))
    in_specs = [pl.BlockSpec((tm, d), lambda i, j: (i, 0)), pl.BlockSpec((d, tn), lambda i, j: (0, j))]
    out_specs = pl.BlockSpec((tm, tn), lambda i, j: (i, j))
    out_shape = jax.ShapeDtypeStruct((t, n), F32)
    args = [h, w]
    if w_extra is not None:
        nx = w_extra.shape[1]
        in_specs.append(pl.BlockSpec((d, nx), lambda i, j: (0, 0)))
        out_specs = [out_specs, pl.BlockSpec((tm, nx), lambda i, j: (i, 0))]
        out_shape = [out_shape, jax.ShapeDtypeStruct((t, nx), F32)]
        args.append(w_extra)
    return pl.pallas_call(
        functools.partial(_premod_mm_kernel, w_extra is not None),
        grid=(t // tm, n // tn),
        in_specs=in_specs,
        out_specs=out_specs,
        out_shape=out_shape,
        compiler_params=_params(("parallel", "arbitrary")),
        name=name,
    )(*args)


def _modulated_matmul(x, shift, scale, g, w, tm, name, w_extra=None):
    t, d = x.shape
    n = w.shape[1]
    tn = _pick(n, (1024, 512, 256, 128))
    nb = tm // ADA_BLOCK
    in_specs = [pl.BlockSpec((tm, d), lambda i, j: (i, 0)),
                pl.BlockSpec((nb, d), lambda i, j: (i, 0)),
                pl.BlockSpec((nb, d), lambda i, j: (i, 0)),
                pl.BlockSpec((1, d), lambda i, j: (0, 0)),
                pl.BlockSpec((d, tn), lambda i, j: (0, j))]
    out_specs = pl.BlockSpec((tm, tn), lambda i, j: (i, j))
    out_shape = jax.ShapeDtypeStruct((t, n), F32)
    args = [x, shift, scale, g.reshape(1, d), w]
    if w_extra is not None:
        nx = w_extra.shape[1]
        in_specs.append(pl.BlockSpec((d, nx), lambda i, j: (0, 0)))
        out_specs = [out_specs, pl.BlockSpec((tm, nx), lambda i, j: (i, 0))]
        out_shape = [out_shape, jax.ShapeDtypeStruct((t, nx), F32)]
        args.append(w_extra)
    return pl.pallas_call(
        functools.partial(_modmm_kernel, w_extra is not None),
        grid=(t // tm, n // tn),
        in_specs=in_specs,
        out_specs=out_specs,
        out_shape=out_shape,
        scratch_shapes=[pltpu.VMEM((tm, d), BF16)],
        compiler_params=_params(("parallel", "arbitrary")),
        name=name,
    )(*args)


def _outproj_kernel(n_in, tile_ranges, *refs):
    n_grp = len(tile_ranges)
    a_refs = [refs[p * n_grp:(p + 1) * n_grp] for p in range(n_in)]
    w_refs = refs[n_in * n_grp:n_in * n_grp + n_in]
    x_ref, gate_ref, o_ref, y_ref = refs[n_in * n_grp + n_in:]
    i = pl.program_id(0)
    for g, (lo, hi) in enumerate(tile_ranges):
        @pl.when(jnp.logical_and(i >= lo, i < hi))
        def _():
            y = _dot(a_refs[0][g][...], w_refs[0][...])
            for p in range(1, n_in):
                y = y + _dot(a_refs[p][g][...], w_refs[p][...])
            y_ref[...] = y
    nblk = x_ref.shape[0] // ADA_BLOCK

    def body(r, carry):
        rows = pl.ds(pl.multiple_of(r * ADA_BLOCK, ADA_BLOCK), ADA_BLOCK)
        o_ref[rows, :] = x_ref[rows, :] + gate_ref[pl.ds(r, 1), :] * y_ref[rows, :]
        return carry

    lax.fori_loop(0, nblk, body, 0)


def _outproj_residual(acts, ws, x, gate, tm, name):
    t, d = x.shape
    tn = _pick(d, (1024, 512, 256, 128))
    nb = tm // ADA_BLOCK
    n_in = len(acts)
    tile_ranges, lo = [], 0
    for a in acts[0]:
        tile_ranges.append((lo, lo + a.shape[0] // tm))
        lo = tile_ranges[-1][1]

    def group_spec(a, lo, hi):
        return pl.BlockSpec((tm, a.shape[1]), lambda i, j: (jnp.clip(i - lo, 0, hi - lo - 1), 0))

    in_specs = ([group_spec(a, *tile_ranges[g]) for piece in acts for g, a in enumerate(piece)]
                + [pl.BlockSpec((w.shape[0], tn), lambda i, j: (0, j)) for w in ws]
                + [pl.BlockSpec((tm, tn), lambda i, j: (i, j)),
                   pl.BlockSpec((nb, tn), lambda i, j: (i, j))])
    return pl.pallas_call(
        functools.partial(_outproj_kernel, n_in, tuple(tile_ranges)),
        grid=(t // tm, d // tn),
        in_specs=in_specs,
        out_specs=pl.BlockSpec((tm, tn), lambda i, j: (i, j)),
        out_shape=jax.ShapeDtypeStruct((t, d), F32),
        scratch_shapes=[pltpu.VMEM((tm, tn), F32)],
        compiler_params=_params(("parallel", "parallel")),
        name=name,
    )(*[a for piece in acts for a in piece], *ws, x, gate)


CONV_ROWS = 32
CONV_HIST = 32


def _conv_kernel(tl, val_ref, gate_ref, pval_ref, pgate_ref, buf_ref, w_ref, b_ref, ng_ref, nb_ref,
                 o_ref, nbuf_ref, up_ref, sh_ref):
    li = pl.program_id(1)
    hist = CONV_WIDTH - 1
    pad = CONV_HIST - hist
    glu = val_ref[...] * jax.nn.sigmoid(gate_ref[...])
    up_ref[CONV_HIST:CONV_HIST + tl, :] = glu
    up_ref[0:pad, :] = jnp.zeros((pad, up_ref.shape[1]), F32)

    @pl.when(li == 0)
    def _():
        up_ref[pad:CONV_HIST, :] = buf_ref[...]

    @pl.when(li > 0)
    def _():
        prev = pval_ref[...] * jax.nn.sigmoid(pgate_ref[...])
        up_ref[pad:CONV_HIST, :] = prev[pad:, :]

    n_sh = tl + CONV_HIST - 8
    for s in range(1, 8):
        sh_ref[s - 1, 0:n_sh, :] = up_ref[s:s + n_sh, :]

    def tap_rows(row):
        base, s = row - row % 8, row % 8
        src = up_ref if s == 0 else sh_ref.at[s - 1]
        return src[base:base + CONV_ROWS, :]

    for c in range(tl // CONV_ROWS):
        r0 = c * CONV_ROWS
        acc = jnp.zeros((CONV_ROWS, val_ref.shape[1]), F32)
        for j in range(CONV_WIDTH):
            acc = acc + tap_rows(r0 + pad + j) * w_ref[j:j + 1, :]
        acc = acc + b_ref[...]
        mu = jnp.mean(acc, axis=-1, keepdims=True)
        dlt = acc - mu
        y = dlt * lax.rsqrt(jnp.mean(dlt * dlt, axis=-1, keepdims=True) + EPS)
        y = y * ng_ref[...] + nb_ref[...]
        o_ref[r0:r0 + CONV_ROWS, :] = _silu(y).astype(o_ref.dtype)

    @pl.when(li == pl.num_programs(1) - 1)
    def _():
        nbuf_ref[...] = up_ref[CONV_HIST + tl - hist:CONV_HIST + tl, :]


def _conv_branch(u, row0, batch, seq, conv_buf, conv_w, conv_b, cn_g, cn_b, d_conv):
    tl = _pick(seq, (128, 64, 32))
    nl = seq // tl
    hist = CONV_WIDTH - 1
    rb = row0 // tl
    pb = tl // CONV_HIST
    cur = lambda col: pl.BlockSpec((tl, d_conv), lambda b, l: (rb + b * nl + l, col))
    prev = lambda col: pl.BlockSpec(
        (CONV_HIST, d_conv), lambda b, l: (jnp.maximum((rb + b * nl + l) * pb - 1, 0), col))
    vec = pl.BlockSpec((1, d_conv), lambda b, l: (0, 0))
    return pl.pallas_call(
        functools.partial(_conv_kernel, tl),
        grid=(batch, nl),
        in_specs=[cur(0), cur(1), prev(0), prev(1),
                  pl.BlockSpec((None, hist, d_conv), lambda b, l: (b, 0, 0)),
                  pl.BlockSpec((CONV_WIDTH, d_conv), lambda b, l: (0, 0)),
                  vec, vec, vec],
        out_specs=[pl.BlockSpec((tl, d_conv), lambda b, l: (b * nl + l, 0)),
                   pl.BlockSpec((None, hist, d_conv), lambda b, l: (b, 0, 0))],
        out_shape=[jax.ShapeDtypeStruct((batch * seq, d_conv), BF16),
                   jax.ShapeDtypeStruct((batch, hist, d_conv), F32)],
        scratch_shapes=[pltpu.VMEM((CONV_HIST + tl, d_conv), F32),
                        pltpu.VMEM((7, CONV_HIST + tl, d_conv), F32)],
        compiler_params=_params(("parallel", "arbitrary")),
        name="conv_branch",
    )(u, u, u, u, conv_buf, conv_w, conv_b.reshape(1, -1), cn_g.reshape(1, -1), cn_b.reshape(1, -1))


def _retention_kernel(q_ref, k_ref, v_ref, g_ref, s0_ref, din_ref, dq_ref, dk_ref, dblk_ref, rg_ref,
                      o_ref, s_out_ref, s_ref):
    ci = pl.program_id(1)

    @pl.when(ci == 0)
    def _():
        s_ref[...] = s0_ref[...]

    dv = s_ref.shape[2]
    for h in range(RET_HEADS):
        cols = slice(h * dv, (h + 1) * dv)
        q = q_ref[:, cols].astype(BF16)
        k = k_ref[:, cols] * (RET_DK ** -0.5)
        v = v_ref[:, cols].astype(BF16)
        s_prev = s_ref[h]
        sc = _dot_nt(q, k.astype(BF16)) * din_ref[h]
        o = _dot(sc.astype(BF16), v) + _dot(q, s_prev.astype(BF16)) * dq_ref[h]
        kd = (k * dk_ref[h]).astype(BF16)
        s_ref[h] = s_prev * dblk_ref[h] + _dot_tn(kd, v)
        o = o * lax.rsqrt(jnp.mean(o * o, axis=-1, keepdims=True) + EPS) * rg_ref[:, cols]
        o_ref[:, cols] = (o * _silu(g_ref[:, cols])).astype(o_ref.dtype)

    @pl.when(ci == pl.num_programs(1) - 1)
    def _():
        s_out_ref[...] = s_ref[...]


def _retention_branch(u, row0, batch, seq, state0, ret_g, col_q):
    heads, dk, dv = state0.shape[1:]
    d_ret = heads * dv
    c = _pick(seq, (256, 128, 64, 32))
    nc = seq // c
    rb = row0 // c
    lg = jnp.log1p(-(2.0 ** (-5.0 - jnp.arange(heads, dtype=F32))))
    pos = jnp.arange(c, dtype=F32)
    diff = pos[:, None] - pos[None, :]
    d_in = jnp.where(diff >= 0, jnp.exp(lg[:, None, None] * jnp.maximum(diff, 0.0)), 0.0)
    d_q = jnp.broadcast_to(jnp.exp(lg[:, None] * (pos[None, :] + 1.0))[:, :, None], (heads, c, dv))
    d_k = jnp.broadcast_to(jnp.exp(lg[:, None] * (c - 1.0 - pos[None, :]))[:, :, None], (heads, c, dk))
    d_blk = jnp.broadcast_to(jnp.exp(lg * c)[:, None, None], (heads, dk, dv))
    blk = lambda col: pl.BlockSpec((c, d_ret), lambda b, i: (rb + b * nc + i, col))
    const3 = lambda shape: pl.BlockSpec(shape, lambda b, i: (0, 0, 0))
    return pl.pallas_call(
        _retention_kernel,
        grid=(batch, nc),
        in_specs=[blk(col_q), blk(col_q + 1), blk(col_q + 2), blk(col_q + 3),
                  pl.BlockSpec((None, heads, dk, dv), lambda b, i: (b, 0, 0, 0)),
                  const3((heads, c, c)), const3((heads, c, dv)), const3((heads, c, dk)),
                  const3((heads, dk, dv)),
                  pl.BlockSpec((1, d_ret), lambda b, i: (0, 0))],
        out_specs=[pl.BlockSpec((c, d_ret), lambda b, i: (b * nc + i, 0)),
                   pl.BlockSpec((None, heads, dk, dv), lambda b, i: (b, 0, 0, 0))],
        out_shape=[jax.ShapeDtypeStruct((batch * seq, d_ret), BF16),
                   jax.ShapeDtypeStruct(state0.shape, F32)],
        scratch_shapes=[pltpu.VMEM((heads, dk, dv), F32)],
        compiler_params=_params(("parallel", "arbitrary")),
        name="retention_branch",
    )(u, u, u, u, state0, d_in, d_q, d_k, d_blk, ret_g.reshape(1, d_ret))


def _head_rms(x, g, hd):
    outs = []
    for h in range(x.shape[1] // hd):
        xh = x[:, h * hd:(h + 1) * hd]
        outs.append(xh * lax.rsqrt(jnp.mean(xh * xh, axis=-1, keepdims=True) + EPS) * g)
    return outs


def _qknorm_kernel(q_ref, k_ref, v_ref, qi_ref, ki_ref, qg_ref, kg_ref, kig_ref,
                   qo_ref, ko_ref, qio_ref, kio_ref, kb_ref, kib_ref, vt_ref):
    hd = qg_ref.shape[1]
    qio_ref[...] = qi_ref[...].astype(qio_ref.dtype)
    q_scale = hd ** -0.5 * LOG2E
    for h, qh in enumerate(_head_rms(q_ref[...], qg_ref[...], hd)):
        qo_ref[:, h * hd:(h + 1) * hd] = (qh * q_scale).astype(qo_ref.dtype)
    for h, kh in enumerate(_head_rms(k_ref[...], kg_ref[...], hd)):
        ko_ref[:, h * hd:(h + 1) * hd] = kh
        kb_ref[:, h * hd:(h + 1) * hd] = kh.astype(kb_ref.dtype)
    ki = ki_ref[:, :IDX_DIM]
    ki = ki * lax.rsqrt(jnp.mean(ki * ki, axis=-1, keepdims=True) + EPS) * kig_ref[...]
    kio_ref[...] = ki
    kib_ref[...] = ki.astype(kib_ref.dtype)
    blk = vt_ref.shape[2]
    for c in range(vt_ref.shape[0]):
        vt_ref[c] = v_ref[c * blk:(c + 1) * blk, :].T.astype(vt_ref.dtype)


def _qk_norms(u, ux, q_g, k_g, kidx_g, tm, d_q, d_kv):
    t = u.shape[0]
    hd = q_g.shape[0]
    d_qi = IDX_HEADS * IDX_DIM
    blk = min(tm, 256)
    return pl.pallas_call(
        _qknorm_kernel,
        grid=(t // tm,),
        in_specs=[pl.BlockSpec((tm, d_q), lambda i: (i, 0)),
                  pl.BlockSpec((tm, d_kv), lambda i: (i, d_q // d_kv)),
                  pl.BlockSpec((tm, d_kv), lambda i: (i, d_q // d_kv + 1)),
                  pl.BlockSpec((tm, d_qi), lambda i: (i, (d_q + 2 * d_kv) // d_qi)),
                  pl.BlockSpec((tm, LANES), lambda i: (i, 0)),
                  pl.BlockSpec((1, hd), lambda i: (0, 0)),
                  pl.BlockSpec((1, hd), lambda i: (0, 0)),
                  pl.BlockSpec((1, IDX_DIM), lambda i: (0, 0))],
        out_specs=[pl.BlockSpec((tm, d_q), lambda i: (i, 0)),
                   pl.BlockSpec((tm, d_kv), lambda i: (i, 0)),
                   pl.BlockSpec((tm, d_qi), lambda i: (i, 0)),
                   pl.BlockSpec((tm, IDX_DIM), lambda i: (i, 0)),
                   pl.BlockSpec((tm, d_kv), lambda i: (i, 0)),
                   pl.BlockSpec((tm, IDX_DIM), lambda i: (i, 0)),
                   pl.BlockSpec((tm // blk, d_kv, blk), lambda i: (i, 0, 0))],
        out_shape=[jax.ShapeDtypeStruct((t, d_q), BF16),
                   jax.ShapeDtypeStruct((t, d_kv), F32),
                   jax.ShapeDtypeStruct((t, d_qi), BF16),
                   jax.ShapeDtypeStruct((t, IDX_DIM), F32),
                   jax.ShapeDtypeStruct((t, d_kv), BF16),
                   jax.ShapeDtypeStruct((t, IDX_DIM), BF16),
                   jax.ShapeDtypeStruct((t // blk, d_kv, blk), BF16)],
        compiler_params=_params(("parallel",)),
        name="qk_norms",
    )(u, u, u, u, ux, q_g.reshape(1, hd), k_g.reshape(1, hd), kidx_g.reshape(1, IDX_DIM))


def _dsa_kernel(tq, rep, kb, past, n_keys, n_sel, q_ref, qi_ref, wt_ref, slope_ref, tri_ref,
                k_ref, vt_ref, ki_ref, o_ref, sc_ref, dist_ref, qg_ref, acc_ref, z_ref, rel_ref):
    t0 = pl.program_id(1) * tq
    hd = k_ref.shape[2] // ATT_KV_HEADS
    grp = ATT_HEADS // ATT_KV_HEADS
    wq = rep * tq
    n_adm_tile = jnp.minimum(((past + t0 + tq - 1) // CHUNK + 1) * CHUNK, n_keys)
    nkb = (n_adm_tile + kb - 1) // kb
    q_pos = past + t0 + lax.broadcasted_iota(jnp.int32, (1, wq), 1) % tq
    q_chunk = q_pos // CHUNK
    n_adm = jnp.minimum((q_chunk + 1) * CHUNK, n_keys).astype(F32)
    need = jnp.minimum(n_adm, float(n_sel))
    neg_inf = jnp.float32(-jnp.inf)
    w_t = wt_ref[...] * (IDX_HEADS ** -0.5 * IDX_DIM ** -0.5)

    def key_pos(i):
        return i * kb + lax.broadcasted_iota(jnp.int32, (kb, 1), 0)

    def admissible(i):
        kp = key_pos(i)
        return jnp.logical_and(kp // CHUNK <= q_chunk, kp < n_keys)

    def score_body(i, carry):
        lo, hi = carry
        ki = ki_ref[i]
        for h in range(IDX_HEADS):
            qi = jnp.concatenate([qi_ref[:, h * IDX_DIM:(h + 1) * IDX_DIM]] * rep, axis=0)
            rel_ref[h] = _dot_nt(ki, qi)
        acc = jnp.zeros((kb, wq), F32)
        for h in range(IDX_HEADS):
            acc = acc + jnp.maximum(rel_ref[h], 0.0) * w_t[h:h + 1, :]
        adm = admissible(i)
        sc_ref[i] = jnp.where(adm, acc, neg_inf)
        lo = jnp.minimum(lo, jnp.min(jnp.where(adm, acc, jnp.inf), axis=0, keepdims=True))
        hi = jnp.maximum(hi, jnp.max(jnp.where(adm, acc, neg_inf), axis=0, keepdims=True))
        return lo, hi

    lo, hi = lax.fori_loop(0, nkb, score_body,
                           (jnp.full((1, wq), jnp.inf, F32), jnp.full((1, wq), neg_inf, F32)))

    def count(pred):
        def body(i, acc):
            ones = jnp.where(pred(sc_ref[i]), 1.0, 0.0)
            parts = [ones[r:r + 8, :] for r in range(0, kb, 8)]
            while len(parts) > 1:
                parts = [parts[a] + parts[a + 1] for a in range(0, len(parts), 2)]
            return acc + parts[0]
        return jnp.sum(lax.fori_loop(0, nkb, body, jnp.zeros((8, wq), F32)), axis=0, keepdims=True)

    def bisect(_, carry):
        lo, hi, c_lo = carry
        mid = 0.5 * (lo + hi)
        c_mid = count(lambda s: s >= mid)
        ge = c_mid >= need
        return jnp.where(ge, mid, lo), jnp.where(ge, hi, mid), jnp.where(ge, c_mid, c_lo)

    def bisect_round(carry):
        rnd, lo, hi, c_lo = carry
        lo, hi, c_lo = lax.fori_loop(0, BISECT_ROUND, bisect, (lo, hi, c_lo))
        return rnd + 1, lo, hi, c_lo

    def unresolved(carry):
        rnd, _, _, c_lo = carry
        return jnp.logical_and(rnd < BISECT_ITERS // BISECT_ROUND, jnp.max(c_lo - need) > 0.0)

    _, lo, hi, c_lo = lax.while_loop(unresolved, bisect_round, (jnp.int32(0), lo, hi, n_adm))

    def write_dist(i, sel):
        dist = jnp.abs(q_pos - key_pos(i)).astype(F32)
        dist_ref[i] = jnp.where(sel, dist, MASKED_DIST)

    resolved = jnp.max(c_lo - need) <= 0.0

    @pl.when(resolved)
    def _():
        def body(i, carry):
            write_dist(i, sc_ref[i] >= lo)
            return carry
        lax.fori_loop(0, nkb, body, 0)

    @pl.when(jnp.logical_not(resolved))
    def _():
        n_above = count(lambda s: s > hi)
        room = need - n_above

        def body(i, seen):
            s = sc_ref[i]
            above = s > hi
            band = jnp.logical_and(s >= lo, jnp.logical_not(above))
            band_f = jnp.where(band, 1.0, 0.0)
            rank = _dot(tri_ref[...], band_f.astype(BF16)) + seen
            write_dist(i, jnp.logical_or(above, jnp.logical_and(band, rank <= room)))
            return seen + jnp.sum(band_f, axis=0, keepdims=True)
        lax.fori_loop(0, nkb, body, jnp.zeros((1, wq), F32))

    for g in range(ATT_KV_HEADS):
        qg_ref[g] = jnp.concatenate(
            [q_ref[:, (g * grp + r) * hd:(g * grp + r + 1) * hd] for r in range(grp)], axis=0)
    acc_ref[...] = jnp.zeros_like(acc_ref)

    def att_body(i, carry):
        ms, ls = carry
        dist = jnp.concatenate([dist_ref[i]] * (grp // rep), axis=1)
        new_ms, new_ls = [], []
        for g in range(ATT_KV_HEADS):
            z_ref[g] = _dot_nt(k_ref[i, :, g * hd:(g + 1) * hd], qg_ref[g])
        for g in range(ATT_KV_HEADS):
            z = z_ref[g] - slope_ref[g] * dist
            m_new = jnp.maximum(ms[g], jnp.max(z, axis=0, keepdims=True))
            alpha = jnp.exp2(ms[g] - m_new)
            p = jnp.exp2(z - m_new)
            new_ls.append(ls[g] * alpha + jnp.sum(p, axis=0, keepdims=True))
            new_ms.append(m_new)
            acc_ref[g] = acc_ref[g] * alpha + _dot(vt_ref[i, g * hd:(g + 1) * hd, :], p.astype(BF16))
        return tuple(new_ms), tuple(new_ls)

    _, ls = lax.fori_loop(
        0, nkb, att_body,
        (tuple(jnp.full((1, grp * tq), neg_inf, F32) for _ in range(ATT_KV_HEADS)),
         tuple(jnp.zeros((1, grp * tq), F32) for _ in range(ATT_KV_HEADS))))
    for g in range(ATT_KV_HEADS):
        out = (acc_ref[g] / ls[g]).T
        for r in range(grp):
            o_ref[:, (g * grp + r) * hd:(g * grp + r + 1) * hd] = out[r * tq:(r + 1) * tq, :].astype(o_ref.dtype)


def _sparse_attention(q, qidx, widx, k_blk, vt_blk, ki_blk, row0, batch, seq, past, n_keys, kb):
    d_q = q.shape[1]
    nkb_all = k_blk.shape[1]
    d_kv = k_blk.shape[3]
    tq = _pick(seq, (256, 128, 64, 32))
    nq = seq // tq
    rb = row0 // tq
    grp = ATT_HEADS // ATT_KV_HEADS
    rep = max(1, LANES // tq)
    wq = rep * tq
    n_sel = min(TOPK_MAX, n_keys // 4)
    widx_t = jnp.tile(jnp.swapaxes(widx.reshape(batch * nq, tq, IDX_HEADS), 1, 2), (1, 1, rep))
    slopes = LOG2E * 2.0 ** (-8.0 * jnp.arange(1, ATT_HEADS + 1, dtype=F32) / ATT_HEADS)
    slope_rows = jnp.repeat(slopes.reshape(ATT_KV_HEADS, grp), tq, axis=1).reshape(ATT_KV_HEADS, 1, grp * tq)
    tri = (jnp.arange(kb)[:, None] >= jnp.arange(kb)[None, :]).astype(BF16)
    return pl.pallas_call(
        functools.partial(_dsa_kernel, tq, rep, kb, past, n_keys, n_sel),
        grid=(batch, nq),
        in_specs=[pl.BlockSpec((tq, d_q), lambda b, i: (rb + b * nq + i, 0)),
                  pl.BlockSpec((tq, qidx.shape[1]), lambda b, i: (rb + b * nq + i, 0)),
                  pl.BlockSpec((None, IDX_HEADS, wq), lambda b, i: (b * nq + i, 0, 0)),
                  pl.BlockSpec((ATT_KV_HEADS, 1, grp * tq), lambda b, i: (0, 0, 0)),
                  pl.BlockSpec((kb, kb), lambda b, i: (0, 0)),
                  pl.BlockSpec((None, nkb_all, kb, d_kv), lambda b, i: (b, 0, 0, 0)),
                  pl.BlockSpec((None, nkb_all, d_kv, kb), lambda b, i: (b, 0, 0, 0)),
                  pl.BlockSpec((None, nkb_all, kb, IDX_DIM), lambda b, i: (b, 0, 0, 0))],
        out_specs=pl.BlockSpec((tq, d_q), lambda b, i: (b * nq + i, 0)),
        out_shape=jax.ShapeDtypeStruct((batch * seq, d_q), BF16),
        scratch_shapes=[pltpu.VMEM((nkb_all, kb, wq), F32), pltpu.VMEM((nkb_all, kb, wq), F32),
                        pltpu.VMEM((ATT_KV_HEADS, grp * tq, d_kv // ATT_KV_HEADS), BF16),
                        pltpu.VMEM((ATT_KV_HEADS, d_kv // ATT_KV_HEADS, grp * tq), F32),
                        pltpu.VMEM((ATT_KV_HEADS, kb, grp * tq), F32),
                        pltpu.VMEM((IDX_HEADS, kb, wq), F32)],
        compiler_params=_params(("parallel", "arbitrary")),
        name="sparse_attention",
    )(q, qidx, widx_t, slope_rows, tri, k_blk, vt_blk, ki_blk)


META_E, META_W, META_R = 0, 2, 4


def _router_kernel(x_ref, shift_ref, scale_ref, g_ref, whi_ref, wlo_ref, br_ref, tri_ref,
                   h_ref, meta_ref, cnt_ref, carry_ref):
    @pl.when(pl.program_id(0) == 0)
    def _():
        carry_ref[...] = jnp.zeros_like(carry_ref)

    def store(rows, y):
        h_ref[rows, :] = y
    _modulate_rows(x_ref, shift_ref, scale_ref, g_ref, store)

    h = h_ref[...]
    h_hi = h.astype(BF16)
    h_lo = (h - h_hi.astype(F32)).astype(BF16)
    logits = (_dot(h_hi, whi_ref[...]) + (_dot(h_hi, wlo_ref[...]) + _dot(h_lo, whi_ref[...]))) + br_ref[...]
    tm = logits.shape[0]
    lane = lax.broadcasted_iota(jnp.int32, (tm, LANES), 1).astype(F32)
    neg_inf = jnp.float32(-jnp.inf)

    def first_argmax(v):
        top = jnp.max(v, axis=-1, keepdims=True)
        return top, jnp.min(jnp.where(v == top, lane, float(LANES)), axis=-1, keepdims=True)

    is_group = lane < N_GROUPS
    gl = jnp.where(is_group, logits, neg_inf)
    g_top, g_sel = first_argmax(gl)
    g_w = 1.0 / jnp.sum(jnp.where(is_group, jnp.exp(gl - g_top), 0.0), axis=-1, keepdims=True)
    first = N_GROUPS + g_sel * EXPERTS_PER_GROUP
    el = jnp.where(jnp.logical_and(lane >= first, lane < first + EXPERTS_PER_GROUP), logits, neg_inf)
    v1, i1 = first_argmax(el)
    v2, i2 = first_argmax(jnp.where(lane == i1, neg_inf, el))
    e21 = jnp.exp(v2 - v1)
    w1 = g_w / (1.0 + e21)
    w2 = g_w * e21 / (1.0 + e21)
    e1 = i1 - N_GROUPS
    e2 = i2 - N_GROUPS

    oh1 = jnp.where(lane == e1, 1.0, 0.0)
    oh2 = jnp.where(lane == e2, 1.0, 0.0)
    both = oh1 + oh2
    before = _dot(tri_ref[...], both.astype(BF16)) + carry_ref[...]
    r1 = jnp.sum(before * oh1, axis=-1, keepdims=True)
    r2 = jnp.sum(before * oh2, axis=-1, keepdims=True)
    carry_ref[...] += jnp.sum(both, axis=0, keepdims=True)

    meta = jnp.zeros((tm, LANES), F32)
    for ln, val in ((META_E, e1), (META_E + 1, e2), (META_W, w1), (META_W + 1, w2),
                    (META_R, r1), (META_R + 1, r2)):
        meta = jnp.where(lane == ln, val, meta)
    meta_ref[...] = meta
    cnt_ref[...] = carry_ref[...].astype(cnt_ref.dtype)


def _router(x, shift, scale, g, w_router, b_router, tm):
    t, d = x.shape
    nb = tm // ADA_BLOCK
    tri = (jnp.arange(tm)[:, None] > jnp.arange(tm)[None, :]).astype(BF16)
    w_hi = w_router.astype(BF16)
    w_lo = (w_router - w_hi.astype(F32)).astype(BF16)
    return pl.pallas_call(
        _router_kernel,
        grid=(t // tm,),
        in_specs=[pl.BlockSpec((tm, d), lambda i: (i, 0)),
                  pl.BlockSpec((nb, d), lambda i: (i, 0)),
                  pl.BlockSpec((nb, d), lambda i: (i, 0)),
                  pl.BlockSpec((1, d), lambda i: (0, 0)),
                  pl.BlockSpec((d, LANES), lambda i: (0, 0)),
                  pl.BlockSpec((d, LANES), lambda i: (0, 0)),
                  pl.BlockSpec((1, LANES), lambda i: (0, 0)),
                  pl.BlockSpec((tm, tm), lambda i: (0, 0))],
        out_specs=[pl.BlockSpec((tm, d), lambda i: (i, 0)),
                   pl.BlockSpec((tm, LANES), lambda i: (i, 0)),
                   pl.BlockSpec((1, LANES), lambda i: (0, 0))],
        out_shape=[jax.ShapeDtypeStruct((t, d), F32),
                   jax.ShapeDtypeStruct((t, LANES), F32),
                   jax.ShapeDtypeStruct((1, LANES), jnp.int32)],
        scratch_shapes=[pltpu.VMEM((1, LANES), F32)],
        compiler_params=_params(("arbitrary",)),
        name="moe_router",
    )(x, shift, scale, g.reshape(1, d), w_hi, w_lo, b_router, tri)


def _row_copy(src_ref, dst_ref, sem, src_row, dst_row):
    return pltpu.make_async_copy(src_ref.at[pl.ds(src_row, 1)], dst_ref.at[pl.ds(dst_row, 1)], sem)


def _wait_rows(hbm_ref, sem, n):
    rows = hbm_ref.at[pl.ds(0, n)]
    pltpu.make_async_copy(rows, rows, sem).wait()


DMA_UNROLL = 8


def _dispatch_kernel(pos_ref, h_ref, xs_ref, sem):
    n = h_ref.shape[0]

    def start(r, carry):
        for k in range(2):
            _row_copy(h_ref, xs_ref, sem, r, pos_ref[0, 0, 2 * r + k]).start(priority=k)
        return carry
    lax.fori_loop(0, n, start, 0, unroll=DMA_UNROLL)
    _wait_rows(xs_ref, sem, 2 * n)


def _dispatch(h, pos, ch):
    t, d = h.shape
    return pl.pallas_call(
        _dispatch_kernel,
        grid=(t // ch,),
        in_specs=[pl.BlockSpec((1, 1, 2 * ch), lambda i: (i, 0, 0), memory_space=pltpu.SMEM),
                  pl.BlockSpec((ch, d), lambda i: (i, 0))],
        out_specs=pl.BlockSpec(memory_space=pl.ANY),
        out_shape=jax.ShapeDtypeStruct((2 * t, d), F32),
        scratch_shapes=[pltpu.SemaphoreType.DMA(())],
        compiler_params=_params(("arbitrary",)),
        name="moe_dispatch",
    )(pos.reshape(t // ch, 1, 2 * ch), h)


def _cast_rows(src_ref, dst_ref, rows):
    def body(c, carry):
        r = pl.ds(pl.multiple_of(c * rows, rows), rows)
        dst_ref[r, :] = src_ref[r, :].astype(dst_ref.dtype)
        return carry
    lax.fori_loop(0, src_ref.shape[0] // rows, body, 0)


def _expert_kernel(vt_ref, ve_ref, vlo_ref, vhi_ref, x_ref, wg_ref, wu_ref, wd_ref, o_ref,
                   wgb_ref, wub_ref, wdb_ref):
    v = pl.program_id(0)
    lo, hi = vlo_ref[v], vhi_ref[v]
    prev = jnp.maximum(v - 1, 0)

    @pl.when(jnp.logical_and(hi > lo, jnp.logical_or(v == 0, ve_ref[prev] != ve_ref[v])))
    def _():
        _cast_rows(wg_ref, wgb_ref, 64)
        _cast_rows(wu_ref, wub_ref, 64)
        _cast_rows(wd_ref, wdb_ref, 16)

    @pl.when(hi > lo)
    def _():
        x = x_ref[...].astype(BF16)
        a = _dot(x, wgb_ref[...])
        b = _dot(x, wub_ref[...])
        y = _dot((_silu(a) * b).astype(BF16), wdb_ref[...])
        row = lax.broadcasted_iota(jnp.int32, (x.shape[0], 1), 0)
        mine = jnp.logical_and(row >= lo, row < hi)
        first = jnp.logical_or(v == 0, vt_ref[prev] != vt_ref[v])

        @pl.when(first)
        def _():
            o_ref[...] = jnp.where(mine, y, 0.0)

        @pl.when(jnp.logical_not(first))
        def _():
            o_ref[...] = jnp.where(mine, y, o_ref[...])


def _expert_mlp(xs, visits, layer, wg, wu, wd):
    p, d = xs.shape
    de = wg.shape[3]
    tm = EXPERT_TILE
    n_visits = visits[0].shape[0]
    return pl.pallas_call(
        _expert_kernel,
        grid_spec=pltpu.PrefetchScalarGridSpec(
            num_scalar_prefetch=4,
            grid=(n_visits,),
            in_specs=[pl.BlockSpec((tm, d), lambda v, vt, ve, lo, hi: (vt[v], 0)),
                      pl.BlockSpec((None, None, d, de), lambda v, vt, ve, lo, hi: (layer, ve[v], 0, 0)),
                      pl.BlockSpec((None, None, d, de), lambda v, vt, ve, lo, hi: (layer, ve[v], 0, 0)),
                      pl.BlockSpec((None, None, de, d), lambda v, vt, ve, lo, hi: (layer, ve[v], 0, 0))],
            out_specs=pl.BlockSpec((tm, d), lambda v, vt, ve, lo, hi: (vt[v], 0)),
            scratch_shapes=[pltpu.VMEM((d, de), BF16), pltpu.VMEM((d, de), BF16), pltpu.VMEM((de, d), BF16)]),
        out_shape=jax.ShapeDtypeStruct((p, d), F32),
        compiler_params=_params(("arbitrary",)),
        name="moe_experts",
    )(*visits, xs, wg, wu, wd)


def _combine_kernel(n_first, with_next, pos_ref, npos_ref, x_ref, gate_ref, meta_ref, ys_ref, *refs):
    if with_next:
        nshift_ref, nscale_ref, ng_ref = refs[:3]
        refs = refs[3:]
    o_refs, (ybuf, sems) = refs[:-2], refs[-2:]
    i = pl.program_id(0)
    n = pl.num_programs(0)
    tm = x_ref.shape[0]

    def fetch(p_ref, slot):
        def body(r, carry):
            for k in range(2):
                _row_copy(ys_ref, ybuf.at[slot, k], sems.at[slot], p_ref[0, 0, 2 * r + k], r).start(priority=k)
            return carry
        lax.fori_loop(0, tm, body, 0, unroll=DMA_UNROLL)

    @pl.when(i == 0)
    def _():
        fetch(pos_ref, 0)

    @pl.when(i + 1 < n)
    def _():
        fetch(npos_ref, (i + 1) % 2)

    slot = i % 2

    _wait_rows(ys_ref, sems.at[slot], 2 * tm)

    def write(o_ref):
        def body(r, carry):
            rows = pl.ds(pl.multiple_of(r * ADA_BLOCK, ADA_BLOCK), ADA_BLOCK)
            meta = meta_ref[rows, :]
            y = (meta[:, META_W:META_W + 1] * ybuf[slot, 0, rows, :]
                 + meta[:, META_W + 1:META_W + 2] * ybuf[slot, 1, rows, :])
            x_new = x_ref[rows, :] + gate_ref[pl.ds(r, 1), :] * y
            o_ref[rows, :] = x_new
            if with_next:
                h = x_new * lax.rsqrt(jnp.mean(x_new * x_new, axis=-1, keepdims=True) + EPS) * ng_ref[...]
                h = h * (1.0 + nscale_ref[pl.ds(r, 1), :]) + nshift_ref[pl.ds(r, 1), :]
                o_refs[1][rows, :] = h.astype(o_refs[1].dtype)
            return carry
        lax.fori_loop(0, tm // ADA_BLOCK, body, 0)

    if n_first is None:
        write(o_refs[0])
    else:
        pl.when(i < n_first)(lambda: write(o_refs[0]))
        pl.when(i >= n_first)(lambda: write(o_refs[1]))


def _combine(x, gate, meta, ys, pos, tm, split_rows=None, next_mod=None):
    t, d = x.shape
    nb = tm // ADA_BLOCK
    n = t // tm
    pos3 = pos.reshape(n, 1, 2 * tm)
    extra_specs, extra_args = [], []
    if next_mod is not None:
        assert split_rows is None
        n_first = None
        extra_specs = [pl.BlockSpec((nb, d), lambda i: (i, 0)), pl.BlockSpec((nb, d), lambda i: (i, 0)),
                       pl.BlockSpec((1, d), lambda i: (0, 0))]
        extra_args = [next_mod[0], next_mod[1], next_mod[2].reshape(1, d)]
        out_specs = [pl.BlockSpec((tm, d), lambda i: (i, 0)), pl.BlockSpec((tm, d), lambda i: (i, 0))]
        out_shape = [jax.ShapeDtypeStruct((t, d), F32), jax.ShapeDtypeStruct((t, d), BF16)]
    elif split_rows is None:
        n_first = None
        out_specs = pl.BlockSpec((tm, d), lambda i: (i, 0))
        out_shape = jax.ShapeDtypeStruct((t, d), F32)
    else:
        n_first = split_rows // tm
        out_specs = [pl.BlockSpec((tm, d), lambda i: (jnp.minimum(i, n_first - 1), 0)),
                     pl.BlockSpec((tm, d), lambda i: (jnp.maximum(i - n_first, 0), 0))]
        out_shape = [jax.ShapeDtypeStruct((split_rows, d), F32), jax.ShapeDtypeStruct((t - split_rows, d), F32)]
    return pl.pallas_call(
        functools.partial(_combine_kernel, n_first, next_mod is not None),
        grid=(n,),
        in_specs=[pl.BlockSpec((1, 1, 2 * tm), lambda i: (i, 0, 0), memory_space=pltpu.SMEM),
                  pl.BlockSpec((1, 1, 2 * tm), lambda i: (jnp.minimum(i + 1, n - 1), 0, 0), memory_space=pltpu.SMEM),
                  pl.BlockSpec((tm, d), lambda i: (i, 0)),
                  pl.BlockSpec((nb, d), lambda i: (i, 0)),
                  pl.BlockSpec((tm, LANES), lambda i: (i, 0)),
                  pl.BlockSpec(memory_space=pl.ANY)] + extra_specs,
        out_specs=out_specs,
        out_shape=out_shape,
        scratch_shapes=[pltpu.VMEM((2, 2, tm, d), F32), pltpu.SemaphoreType.DMA((2,))],
        compiler_params=_params(("arbitrary",)),
        name="moe_combine",
    )(pos3, pos3, x, gate, meta, ys, *extra_args)


def _plan_kernel(n_tiles, cnt_ref, start_ref, vt_ref, ve_ref, vlo_ref, vhi_ref):
    tile = EXPERT_TILE
    n_visits = vt_ref.shape[0]

    def per_expert(e, carry):
        run, v = carry
        c = cnt_ref[0, e]
        end = run + c
        start_ref[e] = run
        first_tile = run // tile
        n_vis = jnp.where(c > 0, (end - 1) // tile - first_tile + 1, 0)

        def per_visit(j, _):
            t = first_tile + j
            vt_ref[v + j] = t
            ve_ref[v + j] = e
            vlo_ref[v + j] = jnp.maximum(run - t * tile, 0)
            vhi_ref[v + j] = jnp.minimum(end - t * tile, tile)
            return 0
        lax.fori_loop(0, n_vis, per_visit, 0)
        return end, v + n_vis

    _, n_real = lax.fori_loop(0, N_EXPERTS, per_expert, (jnp.int32(0), jnp.int32(0)))

    def trailing(v, _):
        vt_ref[v] = n_tiles - 1
        ve_ref[v] = ve_ref[jnp.maximum(n_real - 1, 0)]
        vlo_ref[v] = 0
        vhi_ref[v] = 0
        return 0
    lax.fori_loop(n_real, n_visits, trailing, 0)


def _moe_plan(counts, n_tiles):
    n_visits = n_tiles + N_EXPERTS - 1
    smem = pl.BlockSpec(memory_space=pltpu.SMEM)
    vis = jax.ShapeDtypeStruct((n_visits,), jnp.int32)
    return pl.pallas_call(
        functools.partial(_plan_kernel, n_tiles),
        in_specs=[smem],
        out_specs=[smem] * 5,
        out_shape=[jax.ShapeDtypeStruct((N_EXPERTS,), jnp.int32), vis, vis, vis, vis],
        name="moe_plan",
    )(counts)


def _hier_moe(x, shift, scale, gate, g, w_group, b_group, w_er, b_er, layer, wg, wu, wd, tm,
              split_rows=None, next_mod=None):
    t, d = x.shape
    w_router = jnp.concatenate([w_group, jnp.moveaxis(w_er, 0, 1).reshape(d, N_EXPERTS)], axis=1)
    b_router = jnp.concatenate([b_group, b_er.reshape(N_EXPERTS)])
    n_route = N_GROUPS + N_EXPERTS
    w_router = jnp.pad(w_router, ((0, 0), (0, LANES - n_route)))
    b_router = jnp.pad(b_router, (0, LANES - n_route)).reshape(1, LANES)
    h, meta, counts = _router(x, shift, scale, g, w_router, b_router, min(tm, ROUTER_TILE))

    starts, *visits = _moe_plan(counts, 2 * t // EXPERT_TILE)
    expert = meta[:, META_E:META_E + 2].astype(jnp.int32)
    rank = meta[:, META_R:META_R + 2].astype(jnp.int32)
    pos = jnp.sum(jnp.where(expert[:, :, None] == jnp.arange(N_EXPERTS), starts, 0), axis=-1) + rank

    xs = _dispatch(h, pos, min(tm, GATHER_CHUNK))
    ys = _expert_mlp(xs, visits, layer, wg, wu, wd)
    return _combine(x, gate, meta, ys, pos, min(tm, COMBINE_TILE), split_rows, next_mod)


def _cached_keys_kernel(past, seq, ck_ref, cv_ref, cki_ref, kn_ref, vn_ref, kin_ref,
                        k_ref, vt_ref, ki_ref, ks_ref, vs_ref, kis_ref):
    nkb, kb = k_ref.shape[0], k_ref.shape[1]
    n_keys = past + seq
    for cache_ref, new_ref, stage in ((ck_ref, kn_ref, ks_ref), (cv_ref, vn_ref, vs_ref), (cki_ref, kin_ref, kis_ref)):
        stage[0:past, :] = cache_ref[...]
        stage[past:n_keys, :] = new_ref[...]
        if nkb * kb > n_keys:
            stage[n_keys:, :] = jnp.zeros((nkb * kb - n_keys, stage.shape[1]), F32)
    for i in range(nkb):
        rows = slice(i * kb, (i + 1) * kb)
        k_ref[i] = ks_ref[rows, :].astype(k_ref.dtype)
        vt_ref[i] = vs_ref[rows, :].T.astype(vt_ref.dtype)
        ki_ref[i] = kis_ref[rows, :].astype(ki_ref.dtype)


def _cached_key_blocks(cache_k, cache_v, cache_ki, k_new, v_new, ki_new, kb):
    b, past, d_kv = cache_k.shape
    seq = k_new.shape[1]
    nkb = -(-(past + seq) // kb)
    per_stream = lambda a: pl.BlockSpec((None,) + a.shape[1:], lambda i: (i, 0, 0))
    blocked = lambda r, c: pl.BlockSpec((None, nkb, r, c), lambda i: (i, 0, 0, 0))
    args = (cache_k, cache_v, cache_ki, k_new, v_new, ki_new)
    return pl.pallas_call(
        functools.partial(_cached_keys_kernel, past, seq),
        grid=(b,),
        in_specs=[per_stream(a) for a in args],
        out_specs=[blocked(kb, d_kv), blocked(d_kv, kb), blocked(kb, IDX_DIM)],
        out_shape=[jax.ShapeDtypeStruct((b, nkb, kb, d_kv), BF16),
                   jax.ShapeDtypeStruct((b, nkb, d_kv, kb), BF16),
                   jax.ShapeDtypeStruct((b, nkb, kb, IDX_DIM), BF16)],
        scratch_shapes=[pltpu.VMEM((nkb * kb, d_kv), F32), pltpu.VMEM((nkb * kb, d_kv), F32),
                        pltpu.VMEM((nkb * kb, IDX_DIM), F32)],
        compiler_params=_params(("parallel",)),
        name="cached_key_blocks",
    )(*args)


def _key_blocks(k, vt_src, ki, n_pad, kb):
    b, s, _ = k.shape
    padk = lambda a: jnp.pad(a.astype(BF16), ((0, 0), (0, n_pad - s), (0, 0)))
    nkb = n_pad // kb
    k_blk = padk(k).reshape(b, nkb, kb, k.shape[2])
    vt_blk = jnp.swapaxes(padk(vt_src).reshape(b, nkb, kb, vt_src.shape[2]), 2, 3)
    ki_blk = padk(ki).reshape(b, nkb, kb, ki.shape[2])
    return k_blk, vt_blk, ki_blk


def kernel(x_prompt, x_sample, c_prompt, c_sample, cache_conv, state_ret, cache_k, cache_v, cache_kidx, norm_mix_g, norm_ffn_g, w_ada, b_ada, cr_w_in, conv_w, conv_b, conv_norm_g, conv_norm_b, ret_norm_g, cr_w_out, dsa_w_in, q_norm_g, k_norm_g, kidx_norm_g, dsa_w_out, moe_w_group, moe_b_group, moe_w_erouter, moe_b_erouter, moe_w_gate, moe_w_up, moe_w_down):
    bp, lp, d = x_prompt.shape
    bs, ls, _ = x_sample.shape
    tp, ts = bp * lp, bs * ls
    t = tp + ts
    depth = w_ada.shape[0]
    past = cache_k.shape[2]
    d_conv = conv_w.shape[2]
    d_ret = ret_norm_g.shape[1]
    d_q = dsa_w_out.shape[1]
    d_kv = cache_k.shape[3] * cache_k.shape[4]
    tm = _pick(math.gcd(tp, ts), (1024, 512, 256, 128))
    groups = ((0, bp, lp), (tp, bs, ls))

    x = jnp.concatenate([x_prompt.reshape(tp, d), x_sample.reshape(ts, d)], axis=0)

    c_all = jnp.concatenate([c_prompt, c_sample], axis=0)
    n_c = c_all.shape[0]
    c_all = jnp.pad(c_all, ((0, -n_c % 8), (0, 0)))
    ada = _ada(c_all, w_ada, b_ada)
    per_block = lambda a, n: jnp.broadcast_to(a[:, :, None, :], a.shape[:2] + (n, a.shape[2])).reshape(depth, -1, a.shape[2])
    ada_blk = jnp.concatenate([per_block(ada[:, :bp], lp // ADA_BLOCK),
                               per_block(ada[:, bp:bp + bs], ls // ADA_BLOCK)], axis=1)

    new_conv, new_ret, new_k, new_v, new_kidx = [], [], [], [], []
    h_pre = None

    def in_proj(w, name, w_extra=None):
        if h_pre is not None:
            return _premodulated_matmul(h_pre, w, tm, name, w_extra)
        return _modulated_matmul(x, sh1, sc1, norm_mix_g[i], w, tm, name, w_extra)

    for i in range(depth):
        sh1, sc1, g1, sh2, sc2, g2 = [ada_blk[i, :, m * d:(m + 1) * d] for m in range(6)]
        j = i // 2
        if i % 2 == 0:
            u = in_proj(cr_w_in[j].astype(BF16), "cr_in_proj")
            a_out, b_out, bufs, states = [], [], [], []
            for gi, (row0, batch, seq) in enumerate(groups):
                buf0 = jnp.zeros((batch, CONV_WIDTH - 1, d_conv), F32) if gi == 0 else cache_conv[j]
                st0 = jnp.zeros((batch,) + state_ret.shape[2:], F32) if gi == 0 else state_ret[j]
                a, nbuf = _conv_branch(u, row0, batch, seq, buf0, conv_w[j], conv_b[j],
                                       conv_norm_g[j], conv_norm_b[j], d_conv)
                bo, nst = _retention_branch(u, row0, batch, seq, st0, ret_norm_g[j], 2 * d_conv // d_ret)
                a_out.append(a); b_out.append(bo); bufs.append(nbuf); states.append(nst)
            new_conv.append(bufs)
            new_ret.append(states)
            w_out = cr_w_out[j].astype(BF16)
            x = _outproj_residual([a_out, b_out], [w_out[:d_conv], w_out[d_conv:]], x, g1, tm, "cr_out_proj")
        else:
            w_in = dsa_w_in[j].astype(BF16)
            n_main = d_q + 2 * d_kv + IDX_HEADS * IDX_DIM
            w_x = jnp.pad(w_in[:, n_main:], ((0, 0), (0, LANES - (w_in.shape[1] - n_main))))
            u, ux = in_proj(w_in[:, :n_main], "dsa_in_proj", w_extra=w_x)
            q, k, qidx, kidx, k_bf, ki_bf, vt_bf = _qk_norms(u, ux, q_norm_g[j], k_norm_g[j], kidx_norm_g[j],
                                                             tm, d_q, d_kv)
            v = u[:, d_q + d_kv:d_q + 2 * d_kv]
            widx = ux[:, IDX_DIM:IDX_DIM + IDX_HEADS]
            outs, ks, vs, kis = [], [], [], []
            for gi, (row0, batch, seq) in enumerate(groups):
                rows = slice(row0, row0 + batch * seq)
                kg = k[rows].reshape(batch, seq, d_kv)
                vg = v[rows].reshape(batch, seq, d_kv)
                kig = kidx[rows].reshape(batch, seq, IDX_DIM)
                ks.append(kg); vs.append(vg); kis.append(kig)
                g_past = 0 if gi == 0 else past
                n_keys = g_past + seq
                kb = 256 if n_keys >= 256 else 128
                n_pad = -(-n_keys // kb) * kb
                if g_past:
                    k_blk, vt_blk, ki_blk = _cached_key_blocks(
                        cache_k[j].reshape(batch, past, d_kv), cache_v[j].reshape(batch, past, d_kv),
                        cache_kidx[j], kg, vg, kig, kb)
                elif n_pad == n_keys and vt_bf.shape[2] == kb:
                    blocks = slice(row0 // kb, (row0 + batch * seq) // kb)
                    k_blk = k_bf[rows].reshape(batch, seq // kb, kb, d_kv)
                    vt_blk = vt_bf[blocks].reshape(batch, seq // kb, d_kv, kb)
                    ki_blk = ki_bf[rows].reshape(batch, seq // kb, kb, IDX_DIM)
                else:
                    k_blk, vt_blk, ki_blk = _key_blocks(kg, vg, kig, n_pad, kb)
                outs.append(_sparse_attention(q, qidx, widx[rows], k_blk, vt_blk, ki_blk,
                                              row0, batch, seq, g_past, n_keys, kb))
            new_k.append(ks); new_v.append(vs); new_kidx.append(kis)
            x = _outproj_residual([outs], [dsa_w_out[j].astype(BF16)], x, g1, tm, "dsa_out_proj")
        last = i == depth - 1
        next_mod = None if last else (ada_blk[i + 1, :, 0:d], ada_blk[i + 1, :, d:2 * d], norm_mix_g[i + 1])
        x = _hier_moe(x, sh2, sc2, g2, norm_ffn_g[i], moe_w_group[i], moe_b_group[i], moe_w_erouter[i],
                      moe_b_erouter[i], i, moe_w_gate, moe_w_up, moe_w_down, tm,
                      split_rows=tp if last else None, next_mod=next_mod)
        if not last:
            x, h_pre = x

    y_p, y_s = x
    kv_heads, hd = cache_k.shape[3], cache_k.shape[4]
    stack = lambda per_layer, gi, shape: jnp.stack([lay[gi].reshape(shape) for lay in per_layer])
    return (y_p.reshape(bp, lp, d), y_s.reshape(bs, ls, d),
            stack(new_conv, 0, (bp, CONV_WIDTH - 1, d_conv)), stack(new_conv, 1, (bs, CONV_WIDTH - 1, d_conv)),
            stack(new_ret, 0, (bp,) + state_ret.shape[2:]), stack(new_ret, 1, (bs,) + state_ret.shape[2:]),
            stack(new_k, 0, (bp, lp, kv_heads, hd)), stack(new_k, 1, (bs, ls, kv_heads, hd)),
            stack(new_v, 0, (bp, lp, kv_heads, hd)), stack(new_v, 1, (bs, ls, kv_heads, hd)),
            stack(new_kidx, 0, (bp, lp, IDX_DIM)), stack(new_kidx, 1, (bs, ls, IDX_DIM)))
```

```python
import functools
import math

import jax
import jax.numpy as jnp
from jax import lax
from jax.experimental import pallas as pl
from jax.experimental.pallas import tpu as pltpu

F32 = jnp.float32
BF16 = jnp.bfloat16

EPS = 1e-6
CHUNK = 64
ADA_BLOCK = 32
CONV_WIDTH = 31
RET_HEADS = 8
RET_DK = 128
ATT_HEADS = 16
ATT_KV_HEADS = 4
IDX_HEADS = 16
IDX_DIM = 64
TOPK_MAX = 256
N_GROUPS = 4
EXPERTS_PER_GROUP = 8
N_EXPERTS = N_GROUPS * EXPERTS_PER_GROUP
LANES = 128
VMEM_LIMIT = 56 * 1024 * 1024
BISECT_ITERS = 30
BISECT_ROUND = 5
MASKED_DIST = 1e30
LOG2E = math.log2(math.e)
EXPERT_TILE = 256
GATHER_CHUNK = 1024
ROUTER_TILE = 512
COMBINE_TILE = 256


def _params(sem, vmem=VMEM_LIMIT):
    return pltpu.CompilerParams(dimension_semantics=sem, vmem_limit_bytes=vmem)


def _pick(n, cands):
    for c in cands:
        if n % c == 0:
            return c
    raise ValueError(f"no tile in {cands} divides {n}")


def _dot(a, b):
    return jnp.dot(a, b, preferred_element_type=F32)


def _dot_nt(a, b):
    return lax.dot_general(a, b, (((1,), (1,)), ((), ())), preferred_element_type=F32)


def _dot_tn(a, b):
    return lax.dot_general(a, b, (((0,), (0,)), ((), ())), preferred_element_type=F32)


def _silu(x):
    return x * jax.nn.sigmoid(x)


def _ada_kernel(c_ref, w_ref, b_ref, o_ref):
    c = _silu(c_ref[...]).astype(BF16)
    o_ref[...] = _dot(c, w_ref[...].astype(BF16)) + b_ref[...]


def _ada(c_all, w_ada, b_ada):
    depth, d, n = w_ada.shape
    rows = c_all.shape[0]
    tn = _pick(n, (1024, 512, 256, 128))
    return pl.pallas_call(
        _ada_kernel,
        grid=(depth, n // tn),
        in_specs=[pl.BlockSpec((rows, d), lambda l, j: (0, 0)),
                  pl.BlockSpec((None, d, tn), lambda l, j: (l, 0, j)),
                  pl.BlockSpec((None, 1, tn), lambda l, j: (l, 0, j))],
        out_specs=pl.BlockSpec((None, rows, tn), lambda l, j: (l, 0, j)),
        out_shape=jax.ShapeDtypeStruct((depth, rows, n), F32),
        compiler_params=_params(("parallel", "parallel")),
        name="ada",
    )(c_all, w_ada, b_ada.reshape(depth, 1, n))


def _modulate_rows(x_ref, shift_ref, scale_ref, g_ref, store):
    nblk = x_ref.shape[0] // ADA_BLOCK

    def body(r, carry):
        rows = pl.ds(pl.multiple_of(r * ADA_BLOCK, ADA_BLOCK), ADA_BLOCK)
        x = x_ref[rows, :]
        y = x * lax.rsqrt(jnp.mean(x * x, axis=-1, keepdims=True) + EPS) * g_ref[...]
        y = y * (1.0 + scale_ref[pl.ds(r, 1), :]) + shift_ref[pl.ds(r, 1), :]
        store(rows, y)
        return carry

    lax.fori_loop(0, nblk, body, 0)


def _modmm_kernel(has_extra, x_ref, shift_ref, scale_ref, g_ref, w_ref, *refs):
    if has_extra:
        wx_ref, o_ref, ox_ref, h_ref = refs
    else:
        o_ref, h_ref = refs

    @pl.when(pl.program_id(1) == 0)
    def _():
        def store(rows, y):
            h_ref[rows, :] = y.astype(BF16)
        _modulate_rows(x_ref, shift_ref, scale_ref, g_ref, store)
        if has_extra:
            ox_ref[...] = _dot(h_ref[...], wx_ref[...])

    o_ref[...] = _dot(h_ref[...], w_ref[...])


def _premod_mm_kernel(has_extra, h_ref, w_ref, *refs):
    if has_extra:
        wx_ref, o_ref, ox_ref = refs

        @pl.when(pl.program_id(1) == 0)
        def _():
            ox_ref[...] = _dot(h_ref[...], wx_ref[...])
    else:
        (o_ref,) = refs
    o_ref[...] = _dot(h_ref[...], w_ref[...])


def _premodulated_matmul(h, w, tm, name, w_extra=None):
    t, d = h.shape
    n = w.shape[1]
    tn = _pick(n, (1024, 512, 256, 128))
    in_specs = [pl.BlockSpec((tm, d), lambda i, j: (i, 0)), pl.BlockSpec((d, tn), lambda i, j: (0, j))]
    out_specs = pl.BlockSpec((tm, tn), lambda i, j: (i, j))
    out_shape = jax.ShapeDtypeStruct((t, n), F32)
    args = [h, w]
    if w_extra is not None:
        nx = w_extra.shape[1]
        in_specs.append(pl.BlockSpec((d, nx), lambda i, j: (0, 0)))
        out_specs = [out_specs, pl.BlockSpec((tm, nx), lambda i, j: (i, 0))]
        out_shape = [out_shape, jax.ShapeDtypeStruct((t, nx), F32)]
        args.append(w_extra)
    return pl.pallas_call(
        functools.partial(_premod_mm_kernel, w_extra is not None),
        grid=(t // tm, n // tn),
        in_specs=in_specs,
        out_specs=out_specs,
        out_shape=out_shape,
        compiler_params=_params(("parallel", "arbitrary")),
        name=name,
    )(*args)


def _modulated_matmul(x, shift, scale, g, w, tm, name, w_extra=None):
    t, d = x.shape
    n = w.shape[1]
    tn = _pick(n, (1024, 512, 256, 128))
    nb = tm // ADA_BLOCK
    in_specs = [pl.BlockSpec((tm, d), lambda i, j: (i, 0)),
                pl.BlockSpec((nb, d), lambda i, j: (i, 0)),
                pl.BlockSpec((nb, d), lambda i, j: (i, 0)),
                pl.BlockSpec((1, d), lambda i, j: (0, 0)),
                pl.BlockSpec((d, tn), lambda i, j: (0, j))]
    out_specs = pl.BlockSpec((tm, tn), lambda i, j: (i, j))
    out_shape = jax.ShapeDtypeStruct((t, n), F32)
    args = [x, shift, scale, g.reshape(1, d), w]
    if w_extra is not None:
        nx = w_extra.shape[1]
        in_specs.append(pl.BlockSpec((d, nx), lambda i, j: (0, 0)))
        out_specs = [out_specs, pl.BlockSpec((tm, nx), lambda i, j: (i, 0))]
        out_shape = [out_shape, jax.ShapeDtypeStruct((t, nx), F32)]
        args.append(w_extra)
    return pl.pallas_call(
        functools.partial(_modmm_kernel, w_extra is not None),
        grid=(t // tm, n // tn),
        in_specs=in_specs,
        out_specs=out_specs,
        out_shape=out_shape,
        scratch_shapes=[pltpu.VMEM((tm, d), BF16)],
        compiler_params=_params(("parallel", "arbitrary")),
        name=name,
    )(*args)


def _outproj_kernel(n_in, tile_ranges, *refs):
    n_grp = len(tile_ranges)
    a_refs = [refs[p * n_grp:(p + 1) * n_grp] for p in range(n_in)]
    w_refs = refs[n_in * n_grp:n_in * n_grp + n_in]
    x_ref, gate_ref, o_ref, y_ref = refs[n_in * n_grp + n_in:]
    i = pl.program_id(0)
    for g, (lo, hi) in enumerate(tile_ranges):
        @pl.when(jnp.logical_and(i >= lo, i < hi))
        def _():
            y = _dot(a_refs[0][g][...], w_refs[0][...])
            for p in range(1, n_in):
                y = y + _dot(a_refs[p][g][...], w_refs[p][...])
            y_ref[...] = y
    nblk = x_ref.shape[0] // ADA_BLOCK

    def body(r, carry):
        rows = pl.ds(pl.multiple_of(r * ADA_BLOCK, ADA_BLOCK), ADA_BLOCK)
        o_ref[rows, :] = x_ref[rows, :] + gate_ref[pl.ds(r, 1), :] * y_ref[rows, :]
        return carry

    lax.fori_loop(0, nblk, body, 0)


def _outproj_residual(acts, ws, x, gate, tm, name):
    t, d = x.shape
    tn = _pick(d, (1024, 512, 256, 128))
    nb = tm // ADA_BLOCK
    n_in = len(acts)
    tile_ranges, lo = [], 0
    for a in acts[0]:
        tile_ranges.append((lo, lo + a.shape[0] // tm))
        lo = tile_ranges[-1][1]

    def group_spec(a, lo, hi):
        return pl.BlockSpec((tm, a.shape[1]), lambda i, j: (jnp.clip(i - lo, 0, hi - lo - 1), 0))

    in_specs = ([group_spec(a, *tile_ranges[g]) for piece in acts for g, a in enumerate(piece)]
                + [pl.BlockSpec((w.shape[0], tn), lambda i, j: (0, j)) for w in ws]
                + [pl.BlockSpec((tm, tn), lambda i, j: (i, j)),
                   pl.BlockSpec((nb, tn), lambda i, j: (i, j))])
    return pl.pallas_call(
        functools.partial(_outproj_kernel, n_in, tuple(tile_ranges)),
        grid=(t // tm, d // tn),
        in_specs=in_specs,
        out_specs=pl.BlockSpec((tm, tn), lambda i, j: (i, j)),
        out_shape=jax.ShapeDtypeStruct((t, d), F32),
        scratch_shapes=[pltpu.VMEM((tm, tn), F32)],
        compiler_params=_params(("parallel", "parallel")),
        name=name,
    )(*[a for piece in acts for a in piece], *ws, x, gate)


CONV_ROWS = 32
CONV_HIST = 32


def _conv_kernel(tl, val_ref, gate_ref, pval_ref, pgate_ref, buf_ref, w_ref, b_ref, ng_ref, nb_ref,
                 o_ref, nbuf_ref, up_ref, sh_ref):
    li = pl.program_id(1)
    hist = CONV_WIDTH - 1
    pad = CONV_HIST - hist
    glu = val_ref[...] * jax.nn.sigmoid(gate_ref[...])
    up_ref[CONV_HIST:CONV_HIST + tl, :] = glu
    up_ref[0:pad, :] = jnp.zeros((pad, up_ref.shape[1]), F32)

    @pl.when(li == 0)
    def _():
        up_ref[pad:CONV_HIST, :] = buf_ref[...]

    @pl.when(li > 0)
    def _():
        prev = pval_ref[...] * jax.nn.sigmoid(pgate_ref[...])
        up_ref[pad:CONV_HIST, :] = prev[pad:, :]

    n_sh = tl + CONV_HIST - 8
    for s in range(1, 8):
        sh_ref[s - 1, 0:n_sh, :] = up_ref[s:s + n_sh, :]

    def tap_rows(row):
        base, s = row - row % 8, row % 8
        src = up_ref if s == 0 else sh_ref.at[s - 1]
        return src[base:base + CONV_ROWS, :]

    for c in range(tl // CONV_ROWS):
        r0 = c * CONV_ROWS
        acc = jnp.zeros((CONV_ROWS, val_ref.shape[1]), F32)
        for j in range(CONV_WIDTH):
            acc = acc + tap_rows(r0 + pad + j) * w_ref[j:j + 1, :]
        acc = acc + b_ref[...]
        mu = jnp.mean(acc, axis=-1, keepdims=True)
        dlt = acc - mu
        y = dlt * lax.rsqrt(jnp.mean(dlt * dlt, axis=-1, keepdims=True) + EPS)
        y = y * ng_ref[...] + nb_ref[...]
        o_ref[r0:r0 + CONV_ROWS, :] = _silu(y).astype(o_ref.dtype)

    @pl.when(li == pl.num_programs(1) - 1)
    def _():
        nbuf_ref[...] = up_ref[CONV_HIST + tl - hist:CONV_HIST + tl, :]


def _conv_branch(u, row0, batch, seq, conv_buf, conv_w, conv_b, cn_g, cn_b, d_conv):
    tl = _pick(seq, (256, 128, 64, 32))
    nl = seq // tl
    hist = CONV_WIDTH - 1
    rb = row0 // tl
    pb = tl // CONV_HIST
    cur = lambda col: pl.BlockSpec((tl, d_conv), lambda b, l: (rb + b * nl + l, col))
    prev = lambda col: pl.BlockSpec(
        (CONV_HIST, d_conv), lambda b, l: (jnp.maximum((rb + b * nl + l) * pb - 1, 0), col))
    vec = pl.BlockSpec((1, d_conv), lambda b, l: (0, 0))
    return pl.pallas_call(
        functools.partial(_conv_kernel, tl),
        grid=(batch, nl),
        in_specs=[cur(0), cur(1), prev(0), prev(1),
                  pl.BlockSpec((None, hist, d_conv), lambda b, l: (b, 0, 0)),
                  pl.BlockSpec((CONV_WIDTH, d_conv), lambda b, l: (0, 0)),
                  vec, vec, vec],
        out_specs=[pl.BlockSpec((tl, d_conv), lambda b, l: (b * nl + l, 0)),
                   pl.BlockSpec((None, hist, d_conv), lambda b, l: (b, 0, 0))],
        out_shape=[jax.ShapeDtypeStruct((batch * seq, d_conv), BF16),
                   jax.ShapeDtypeStruct((batch, hist, d_conv), F32)],
        scratch_shapes=[pltpu.VMEM((CONV_HIST + tl, d_conv), F32),
                        pltpu.VMEM((7, CONV_HIST + tl, d_conv), F32)],
        compiler_params=_params(("parallel", "arbitrary")),
        name="conv_branch",
    )(u, u, u, u, conv_buf, conv_w, conv_b.reshape(1, -1), cn_g.reshape(1, -1), cn_b.reshape(1, -1))


def _retention_kernel(q_ref, k_ref, v_ref, g_ref, s0_ref, din_ref, dq_ref, dk_ref, dblk_ref, rg_ref,
                      o_ref, s_out_ref, s_ref):
    ci = pl.program_id(1)

    @pl.when(ci == 0)
    def _():
        s_ref[...] = s0_ref[...]

    dv = s_ref.shape[2]
    for h in range(RET_HEADS):
        cols = slice(h * dv, (h + 1) * dv)
        q = q_ref[:, cols].astype(BF16)
        k = k_ref[:, cols] * (RET_DK ** -0.5)
        v = v_ref[:, cols].astype(BF16)
        s_prev = s_ref[h]
        sc = _dot_nt(q, k.astype(BF16)) * din_ref[h]
        o = _dot(sc.astype(BF16), v) + _dot(q, s_prev.astype(BF16)) * dq_ref[h]
        kd = (k * dk_ref[h]).astype(BF16)
        s_ref[h] = s_prev * dblk_ref[h] + _dot_tn(kd, v)
        o = o * lax.rsqrt(jnp.mean(o * o, axis=-1, keepdims=True) + EPS) * rg_ref[:, cols]
        o_ref[:, cols] = (o * _silu(g_ref[:, cols])).astype(o_ref.dtype)

    @pl.when(ci == pl.num_programs(1) - 1)
    def _():
        s_out_ref[...] = s_ref[...]


def _retention_branch(u, row0, batch, seq, state0, ret_g, col_q):
    heads, dk, dv = state0.shape[1:]
    d_ret = heads * dv
    c = _pick(seq, (512, 256, 128, 64, 32))
    nc = seq // c
    rb = row0 // c
    lg = jnp.log1p(-(2.0 ** (-5.0 - jnp.arange(heads, dtype=F32))))
    pos = jnp.arange(c, dtype=F32)
    diff = pos[:, None] - pos[None, :]
    d_in = jnp.where(diff >= 0, jnp.exp(lg[:, None, None] * jnp.maximum(diff, 0.0)), 0.0)
    d_q = jnp.broadcast_to(jnp.exp(lg[:, None] * (pos[None, :] + 1.0))[:, :, None], (heads, c, dv))
    d_k = jnp.broadcast_to(jnp.exp(lg[:, None] * (c - 1.0 - pos[None, :]))[:, :, None], (heads, c, dk))
    d_blk = jnp.broadcast_to(jnp.exp(lg * c)[:, None, None], (heads, dk, dv))
    blk = lambda col: pl.BlockSpec((c, d_ret), lambda b, i: (rb + b * nc + i, col))
    const3 = lambda shape: pl.BlockSpec(shape, lambda b, i: (0, 0, 0))
    return pl.pallas_call(
        _retention_kernel,
        grid=(batch, nc),
        in_specs=[blk(col_q), blk(col_q + 1), blk(col_q + 2), blk(col_q + 3),
                  pl.BlockSpec((None, heads, dk, dv), lambda b, i: (b, 0, 0, 0)),
                  const3((heads, c, c)), const3((heads, c, dv)), const3((heads, c, dk)),
                  const3((heads, dk, dv)),
                  pl.BlockSpec((1, d_ret), lambda b, i: (0, 0))],
        out_specs=[pl.BlockSpec((c, d_ret), lambda b, i: (b * nc + i, 0)),
                   pl.BlockSpec((None, heads, dk, dv), lambda b, i: (b, 0, 0, 0))],
        out_shape=[jax.ShapeDtypeStruct((batch * seq, d_ret), BF16),
                   jax.ShapeDtypeStruct(state0.shape, F32)],
        scratch_shapes=[pltpu.VMEM((heads, dk, dv), F32)],
        compiler_params=_params(("parallel", "arbitrary")),
        name="retention_branch",
    )(u, u, u, u, state0, d_in, d_q, d_k, d_blk, ret_g.reshape(1, d_ret))


def _head_rms(x, g, hd):
    outs = []
    for h in range(x.shape[1] // hd):
        xh = x[:, h * hd:(h + 1) * hd]
        outs.append(xh * lax.rsqrt(jnp.mean(xh * xh, axis=-1, keepdims=True) + EPS) * g)
    return outs


def _qknorm_kernel(q_ref, k_ref, v_ref, qi_ref, ki_ref, qg_ref, kg_ref, kig_ref,
                   qo_ref, ko_ref, qio_ref, kio_ref, kb_ref, kib_ref, vt_ref):
    hd = qg_ref.shape[1]
    qio_ref[...] = qi_ref[...].astype(qio_ref.dtype)
    q_scale = hd ** -0.5 * LOG2E
    for h, qh in enumerate(_head_rms(q_ref[...], qg_ref[...], hd)):
        qo_ref[:, h * hd:(h + 1) * hd] = (qh * q_scale).astype(qo_ref.dtype)
    for h, kh in enumerate(_head_rms(k_ref[...], kg_ref[...], hd)):
        ko_ref[:, h * hd:(h + 1) * hd] = kh
        kb_ref[:, h * hd:(h + 1) * hd] = kh.astype(kb_ref.dtype)
    ki = ki_ref[:, :IDX_DIM]
    ki = ki * lax.rsqrt(jnp.mean(ki * ki, axis=-1, keepdims=True) + EPS) * kig_ref[...]
    kio_ref[...] = ki
    kib_ref[...] = ki.astype(kib_ref.dtype)
    blk = vt_ref.shape[2]
    for c in range(vt_ref.shape[0]):
        vt_ref[c] = v_ref[c * blk:(c + 1) * blk, :].T.astype(vt_ref.dtype)


def _qk_norms(u, ux, q_g, k_g, kidx_g, tm, d_q, d_kv):
    t = u.shape[0]
    hd = q_g.shape[0]
    d_qi = IDX_HEADS * IDX_DIM
    blk = min(tm, 256)
    return pl.pallas_call(
        _qknorm_kernel,
        grid=(t // tm,),
        in_specs=[pl.BlockSpec((tm, d_q), lambda i: (i, 0)),
                  pl.BlockSpec((tm, d_kv), lambda i: (i, d_q // d_kv)),
                  pl.BlockSpec((tm, d_kv), lambda i: (i, d_q // d_kv + 1)),
                  pl.BlockSpec((tm, d_qi), lambda i: (i, (d_q + 2 * d_kv) // d_qi)),
                  pl.BlockSpec((tm, LANES), lambda i: (i, 0)),
                  pl.BlockSpec((1, hd), lambda i: (0, 0)),
                  pl.BlockSpec((1, hd), lambda i: (0, 0)),
                  pl.BlockSpec((1, IDX_DIM), lambda i: (0, 0))],
        out_specs=[pl.BlockSpec((tm, d_q), lambda i: (i, 0)),
                   pl.BlockSpec((tm, d_kv), lambda i: (i, 0)),
                   pl.BlockSpec((tm, d_qi), lambda i: (i, 0)),
                   pl.BlockSpec((tm, IDX_DIM), lambda i: (i, 0)),
                   pl.BlockSpec((tm, d_kv), lambda i: (i, 0)),
                   pl.BlockSpec((tm, IDX_DIM), lambda i: (i, 0)),
                   pl.BlockSpec((tm // blk, d_kv, blk), lambda i: (i, 0, 0))],
        out_shape=[jax.ShapeDtypeStruct((t, d_q), BF16),
                   jax.ShapeDtypeStruct((t, d_kv), F32),
                   jax.ShapeDtypeStruct((t, d_qi), BF16),
                   jax.ShapeDtypeStruct((t, IDX_DIM), F32),
                   jax.ShapeDtypeStruct((t, d_kv), BF16),
                   jax.ShapeDtypeStruct((t, IDX_DIM), BF16),
                   jax.ShapeDtypeStruct((t // blk, d_kv, blk), BF16)],
        compiler_params=_params(("parallel",)),
        name="qk_norms",
    )(u, u, u, u, ux, q_g.reshape(1, hd), k_g.reshape(1, hd), kidx_g.reshape(1, IDX_DIM))


def _dsa_kernel(tq, rep, kb, past, n_keys, n_sel, q_ref, qi_ref, wt_ref, slope_ref, tri_ref,
                k_ref, vt_ref, ki_ref, o_ref, sc_ref, dist_ref, qg_ref, acc_ref, z_ref, rel_ref):
    t0 = pl.program_id(1) * tq
    hd = k_ref.shape[2] // ATT_KV_HEADS
    grp = ATT_HEADS // ATT_KV_HEADS
    wq = rep * tq
    n_adm_tile = jnp.minimum(((past + t0 + tq - 1) // CHUNK + 1) * CHUNK, n_keys)
    nkb = (n_adm_tile + kb - 1) // kb
    q_pos = past + t0 + lax.broadcasted_iota(jnp.int32, (1, wq), 1) % tq
    q_chunk = q_pos // CHUNK
    n_adm = jnp.minimum((q_chunk + 1) * CHUNK, n_keys).astype(F32)
    need = jnp.minimum(n_adm, float(n_sel))
    neg_inf = jnp.float32(-jnp.inf)
    w_t = wt_ref[...] * (IDX_HEADS ** -0.5 * IDX_DIM ** -0.5)

    def key_pos(i):
        return i * kb + lax.broadcasted_iota(jnp.int32, (kb, 1), 0)

    def admissible(i):
        kp = key_pos(i)
        return jnp.logical_and(kp // CHUNK <= q_chunk, kp < n_keys)

    def score_body(i, carry):
        lo, hi = carry
        ki = ki_ref[i]
        for h in range(IDX_HEADS):
            qi = jnp.concatenate([qi_ref[:, h * IDX_DIM:(h + 1) * IDX_DIM]] * rep, axis=0)
            rel_ref[h] = _dot_nt(ki, qi)
        acc = jnp.zeros((kb, wq), F32)
        for h in range(IDX_HEADS):
            acc = acc + jnp.maximum(rel_ref[h], 0.0) * w_t[h:h + 1, :]
        adm = admissible(i)
        sc_ref[i] = jnp.where(adm, acc, neg_inf)
        lo = jnp.minimum(lo, jnp.min(jnp.where(adm, acc, jnp.inf), axis=0, keepdims=True))
        hi = jnp.maximum(hi, jnp.max(jnp.where(adm, acc, neg_inf), axis=0, keepdims=True))
        return lo, hi

    lo, hi = lax.fori_loop(0, nkb, score_body,
                           (jnp.full((1, wq), jnp.inf, F32), jnp.full((1, wq), neg_inf, F32)))

    def count(pred):
        def body(i, acc):
            ones = jnp.where(pred(sc_ref[i]), 1.0, 0.0)
            parts = [ones[r:r + 8, :] for r in range(0, kb, 8)]
            while len(parts) > 1:
                parts = [parts[a] + parts[a + 1] for a in range(0, len(parts), 2)]
            return acc + parts[0]
        return jnp.sum(lax.fori_loop(0, nkb, body, jnp.zeros((8, wq), F32)), axis=0, keepdims=True)

    def bisect(_, carry):
        lo, hi, c_lo = carry
        mid = 0.5 * (lo + hi)
        c_mid = count(lambda s: s >= mid)
        ge = c_mid >= need
        return jnp.where(ge, mid, lo), jnp.where(ge, hi, mid), jnp.where(ge, c_mid, c_lo)

    def bisect_round(carry):
        rnd, lo, hi, c_lo = carry
        lo, hi, c_lo = lax.fori_loop(0, BISECT_ROUND, bisect, (lo, hi, c_lo))
        return rnd + 1, lo, hi, c_lo

    def unresolved(carry):
        rnd, _, _, c_lo = carry
        return jnp.logical_and(rnd < BISECT_ITERS // BISECT_ROUND, jnp.max(c_lo - need) > 0.0)

    _, lo, hi, c_lo = lax.while_loop(unresolved, bisect_round, (jnp.int32(0), lo, hi, n_adm))

    def write_dist(i, sel):
        dist = jnp.abs(q_pos - key_pos(i)).astype(F32)
        dist_ref[i] = jnp.where(sel, dist, MASKED_DIST)

    resolved = jnp.max(c_lo - need) <= 0.0

    @pl.when(resolved)
    def _():
        def body(i, carry):
            write_dist(i, sc_ref[i] >= lo)
            return carry
        lax.fori_loop(0, nkb, body, 0)

    @pl.when(jnp.logical_not(resolved))
    def _():
        n_above = count(lambda s: s > hi)
        room = need - n_above

        def body(i, seen):
            s = sc_ref[i]
            above = s > hi
            band = jnp.logical_and(s >= lo, jnp.logical_not(above))
            band_f = jnp.where(band, 1.0, 0.0)
            rank = _dot(tri_ref[...], band_f.astype(BF16)) + seen
            write_dist(i, jnp.logical_or(above, jnp.logical_and(band, rank <= room)))
            return seen + jnp.sum(band_f, axis=0, keepdims=True)
        lax.fori_loop(0, nkb, body, jnp.zeros((1, wq), F32))

    for g in range(ATT_KV_HEADS):
        qg_ref[g] = jnp.concatenate(
            [q_ref[:, (g * grp + r) * hd:(g * grp + r + 1) * hd] for r in range(grp)], axis=0)
    acc_ref[...] = jnp.zeros_like(acc_ref)

    def att_body(i, carry):
        ms, ls = carry
        dist = jnp.concatenate([dist_ref[i]] * (grp // rep), axis=1)
        new_ms, new_ls = [], []
        for g in range(ATT_KV_HEADS):
            z_ref[g] = _dot_nt(k_ref[i, :, g * hd:(g + 1) * hd], qg_ref[g])
        for g in range(ATT_KV_HEADS):
            z = z_ref[g] - slope_ref[g] * dist
            m_new = jnp.maximum(ms[g], jnp.max(z, axis=0, keepdims=True))
            alpha = jnp.exp2(ms[g] - m_new)
            p = jnp.exp2(z - m_new)
            new_ls.append(ls[g] * alpha + jnp.sum(p, axis=0, keepdims=True))
            new_ms.append(m_new)
            acc_ref[g] = acc_ref[g] * alpha + _dot(vt_ref[i, g * hd:(g + 1) * hd, :], p.astype(BF16))
        return tuple(new_ms), tuple(new_ls)

    _, ls = lax.fori_loop(
        0, nkb, att_body,
        (tuple(jnp.full((1, grp * tq), neg_inf, F32) for _ in range(ATT_KV_HEADS)),
         tuple(jnp.zeros((1, grp * tq), F32) for _ in range(ATT_KV_HEADS))))
    for g in range(ATT_KV_HEADS):
        out = (acc_ref[g] / ls[g]).T
        for r in range(grp):
            o_ref[:, (g * grp + r) * hd:(g * grp + r + 1) * hd] = out[r * tq:(r + 1) * tq, :].astype(o_ref.dtype)


def _sparse_attention(q, qidx, widx, k_blk, vt_blk, ki_blk, row0, batch, seq, past, n_keys, kb):
    d_q = q.shape[1]
    nkb_all = k_blk.shape[1]
    d_kv = k_blk.shape[3]
    tq = _pick(seq, (256, 128, 64, 32))
    nq = seq // tq
    rb = row0 // tq
    grp = ATT_HEADS // ATT_KV_HEADS
    rep = max(1, LANES // tq)
    wq = rep * tq
    n_sel = min(TOPK_MAX, n_keys // 4)
    widx_t = jnp.tile(jnp.swapaxes(widx.reshape(batch * nq, tq, IDX_HEADS), 1, 2), (1, 1, rep))
    slopes = LOG2E * 2.0 ** (-8.0 * jnp.arange(1, ATT_HEADS + 1, dtype=F32) / ATT_HEADS)
    slope_rows = jnp.repeat(slopes.reshape(ATT_KV_HEADS, grp), tq, axis=1).reshape(ATT_KV_HEADS, 1, grp * tq)
    tri = (jnp.arange(kb)[:, None] >= jnp.arange(kb)[None, :]).astype(BF16)
    return pl.pallas_call(
        functools.partial(_dsa_kernel, tq, rep, kb, past, n_keys, n_sel),
        grid=(batch, nq),
        in_specs=[pl.BlockSpec((tq, d_q), lambda b, i: (rb + b * nq + i, 0)),
                  pl.BlockSpec((tq, qidx.shape[1]), lambda b, i: (rb + b * nq + i, 0)),
                  pl.BlockSpec((None, IDX_HEADS, wq), lambda b, i: (b * nq + i, 0, 0)),
                  pl.BlockSpec((ATT_KV_HEADS, 1, grp * tq), lambda b, i: (0, 0, 0)),
                  pl.BlockSpec((kb, kb), lambda b, i: (0, 0)),
                  pl.BlockSpec((None, nkb_all, kb, d_kv), lambda b, i: (b, 0, 0, 0)),
                  pl.BlockSpec((None, nkb_all, d_kv, kb), lambda b, i: (b, 0, 0, 0)),
                  pl.BlockSpec((None, nkb_all, kb, IDX_DIM), lambda b, i: (b, 0, 0, 0))],
        out_specs=pl.BlockSpec((tq, d_q), lambda b, i: (b * nq + i, 0)),
        out_shape=jax.ShapeDtypeStruct((batch * seq, d_q), BF16),
        scratch_shapes=[pltpu.VMEM((nkb_all, kb, wq), F32), pltpu.VMEM((nkb_all, kb, wq), F32),
                        pltpu.VMEM((ATT_KV_HEADS, grp * tq, d_kv // ATT_KV_HEADS), BF16),
                        pltpu.VMEM((ATT_KV_HEADS, d_kv // ATT_KV_HEADS, grp * tq), F32),
                        pltpu.VMEM((ATT_KV_HEADS, kb, grp * tq), F32),
                        pltpu.VMEM((IDX_HEADS, kb, wq), F32)],
        compiler_params=_params(("parallel", "arbitrary")),
        name="sparse_attention",
    )(q, qidx, widx_t, slope_rows, tri, k_blk, vt_blk, ki_blk)


META_E, META_W, META_R = 0, 2, 4


def _router_kernel(x_ref, shift_ref, scale_ref, g_ref, whi_ref, wlo_ref, br_ref, tri_ref,
                   h_ref, meta_ref, cnt_ref, carry_ref):
    @pl.when(pl.program_id(0) == 0)
    def _():
        carry_ref[...] = jnp.zeros_like(carry_ref)

    def store(rows, y):
        h_ref[rows, :] = y
    _modulate_rows(x_ref, shift_ref, scale_ref, g_ref, store)

    h = h_ref[...]
    h_hi = h.astype(BF16)
    h_lo = (h - h_hi.astype(F32)).astype(BF16)
    logits = (_dot(h_hi, whi_ref[...]) + (_dot(h_hi, wlo_ref[...]) + _dot(h_lo, whi_ref[...]))) + br_ref[...]
    tm = logits.shape[0]
    lane = lax.broadcasted_iota(jnp.int32, (tm, LANES), 1).astype(F32)
    neg_inf = jnp.float32(-jnp.inf)

    def first_argmax(v):
        top = jnp.max(v, axis=-1, keepdims=True)
        return top, jnp.min(jnp.where(v == top, lane, float(LANES)), axis=-1, keepdims=True)

    is_group = lane < N_GROUPS
    gl = jnp.where(is_group, logits, neg_inf)
    g_top, g_sel = first_argmax(gl)
    g_w = 1.0 / jnp.sum(jnp.where(is_group, jnp.exp(gl - g_top), 0.0), axis=-1, keepdims=True)
    first = N_GROUPS + g_sel * EXPERTS_PER_GROUP
    el = jnp.where(jnp.logical_and(lane >= first, lane < first + EXPERTS_PER_GROUP), logits, neg_inf)
    v1, i1 = first_argmax(el)
    v2, i2 = first_argmax(jnp.where(lane == i1, neg_inf, el))
    e21 = jnp.exp(v2 - v1)
    w1 = g_w / (1.0 + e21)
    w2 = g_w * e21 / (1.0 + e21)
    e1 = i1 - N_GROUPS
    e2 = i2 - N_GROUPS

    oh1 = jnp.where(lane == e1, 1.0, 0.0)
    oh2 = jnp.where(lane == e2, 1.0, 0.0)
    both = oh1 + oh2
    before = _dot(tri_ref[...], both.astype(BF16)) + carry_ref[...]
    r1 = jnp.sum(before * oh1, axis=-1, keepdims=True)
    r2 = jnp.sum(before * oh2, axis=-1, keepdims=True)
    carry_ref[...] += jnp.sum(both, axis=0, keepdims=True)

    meta = jnp.zeros((tm, LANES), F32)
    for ln, val in ((META_E, e1), (META_E + 1, e2), (META_W, w1), (META_W + 1, w2),
                    (META_R, r1), (META_R + 1, r2)):
        meta = jnp.where(lane == ln, val, meta)
    meta_ref[...] = meta
    cnt_ref[...] = carry_ref[...].astype(cnt_ref.dtype)


def _router(x, shift, scale, g, w_router, b_router, tm):
    t, d = x.shape
    nb = tm // ADA_BLOCK
    tri = (jnp.arange(tm)[:, None] > jnp.arange(tm)[None, :]).astype(BF16)
    w_hi = w_router.astype(BF16)
    w_lo = (w_router - w_hi.astype(F32)).astype(BF16)
    return pl.pallas_call(
        _router_kernel,
        grid=(t // tm,),
        in_specs=[pl.BlockSpec((tm, d), lambda i: (i, 0)),
                  pl.BlockSpec((nb, d), lambda i: (i, 0)),
                  pl.BlockSpec((nb, d), lambda i: (i, 0)),
                  pl.BlockSpec((1, d), lambda i: (0, 0)),
                  pl.BlockSpec((d, LANES), lambda i: (0, 0)),
                  pl.BlockSpec((d, LANES), lambda i: (0, 0)),
                  pl.BlockSpec((1, LANES), lambda i: (0, 0)),
                  pl.BlockSpec((tm, tm), lambda i: (0, 0))],
        out_specs=[pl.BlockSpec((tm, d), lambda i: (i, 0)),
                   pl.BlockSpec((tm, LANES), lambda i: (i, 0)),
                   pl.BlockSpec((1, LANES), lambda i: (0, 0))],
        out_shape=[jax.ShapeDtypeStruct((t, d), F32),
                   jax.ShapeDtypeStruct((t, LANES), F32),
                   jax.ShapeDtypeStruct((1, LANES), jnp.int32)],
        scratch_shapes=[pltpu.VMEM((1, LANES), F32)],
        compiler_params=_params(("arbitrary",)),
        name="moe_router",
    )(x, shift, scale, g.reshape(1, d), w_hi, w_lo, b_router, tri)


def _row_copy(src_ref, dst_ref, sem, src_row, dst_row):
    return pltpu.make_async_copy(src_ref.at[pl.ds(src_row, 1)], dst_ref.at[pl.ds(dst_row, 1)], sem)


def _wait_rows(hbm_ref, sem, n):
    rows = hbm_ref.at[pl.ds(0, n)]
    pltpu.make_async_copy(rows, rows, sem).wait()


DMA_UNROLL = 8


def _dispatch_kernel(pos_ref, h_ref, xs_ref, sem):
    n = h_ref.shape[0]

    def start(r, carry):
        for k in range(2):
            _row_copy(h_ref, xs_ref, sem, r, pos_ref[0, 0, 2 * r + k]).start(priority=k)
        return carry
    lax.fori_loop(0, n, start, 0, unroll=DMA_UNROLL)
    _wait_rows(xs_ref, sem, 2 * n)


def _dispatch(h, pos, ch):
    t, d = h.shape
    return pl.pallas_call(
        _dispatch_kernel,
        grid=(t // ch,),
        in_specs=[pl.BlockSpec((1, 1, 2 * ch), lambda i: (i, 0, 0), memory_space=pltpu.SMEM),
                  pl.BlockSpec((ch, d), lambda i: (i, 0))],
        out_specs=pl.BlockSpec(memory_space=pl.ANY),
        out_shape=jax.ShapeDtypeStruct((2 * t, d), F32),
        scratch_shapes=[pltpu.SemaphoreType.DMA(())],
        compiler_params=_params(("arbitrary",)),
        name="moe_dispatch",
    )(pos.reshape(t // ch, 1, 2 * ch), h)


def _cast_rows(src_ref, dst_ref, rows):
    def body(c, carry):
        r = pl.ds(pl.multiple_of(c * rows, rows), rows)
        dst_ref[r, :] = src_ref[r, :].astype(dst_ref.dtype)
        return carry
    lax.fori_loop(0, src_ref.shape[0] // rows, body, 0)


def _expert_kernel(vt_ref, ve_ref, vlo_ref, vhi_ref, x_ref, wg_ref, wu_ref, wd_ref, o_ref,
                   wgb_ref, wub_ref, wdb_ref):
    v = pl.program_id(0)
    lo, hi = vlo_ref[v], vhi_ref[v]
    prev = jnp.maximum(v - 1, 0)

    @pl.when(jnp.logical_and(hi > lo, jnp.logical_or(v == 0, ve_ref[prev] != ve_ref[v])))
    def _():
        _cast_rows(wg_ref, wgb_ref, 64)
        _cast_rows(wu_ref, wub_ref, 64)
        _cast_rows(wd_ref, wdb_ref, 16)

    @pl.when(hi > lo)
    def _():
        x = x_ref[...].astype(BF16)
        a = _dot(x, wgb_ref[...])
        b = _dot(x, wub_ref[...])
        y = _dot((_silu(a) * b).astype(BF16), wdb_ref[...])
        row = lax.broadcasted_iota(jnp.int32, (x.shape[0], 1), 0)
        mine = jnp.logical_and(row >= lo, row < hi)
        first = jnp.logical_or(v == 0, vt_ref[prev] != vt_ref[v])

        @pl.when(first)
        def _():
            o_ref[...] = jnp.where(mine, y, 0.0)

        @pl.when(jnp.logical_not(first))
        def _():
            o_ref[...] = jnp.where(mine, y, o_ref[...])


def _expert_mlp(xs, visits, layer, wg, wu, wd):
    p, d = xs.shape
    de = wg.shape[3]
    tm = EXPERT_TILE
    n_visits = visits[0].shape[0]
    return pl.pallas_call(
        _expert_kernel,
        grid_spec=pltpu.PrefetchScalarGridSpec(
            num_scalar_prefetch=4,
            grid=(n_visits,),
            in_specs=[pl.BlockSpec((tm, d), lambda v, vt, ve, lo, hi: (vt[v], 0)),
                      pl.BlockSpec((None, None, d, de), lambda v, vt, ve, lo, hi: (layer, ve[v], 0, 0)),
                      pl.BlockSpec((None, None, d, de), lambda v, vt, ve, lo, hi: (layer, ve[v], 0, 0)),
                      pl.BlockSpec((None, None, de, d), lambda v, vt, ve, lo, hi: (layer, ve[v], 0, 0))],
            out_specs=pl.BlockSpec((tm, d), lambda v, vt, ve, lo, hi: (vt[v], 0)),
            scratch_shapes=[pltpu.VMEM((d, de), BF16), pltpu.VMEM((d, de), BF16), pltpu.VMEM((de, d), BF16)]),
        out_shape=jax.ShapeDtypeStruct((p, d), F32),
        compiler_params=_params(("arbitrary",)),
        name="moe_experts",
    )(*visits, xs, wg, wu, wd)


def _combine_kernel(n_first, with_next, pos_ref, npos_ref, x_ref, gate_ref, meta_ref, ys_ref, *refs):
    if with_next:
        nshift_ref, nscale_ref, ng_ref = refs[:3]
        refs = refs[3:]
    o_refs, (ybuf, sems) = refs[:-2], refs[-2:]
    i = pl.program_id(0)
    n = pl.num_programs(0)
    tm = x_ref.shape[0]

    def fetch(p_ref, slot):
        def body(r, carry):
            for k in range(2):
                _row_copy(ys_ref, ybuf.at[slot, k], sems.at[slot], p_ref[0, 0, 2 * r + k], r).start(priority=k)
            return carry
        lax.fori_loop(0, tm, body, 0, unroll=DMA_UNROLL)

    @pl.when(i == 0)
    def _():
        fetch(pos_ref, 0)

    @pl.when(i + 1 < n)
    def _():
        fetch(npos_ref, (i + 1) % 2)

    slot = i % 2

    _wait_rows(ys_ref, sems.at[slot], 2 * tm)

    def write(o_ref):
        def body(r, carry):
            rows = pl.ds(pl.multiple_of(r * ADA_BLOCK, ADA_BLOCK), ADA_BLOCK)
            meta = meta_ref[rows, :]
            y = (meta[:, META_W:META_W + 1] * ybuf[slot, 0, rows, :]
                 + meta[:, META_W + 1:META_W + 2] * ybuf[slot, 1, rows, :])
            x_new = x_ref[rows, :] + gate_ref[pl.ds(r, 1), :] * y
            o_ref[rows, :] = x_new
            if with_next:
                h = x_new * lax.rsqrt(jnp.mean(x_new * x_new, axis=-1, keepdims=True) + EPS) * ng_ref[...]
                h = h * (1.0 + nscale_ref[pl.ds(r, 1), :]) + nshift_ref[pl.ds(r, 1), :]
                o_refs[1][rows, :] = h.astype(o_refs[1].dtype)
            return carry
        lax.fori_loop(0, tm // ADA_BLOCK, body, 0)

    if n_first is None:
        write(o_refs[0])
    else:
        pl.when(i < n_first)(lambda: write(o_refs[0]))
        pl.when(i >= n_first)(lambda: write(o_refs[1]))


def _combine(x, gate, meta, ys, pos, tm, split_rows=None, next_mod=None):
    t, d = x.shape
    nb = tm // ADA_BLOCK
    n = t // tm
    pos3 = pos.reshape(n, 1, 2 * tm)
    extra_specs, extra_args = [], []
    if next_mod is not None:
        assert split_rows is None
        n_first = None
        extra_specs = [pl.BlockSpec((nb, d), lambda i: (i, 0)), pl.BlockSpec((nb, d), lambda i: (i, 0)),
                       pl.BlockSpec((1, d), lambda i: (0, 0))]
        extra_args = [next_mod[0], next_mod[1], next_mod[2].reshape(1, d)]
        out_specs = [pl.BlockSpec((tm, d), lambda i: (i, 0)), pl.BlockSpec((tm, d), lambda i: (i, 0))]
        out_shape = [jax.ShapeDtypeStruct((t, d), F32), jax.ShapeDtypeStruct((t, d), BF16)]
    elif split_rows is None:
        n_first = None
        out_specs = pl.BlockSpec((tm, d), lambda i: (i, 0))
        out_shape = jax.ShapeDtypeStruct((t, d), F32)
    else:
        n_first = split_rows // tm
        out_specs = [pl.BlockSpec((tm, d), lambda i: (jnp.minimum(i, n_first - 1), 0)),
                     pl.BlockSpec((tm, d), lambda i: (jnp.maximum(i - n_first, 0), 0))]
        out_shape = [jax.ShapeDtypeStruct((split_rows, d), F32), jax.ShapeDtypeStruct((t - split_rows, d), F32)]
    return pl.pallas_call(
        functools.partial(_combine_kernel, n_first, next_mod is not None),
        grid=(n,),
        in_specs=[pl.BlockSpec((1, 1, 2 * tm), lambda i: (i, 0, 0), memory_space=pltpu.SMEM),
                  pl.BlockSpec((1, 1, 2 * tm), lambda i: (jnp.minimum(i + 1, n - 1), 0, 0), memory_space=pltpu.SMEM),
                  pl.BlockSpec((tm, d), lambda i: (i, 0)),
                  pl.BlockSpec((nb, d), lambda i: (i, 0)),
                  pl.BlockSpec((tm, LANES), lambda i: (i, 0)),
                  pl.BlockSpec(memory_space=pl.ANY)] + extra_specs,
        out_specs=out_specs,
        out_shape=out_shape,
        scratch_shapes=[pltpu.VMEM((2, 2, tm, d), F32), pltpu.SemaphoreType.DMA((2,))],
        compiler_params=_params(("arbitrary",)),
        name="moe_combine",
    )(pos3, pos3, x, gate, meta, ys, *extra_args)


def _plan_kernel(n_tiles, cnt_ref, start_ref, vt_ref, ve_ref, vlo_ref, vhi_ref):
    tile = EXPERT_TILE
    n_visits = vt_ref.shape[0]

    def per_expert(e, carry):
        run, v = carry
        c = cnt_ref[0, e]
        end = run + c
        start_ref[e] = run
        first_tile = run // tile
        n_vis = jnp.where(c > 0, (end - 1) // tile - first_tile + 1, 0)

        def per_visit(j, _):
            t = first_tile + j
            vt_ref[v + j] = t
            ve_ref[v + j] = e
            vlo_ref[v + j] = jnp.maximum(run - t * tile, 0)
            vhi_ref[v + j] = jnp.minimum(end - t * tile, tile)
            return 0
        lax.fori_loop(0, n_vis, per_visit, 0)
        return end, v + n_vis

    _, n_real = lax.fori_loop(0, N_EXPERTS, per_expert, (jnp.int32(0), jnp.int32(0)))

    def trailing(v, _):
        vt_ref[v] = n_tiles - 1
        ve_ref[v] = ve_ref[jnp.maximum(n_real - 1, 0)]
        vlo_ref[v] = 0
        vhi_ref[v] = 0
        return 0
    lax.fori_loop(n_real, n_visits, trailing, 0)


def _moe_plan(counts, n_tiles):
    n_visits = n_tiles + N_EXPERTS - 1
    smem = pl.BlockSpec(memory_space=pltpu.SMEM)
    vis = jax.ShapeDtypeStruct((n_visits,), jnp.int32)
    return pl.pallas_call(
        functools.partial(_plan_kernel, n_tiles),
        in_specs=[smem],
        out_specs=[smem] * 5,
        out_shape=[jax.ShapeDtypeStruct((N_EXPERTS,), jnp.int32), vis, vis, vis, vis],
        name="moe_plan",
    )(counts)


def _hier_moe(x, shift, scale, gate, g, w_group, b_group, w_er, b_er, layer, wg, wu, wd, tm,
              split_rows=None, next_mod=None):
    t, d = x.shape
    w_router = jnp.concatenate([w_group, jnp.moveaxis(w_er, 0, 1).reshape(d, N_EXPERTS)], axis=1)
    b_router = jnp.concatenate([b_group, b_er.reshape(N_EXPERTS)])
    n_route = N_GROUPS + N_EXPERTS
    w_router = jnp.pad(w_router, ((0, 0), (0, LANES - n_route)))
    b_router = jnp.pad(b_router, (0, LANES - n_route)).reshape(1, LANES)
    h, meta, counts = _router(x, shift, scale, g, w_router, b_router, min(tm, ROUTER_TILE))

    starts, *visits = _moe_plan(counts, 2 * t // EXPERT_TILE)
    expert = meta[:, META_E:META_E + 2].astype(jnp.int32)
    rank = meta[:, META_R:META_R + 2].astype(jnp.int32)
    pos = jnp.sum(jnp.where(expert[:, :, None] == jnp.arange(N_EXPERTS), starts, 0), axis=-1) + rank

    xs = _dispatch(h, pos, min(tm, GATHER_CHUNK))
    ys = _expert_mlp(xs, visits, layer, wg, wu, wd)
    return _combine(x, gate, meta, ys, pos, min(tm, COMBINE_TILE), split_rows, next_mod)


def _cached_keys_kernel(past, seq, ck_ref, cv_ref, cki_ref, kn_ref, vn_ref, kin_ref,
                        k_ref, vt_ref, ki_ref, ks_ref, vs_ref, kis_ref):
    nkb, kb = k_ref.shape[0], k_ref.shape[1]
    n_keys = past + seq
    for cache_ref, new_ref, stage in ((ck_ref, kn_ref, ks_ref), (cv_ref, vn_ref, vs_ref), (cki_ref, kin_ref, kis_ref)):
        stage[0:past, :] = cache_ref[...]
        stage[past:n_keys, :] = new_ref[...]
        if nkb * kb > n_keys:
            stage[n_keys:, :] = jnp.zeros((nkb * kb - n_keys, stage.shape[1]), F32)
    for i in range(nkb):
        rows = slice(i * kb, (i + 1) * kb)
        k_ref[i] = ks_ref[rows, :].astype(k_ref.dtype)
        vt_ref[i] = vs_ref[rows, :].T.astype(vt_ref.dtype)
        ki_ref[i] = kis_ref[rows, :].astype(ki_ref.dtype)


def _cached_key_blocks(cache_k, cache_v, cache_ki, k_new, v_new, ki_new, kb):
    b, past, d_kv = cache_k.shape
    seq = k_new.shape[1]
    nkb = -(-(past + seq) // kb)
    per_stream = lambda a: pl.BlockSpec((None,) + a.shape[1:], lambda i: (i, 0, 0))
    blocked = lambda r, c: pl.BlockSpec((None, nkb, r, c), lambda i: (i, 0, 0, 0))
    args = (cache_k, cache_v, cache_ki, k_new, v_new, ki_new)
    return pl.pallas_call(
        functools.partial(_cached_keys_kernel, past, seq),
        grid=(b,),
        in_specs=[per_stream(a) for a in args],
        out_specs=[blocked(kb, d_kv), blocked(d_kv, kb), blocked(kb, IDX_DIM)],
        out_shape=[jax.ShapeDtypeStruct((b, nkb, kb, d_kv), BF16),
                   jax.ShapeDtypeStruct((b, nkb, d_kv, kb), BF16),
                   jax.ShapeDtypeStruct((b, nkb, kb, IDX_DIM), BF16)],
        scratch_shapes=[pltpu.VMEM((nkb * kb, d_kv), F32), pltpu.VMEM((nkb * kb, d_kv), F32),
                        pltpu.VMEM((nkb * kb, IDX_DIM), F32)],
        compiler_params=_params(("parallel",)),
        name="cached_key_blocks",
    )(*args)


def _key_blocks(k, vt_src, ki, n_pad, kb):
    b, s, _ = k.shape
    padk = lambda a: jnp.pad(a.astype(BF16), ((0, 0), (0, n_pad - s), (0, 0)))
    nkb = n_pad // kb
    k_blk = padk(k).reshape(b, nkb, kb, k.shape[2])
    vt_blk = jnp.swapaxes(padk(vt_src).reshape(b, nkb, kb, vt_src.shape[2]), 2, 3)
    ki_blk = padk(ki).reshape(b, nkb, kb, ki.shape[2])
    return k_blk, vt_blk, ki_blk


def kernel(x_prompt, x_sample, c_prompt, c_sample, cache_conv, state_ret, cache_k, cache_v, cache_kidx, norm_mix_g, norm_ffn_g, w_ada, b_ada, cr_w_in, conv_w, conv_b, conv_norm_g, conv_norm_b, ret_norm_g, cr_w_out, dsa_w_in, q_norm_g, k_norm_g, kidx_norm_g, dsa_w_out, moe_w_group, moe_b_group, moe_w_erouter, moe_b_erouter, moe_w_gate, moe_w_up, moe_w_down):
    bp, lp, d = x_prompt.shape
    bs, ls, _ = x_sample.shape
    tp, ts = bp * lp, bs * ls
    t = tp + ts
    depth = w_ada.shape[0]
    past = cache_k.shape[2]
    d_conv = conv_w.shape[2]
    d_ret = ret_norm_g.shape[1]
    d_q = dsa_w_out.shape[1]
    d_kv = cache_k.shape[3] * cache_k.shape[4]
    tm = _pick(math.gcd(tp, ts), (1024, 512, 256, 128))
    groups = ((0, bp, lp), (tp, bs, ls))

    x = jnp.concatenate([x_prompt.reshape(tp, d), x_sample.reshape(ts, d)], axis=0)

    c_all = jnp.concatenate([c_prompt, c_sample], axis=0)
    n_c = c_all.shape[0]
    c_all = jnp.pad(c_all, ((0, -n_c % 8), (0, 0)))
    ada = _ada(c_all, w_ada, b_ada)
    per_block = lambda a, n: jnp.broadcast_to(a[:, :, None, :], a.shape[:2] + (n, a.shape[2])).reshape(depth, -1, a.shape[2])
    ada_blk = jnp.concatenate([per_block(ada[:, :bp], lp // ADA_BLOCK),
                               per_block(ada[:, bp:bp + bs], ls // ADA_BLOCK)], axis=1)

    new_conv, new_ret, new_k, new_v, new_kidx = [], [], [], [], []
    h_pre = None

    def in_proj(w, name, w_extra=None):
        if h_pre is not None:
            return _premodulated_matmul(h_pre, w, tm, name, w_extra)
        return _modulated_matmul(x, sh1, sc1, norm_mix_g[i], w, tm, name, w_extra)

    for i in range(depth):
        sh1, sc1, g1, sh2, sc2, g2 = [ada_blk[i, :, m * d:(m + 1) * d] for m in range(6)]
        j = i // 2
        if i % 2 == 0:
            u = in_proj(cr_w_in[j].astype(BF16), "cr_in_proj")
            a_out, b_out, bufs, states = [], [], [], []
            for gi, (row0, batch, seq) in enumerate(groups):
                buf0 = jnp.zeros((batch, CONV_WIDTH - 1, d_conv), F32) if gi == 0 else cache_conv[j]
                st0 = jnp.zeros((batch,) + state_ret.shape[2:], F32) if gi == 0 else state_ret[j]
                a, nbuf = _conv_branch(u, row0, batch, seq, buf0, conv_w[j], conv_b[j],
                                       conv_norm_g[j], conv_norm_b[j], d_conv)
                bo, nst = _retention_branch(u, row0, batch, seq, st0, ret_norm_g[j], 2 * d_conv // d_ret)
                a_out.append(a); b_out.append(bo); bufs.append(nbuf); states.append(nst)
            new_conv.append(bufs)
            new_ret.append(states)
            w_out = cr_w_out[j].astype(BF16)
            x = _outproj_residual([a_out, b_out], [w_out[:d_conv], w_out[d_conv:]], x, g1, tm, "cr_out_proj")
        else:
            w_in = dsa_w_in[j].astype(BF16)
            n_main = d_q + 2 * d_kv + IDX_HEADS * IDX_DIM
            w_x = jnp.pad(w_in[:, n_main:], ((0, 0), (0, LANES - (w_in.shape[1] - n_main))))
            u, ux = in_proj(w_in[:, :n_main], "dsa_in_proj", w_extra=w_x)
            q, k, qidx, kidx, k_bf, ki_bf, vt_bf = _qk_norms(u, ux, q_norm_g[j], k_norm_g[j], kidx_norm_g[j],
                                                             tm, d_q, d_kv)
            v = u[:, d_q + d_kv:d_q + 2 * d_kv]
            widx = ux[:, IDX_DIM:IDX_DIM + IDX_HEADS]
            outs, ks, vs, kis = [], [], [], []
            for gi, (row0, batch, seq) in enumerate(groups):
                rows = slice(row0, row0 + batch * seq)
                kg = k[rows].reshape(batch, seq, d_kv)
                vg = v[rows].reshape(batch, seq, d_kv)
                kig = kidx[rows].reshape(batch, seq, IDX_DIM)
                ks.append(kg); vs.append(vg); kis.append(kig)
                g_past = 0 if gi == 0 else past
                n_keys = g_past + seq
                kb = 256 if n_keys >= 256 else 128
                n_pad = -(-n_keys // kb) * kb
                if g_past:
                    k_blk, vt_blk, ki_blk = _cached_key_blocks(
                        cache_k[j].reshape(batch, past, d_kv), cache_v[j].reshape(batch, past, d_kv),
                        cache_kidx[j], kg, vg, kig, kb)
                elif n_pad == n_keys and vt_bf.shape[2] == kb:
                    blocks = slice(row0 // kb, (row0 + batch * seq) // kb)
                    k_blk = k_bf[rows].reshape(batch, seq // kb, kb, d_kv)
                    vt_blk = vt_bf[blocks].reshape(batch, seq // kb, d_kv, kb)
                    ki_blk = ki_bf[rows].reshape(batch, seq // kb, kb, IDX_DIM)
                else:
                    k_blk, vt_blk, ki_blk = _key_blocks(kg, vg, kig, n_pad, kb)
                outs.append(_sparse_attention(q, qidx, widx[rows], k_blk, vt_blk, ki_blk,
                                              row0, batch, seq, g_past, n_keys, kb))
            new_k.append(ks); new_v.append(vs); new_kidx.append(kis)
            x = _outproj_residual([outs], [dsa_w_out[j].astype(BF16)], x, g1, tm, "dsa_out_proj")
        last = i == depth - 1
        next_mod = None if last else (ada_blk[i + 1, :, 0:d], ada_blk[i + 1, :, d:2 * d], norm_mix_g[i + 1])
        x = _hier_moe(x, sh2, sc2, g2, norm_ffn_g[i], moe_w_group[i], moe_b_group[i], moe_w_erouter[i],
                      moe_b_erouter[i], i, moe_w_gate, moe_w_up, moe_w_down, tm,
                      split_rows=tp if last else None, next_mod=next_mod)
        if not last:
            x, h_pre = x

    y_p, y_s = x
    kv_heads, hd = cache_k.shape[3], cache_k.shape[4]
    stack = lambda per_layer, gi, shape: jnp.stack([lay[gi].reshape(shape) for lay in per_layer])
    return (y_p.reshape(bp, lp, d), y_s.reshape(bs, ls, d),
            stack(new_conv, 0, (bp, CONV_WIDTH - 1, d_conv)), stack(new_conv, 1, (bs, CONV_WIDTH - 1, d_conv)),
            stack(new_ret, 0, (bp,) + state_ret.shape[2:]), stack(new_ret, 1, (bs,) + state_ret.shape[2:]),
            stack(new_k, 0, (bp, lp, kv_heads, hd)), stack(new_k, 1, (bs, ls, kv_heads, hd)),
            stack(new_v, 0, (bp, lp, kv_heads, hd)), stack(new_v, 1, (bs, ls, kv_heads, hd)),
            stack(new_kidx, 0, (bp, lp, IDX_DIM)), stack(new_kidx, 1, (bs, ls, IDX_DIM)))
```
